```python
import math
import jax, jax.numpy as jnp
from jax import lax
import numpy as np

D_MODEL = 1024
BATCH = 8
SEQ = 2048
DEPTH = 4

CTX_LEN = 256
GRID_W = 64
DIFF_HEADS = 4
DIFF_HD = 64
DIFF_VD = 2 * DIFF_HD
MLA_HEADS = 8
MLA_NOPE = 64
MLA_ROPE = 32
MLA_VD = 64
MLA_Q_RANK = 384
MLA_KV_RANK = 256
FOURIER_GROUPS = 4
FOURIER_GROUP_DIM = 128
N_BRANCHES = 3
D_FF = 2816
Q_BLOCK = 128
ROPE_BASE = 10000.0
RMS_EPS = 1e-6
N_MOD = 9

DIFF_Q_W = DIFF_HEADS * 2 * DIFF_HD
DIFF_V_W = DIFF_HEADS * DIFF_VD
FOURIER_W = FOURIER_GROUPS * FOURIER_GROUP_DIM
MLA_OUT_W = MLA_HEADS * MLA_VD
IN_SIZES = (DIFF_Q_W, DIFF_Q_W, DIFF_V_W, MLA_Q_RANK, MLA_KV_RANK, MLA_ROPE, FOURIER_W, N_BRANCHES * D_MODEL)
IN_WIDTH = sum(IN_SIZES)

kernel_name = 'hybrid_diff_mla_fourier_macaron_dit'


def rms_norm(x, g):
    x32 = x.astype(jnp.float32)
    y = x32 * lax.rsqrt(jnp.mean(x32 * x32, axis=-1, keepdims=True) + RMS_EPS)
    return (y * g.astype(jnp.float32)).astype(x.dtype)


def modulate(x, shift, scale):
    return x * (1 + scale) + shift


def swiglu(x, w_in, w_out):
    g, u = jnp.split(x @ w_in, 2, axis=-1)
    return (jax.nn.silu(g) * u) @ w_out


def axial_rope_tables(row, col, dim):
    nf = dim // 4
    freqs = jnp.power(ROPE_BASE, -jnp.arange(nf, dtype=jnp.float32) / nf)
    ar = row.astype(jnp.float32)[:, None] * freqs[None, :]
    ac = col.astype(jnp.float32)[:, None] * freqs[None, :]
    ang = jnp.concatenate([ar, ar, ac, ac], axis=-1)
    return jnp.cos(ang), jnp.sin(ang)


def axial_rope(x, cos, sin):
    x1, x2, x3, x4 = jnp.split(x, 4, axis=-1)
    rot = jnp.concatenate([-x2, x1, -x4, x3], axis=-1)
    return (x * cos + rot * sin).astype(x.dtype)


def sweep_query_blocks(fn, qs):
    b, h, n, _ = qs[0].shape
    nb = n // Q_BLOCK
    blocks = tuple(q.reshape(b, h, nb, Q_BLOCK, q.shape[-1]).transpose(2, 0, 1, 3, 4) for q in qs)
    out = lax.map(lambda qb: fn(*qb), blocks)
    return out.transpose(1, 2, 0, 3, 4).reshape(b, h, n, out.shape[-1])


def diff_attend(q1, q2, k1, k2, v, lam):
    scale = DIFF_HD ** -0.5
    p1 = jax.nn.softmax(jnp.einsum('bhqd,bhkd->bhqk', q1, k1).astype(jnp.float32) * scale, axis=-1)
    p2 = jax.nn.softmax(jnp.einsum('bhqd,bhkd->bhqk', q2, k2).astype(jnp.float32) * scale, axis=-1)
    w = p1 - lam.astype(jnp.float32) * p2
    return jnp.einsum('bhqk,bhkv->bhqv', w.astype(v.dtype), v)


def mla_attend(qn, qr, kn, kr, v):
    scale = (MLA_NOPE + MLA_ROPE) ** -0.5
    s = jnp.einsum('bhqd,bhkd->bhqk', qn, kn) + jnp.einsum('bhqr,bkr->bhqk', qr, kr)
    p = jax.nn.softmax(s.astype(jnp.float32) * scale, axis=-1)
    return jnp.einsum('bhqk,bhkv->bhqv', p.astype(v.dtype), v)


def split_heads(t, heads, dim):
    b, n, _ = t.shape
    return t.reshape(b, n, heads, dim).transpose(0, 2, 1, 3)


def merge_heads(t):
    b, h, n, d = t.shape
    return t.transpose(0, 2, 1, 3).reshape(b, n, h * d)


def split_proj(p):
    offsets = []
    acc = 0
    for s in IN_SIZES[:-1]:
        acc += s
        offsets.append(acc)
    return jnp.split(p, offsets, axis=-1)


def diff_qkv(pq, pk, pv):
    b, n, _ = pq.shape
    q = pq.reshape(b, n, DIFF_HEADS, 2, DIFF_HD).transpose(3, 0, 2, 1, 4)
    k = pk.reshape(b, n, DIFF_HEADS, 2, DIFF_HD).transpose(3, 0, 2, 1, 4)
    v = split_heads(pv, DIFF_HEADS, DIFF_VD)
    return q[0], q[1], k[0], k[1], v


def diff_out(o, subln_g, lam_init):
    return merge_heads(rms_norm(o, subln_g) * (1.0 - lam_init))


def mla_qkv(pcq, pckv, q_norm_g, w_uq, kv_norm_g, w_ukv):
    q = split_heads(rms_norm(pcq, q_norm_g) @ w_uq, MLA_HEADS, MLA_NOPE + MLA_ROPE)
    kv = split_heads(rms_norm(pckv, kv_norm_g) @ w_ukv, MLA_HEADS, MLA_NOPE + MLA_VD)
    return q[..., :MLA_NOPE], q[..., MLA_NOPE:], kv[..., :MLA_NOPE], kv[..., MLA_NOPE:]


def fourier_mix(pf):
    b, n, _ = pf.shape
    f = pf.reshape(b, n, FOURIER_GROUPS, FOURIER_GROUP_DIM).astype(jnp.float32)
    y = jnp.fft.fft2(f, axes=(1, 3), norm='ortho').real
    return y.reshape(b, n, FOURIER_W).astype(pf.dtype)


def merge_branches(pg, yd, ym, yf, w_bd, w_bm, w_bf, w_o):
    g = jax.nn.sigmoid(pg.reshape(pg.shape[:-1] + (N_BRANCHES, D_MODEL)))
    merged = g[..., 0, :] * (yd @ w_bd) + g[..., 1, :] * (ym @ w_bm) + g[..., 2, :] * (yf @ w_bf)
    return merged @ w_o


def setup_inputs(seed: int = 0) -> dict:
    key = jax.random.key(seed)
    ks = jax.random.split(key, 24)

    def nrm(k, shape, s):
        return jax.random.normal(k, shape, jnp.float32) * s

    def gain(k, shape):
        return 1.0 + 0.02 * jax.random.normal(k, shape, jnp.float32)

    return {
        'x': nrm(ks[0], (BATCH, SEQ, D_MODEL), 1.0),
        'c': nrm(ks[1], (BATCH, D_MODEL), 1.0),
        'ctx': nrm(ks[2], (BATCH, CTX_LEN, D_MODEL), 1.0),
        'c_ctx': nrm(ks[3], (D_MODEL,), 1.0),
        'ada_w': nrm(ks[4], (DEPTH, D_MODEL, N_MOD * D_MODEL), 0.5 * D_MODEL ** -0.5),
        'ada_b': nrm(ks[5], (DEPTH, N_MOD * D_MODEL), 0.02),
        'norm_g': gain(ks[6], (DEPTH, 3, D_MODEL)),
        'ffn_w_in': nrm(ks[7], (DEPTH, 2, D_MODEL, 2 * D_FF), D_MODEL ** -0.5),
        'ffn_w_out': nrm(ks[8], (DEPTH, 2, D_FF, D_MODEL), D_FF ** -0.5),
        'w_in': nrm(ks[9], (DEPTH, D_MODEL, IN_WIDTH), D_MODEL ** -0.5),
        'diff_lambda': nrm(ks[10], (DEPTH, 4, DIFF_HD), 0.1),
        'diff_subln_g': gain(ks[11], (DEPTH, DIFF_VD)),
        'mla_q_norm_g': gain(ks[12], (DEPTH, MLA_Q_RANK)),
        'mla_w_uq': nrm(ks[13], (DEPTH, MLA_Q_RANK, MLA_HEADS * (MLA_NOPE + MLA_ROPE)), MLA_Q_RANK ** -0.5),
        'mla_kv_norm_g': gain(ks[14], (DEPTH, MLA_KV_RANK)),
        'mla_w_ukv': nrm(ks[15], (DEPTH, MLA_KV_RANK, MLA_HEADS * (MLA_NOPE + MLA_VD)), MLA_KV_RANK ** -0.5),
        'w_branch_diff': nrm(ks[16], (DEPTH, DIFF_V_W, D_MODEL), DIFF_V_W ** -0.5),
        'w_branch_mla': nrm(ks[17], (DEPTH, MLA_OUT_W, D_MODEL), MLA_OUT_W ** -0.5),
        'w_branch_fourier': nrm(ks[18], (DEPTH, FOURIER_W, D_MODEL), FOURIER_W ** -0.5),
        'w_out': nrm(ks[19], (DEPTH, D_MODEL, D_MODEL), D_MODEL ** -0.5),
        'final_norm_g': gain(ks[20], (D_MODEL,)),
    }


def reference(x, c, ctx, c_ctx, ada_w, ada_b, norm_g, ffn_w_in, ffn_w_out, w_in, diff_lambda,
              diff_subln_g, mla_q_norm_g, mla_w_uq, mla_kv_norm_g, mla_w_ukv, w_branch_diff,
              w_branch_mla, w_branch_fourier, w_out, final_norm_g):
    n_lat = x.shape[1]
    rows = n_lat // GRID_W
    t_row = jnp.repeat(jnp.arange(rows), GRID_W)
    t_col = jnp.tile(jnp.arange(GRID_W), rows)
    cos_d, sin_d = axial_rope_tables(t_row, t_col, DIFF_HD)
    cos_m, sin_m = axial_rope_tables(t_row, t_col, MLA_ROPE)

    for l in range(DEPTH):
        last = l == DEPTH - 1
        mx = jnp.split((jax.nn.silu(c) @ ada_w[l] + ada_b[l])[:, None, :], N_MOD, axis=-1)
        mc = jnp.split(jax.nn.silu(c_ctx) @ ada_w[l] + ada_b[l], N_MOD, axis=-1)

        x = x + 0.5 * mx[2] * swiglu(modulate(rms_norm(x, norm_g[l, 0]), mx[0], mx[1]),
                                     ffn_w_in[l, 0], ffn_w_out[l, 0])
        ctx = ctx + 0.5 * mc[2] * swiglu(modulate(rms_norm(ctx, norm_g[l, 0]), mc[0], mc[1]),
                                         ffn_w_in[l, 0], ffn_w_out[l, 0])

        px = split_proj(modulate(rms_norm(x, norm_g[l, 1]), mx[3], mx[4]) @ w_in[l])
        pc = split_proj(modulate(rms_norm(ctx, norm_g[l, 1]), mc[3], mc[4]) @ w_in[l])

        lam_init = 0.8 - 0.6 * math.exp(-0.3 * l)
        lam = (jnp.exp(jnp.sum(diff_lambda[l, 0] * diff_lambda[l, 1]))
               - jnp.exp(jnp.sum(diff_lambda[l, 2] * diff_lambda[l, 3])) + lam_init)
        q1x, q2x, k1x, k2x, vx = diff_qkv(px[0], px[1], px[2])
        q1x, q2x, k1x, k2x = (axial_rope(t, cos_d, sin_d) for t in (q1x, q2x, k1x, k2x))
        q1c, q2c, k1c, k2c, vc = diff_qkv(pc[0], pc[1], pc[2])
        k1 = jnp.concatenate([k1c, k1x], axis=2)
        k2 = jnp.concatenate([k2c, k2x], axis=2)
        vd = jnp.concatenate([vc, vx], axis=2)
        od_x = sweep_query_blocks(lambda a, b: diff_attend(a, b, k1, k2, vd, lam), (q1x, q2x))
        yd_x = diff_out(od_x, diff_subln_g[l], lam_init)

        qnx, qrx, knx, vmx = mla_qkv(px[3], px[4], mla_q_norm_g[l], mla_w_uq[l], mla_kv_norm_g[l], mla_w_ukv[l])
        qrx = axial_rope(qrx, cos_m, sin_m)
        krx = axial_rope(px[5], cos_m, sin_m)
        qnc, qrc, knc, vmc = mla_qkv(pc[3], pc[4], mla_q_norm_g[l], mla_w_uq[l], mla_kv_norm_g[l], mla_w_ukv[l])
        krc = pc[5]
        kn = jnp.concatenate([knc, knx], axis=2)
        kr = jnp.concatenate([krc, krx], axis=1)
        vm = jnp.concatenate([vmc, vmx], axis=2)
        om_x = sweep_query_blocks(lambda a, b: mla_attend(a, b, kn, kr, vm), (qnx, qrx))
        ym_x = merge_heads(om_x)

        yf_x = fourier_mix(px[6])

        x = x + mx[5] * merge_branches(px[7], yd_x, ym_x, yf_x, w_branch_diff[l], w_branch_mla[l],
                                       w_branch_fourier[l], w_out[l])
        x = x + 0.5 * mx[8] * swiglu(modulate(rms_norm(x, norm_g[l, 2]), mx[6], mx[7]),
                                     ffn_w_in[l, 1], ffn_w_out[l, 1])

        if not last:
            od_c = diff_attend(q1c, q2c, k1c, k2c, vc, lam)
            yd_c = diff_out(od_c, diff_subln_g[l], lam_init)
            ym_c = merge_heads(mla_attend(qnc, qrc, knc, krc, vmc))
            yf_c = fourier_mix(pc[6])
            ctx = ctx + mc[5] * merge_branches(pc[7], yd_c, ym_c, yf_c, w_branch_diff[l], w_branch_mla[l],
                                               w_branch_fourier[l], w_out[l])
            ctx = ctx + 0.5 * mc[8] * swiglu(modulate(rms_norm(ctx, norm_g[l, 2]), mc[6], mc[7]),
                                             ffn_w_in[l, 1], ffn_w_out[l, 1])

    return rms_norm(x, final_norm_g)
```

```python
import functools
import math

import numpy as np
import jax
import jax.numpy as jnp
from jax import lax
from jax.experimental import pallas as pl
from jax.experimental.pallas import tpu as pltpu

F32 = jnp.float32
BF16 = jnp.bfloat16

GRID_W = 64
DIFF_HEADS = 4
DIFF_HD = 64
DIFF_VD = 2 * DIFF_HD
MLA_HEADS = 8
MLA_NOPE = 64
MLA_ROPE = 32
MLA_VD = 64
MLA_Q_RANK = 384
MLA_KV_RANK = 256
FOURIER_GROUPS = 4
FOURIER_GROUP_DIM = 128
N_BRANCHES = 3
ROPE_BASE = 10000.0
RMS_EPS = 1e-6
N_MOD = 9

DIFF_W = DIFF_HEADS * 2 * DIFF_HD
MLA_QK_W = MLA_HEADS * 128
MLA_V_W = MLA_HEADS * MLA_VD
FOURIER_W = FOURIER_GROUPS * FOURIER_GROUP_DIM

LANES = 128
FFN_CHUNK = 256
VMEM_LIMIT = 56 * 1024 * 1024
LOG2E = math.log2(math.e)


def _tile(n, pref):
    t = min(n, pref)
    while n % t:
        t //= 2
    return t


def _resident(shape):
    nd = len(shape)
    return pl.BlockSpec(shape, lambda *_: (0,) * nd, pipeline_mode=pl.Buffered(1))


def _params(n_axes):
    return pltpu.CompilerParams(dimension_semantics=("arbitrary",) * n_axes,
                                vmem_limit_bytes=VMEM_LIMIT)


def _norm_mod(x, g, shift, scale):
    y = x * lax.rsqrt(jnp.mean(x * x, axis=-1, keepdims=True) + RMS_EPS)
    return ((y * g) * (1.0 + scale) + shift).astype(BF16)


def _rms(x, g):
    return x * lax.rsqrt(jnp.mean(x * x, axis=-1, keepdims=True) + RMS_EPS) * g


def _dot(a, b):
    return jnp.dot(a, b, preferred_element_type=F32)


def _ada_body(s_ref, w_ref, b_ref, o_ref):
    s = s_ref[...]
    a = (s * jax.nn.sigmoid(s)).astype(BF16)
    o_ref[0] = _dot(a, w_ref[0].astype(BF16)) + b_ref[0]


def _ada_call(cond, ada_w, ada_b):
    depth, d, n = ada_w.shape
    rows = cond.shape[0]
    tn = _tile(n, 1024)
    return pl.pallas_call(
        _ada_body,
        grid=(depth, n // tn),
        in_specs=[pl.BlockSpec((rows, d), lambda l, j: (0, 0)),
                  pl.BlockSpec((1, d, tn), lambda l, j: (l, 0, j)),
                  pl.BlockSpec((1, 1, tn), lambda l, j: (l, 0, j))],
        out_specs=pl.BlockSpec((1, rows, tn), lambda l, j: (l, 0, j)),
        out_shape=jax.ShapeDtypeStruct((depth, rows, n), F32),
        compiler_params=_params(2),
        name="ada_mod",
    )(cond, ada_w, ada_b.reshape(depth, 1, n))


def _ffn_body(x_ref, mod_ref, g_ref, win_ref, wout_ref, *rest, mod0, nch, final):
    if final:
        fg_ref, o_ref = rest
    else:
        (o_ref,) = rest
    x = x_ref[0]
    xm = _norm_mod(x, g_ref[...], mod_ref[0, mod0:mod0 + 1, :], mod_ref[0, mod0 + 1:mod0 + 2, :])
    acc = jnp.zeros(x.shape, F32)
    for c in range(nch):
        h = _dot(xm, win_ref[c])
        gate, up = h[:, :FFN_CHUNK], h[:, FFN_CHUNK:]
        act = (gate * jax.nn.sigmoid(gate) * up).astype(BF16)
        acc = acc + _dot(act, wout_ref[c])
    y = x + (0.5 * mod_ref[0, mod0 + 2:mod0 + 3, :]) * acc
    if final:
        y = _rms(y, fg_ref[...])
    o_ref[0] = y


def _ffn_call(x, mod, g, win, wout, mod0, final_g=None, tm=512):
    s, t, d = x.shape
    nch = win.shape[0]
    tm = _tile(t, tm)
    final = final_g is not None
    in_specs = [pl.BlockSpec((1, tm, d), lambda i, j: (i, j, 0)),
                pl.BlockSpec((1, N_MOD, d), lambda i, j: (i, 0, 0)),
                pl.BlockSpec((1, d), lambda i, j: (0, 0)),
                _resident(win.shape), _resident(wout.shape)]
    args = [x, mod, g, win, wout]
    if final:
        in_specs.append(pl.BlockSpec((1, d), lambda i, j: (0, 0)))
        args.append(final_g)
    return pl.pallas_call(
        functools.partial(_ffn_body, mod0=mod0, nch=nch, final=final),
        grid=(s, t // tm),
        in_specs=in_specs,
        out_specs=pl.BlockSpec((1, tm, d), lambda i, j: (i, j, 0)),
        out_shape=jax.ShapeDtypeStruct(x.shape, F32),
        compiler_params=_params(2),
        name="ffn",
    )(*args)


def _rope(x, cos, sin_signed, chunk):
    lane = lax.broadcasted_iota(jnp.int32, x.shape, 1)
    even = (lane % (2 * chunk)) < chunk
    partner = jnp.where(even, pltpu.roll(x, LANES - chunk, 1), pltpu.roll(x, chunk, 1))
    return x * cos + partner * sin_signed


def _proj_body(x_ref, mod_ref, g_ref, w_ref, wuq_ref, wukv_ref, qn_ref, kvn_ref, cs_ref, tab_ref,
               qd_ref, kd_ref, vd_ref, qm_ref, km_ref, vm_ref, zc_ref, zs_ref, gt_ref,
               *, qscale_d, qscale_m):
    x = x_ref[0]
    xm = _norm_mod(x, g_ref[...], mod_ref[0, 3:4, :], mod_ref[0, 4:5, :])
    cos_d, sin_d, cos_m, sin_m = tab_ref[0], tab_ref[1], tab_ref[2], tab_ref[3]

    def mm(c0, width):
        return _dot(xm, w_ref[:, c0:c0 + width])

    c0 = 0
    q = mm(c0, DIFF_W)
    c0 += DIFF_W
    for h in range(DIFF_HEADS):
        sl = slice(h * LANES, (h + 1) * LANES)
        qd_ref[0, :, sl] = (_rope(q[:, sl], cos_d, sin_d, DIFF_HD // 4) * qscale_d).astype(BF16)
    k = mm(c0, DIFF_W)
    c0 += DIFF_W
    for h in range(DIFF_HEADS):
        sl = slice(h * LANES, (h + 1) * LANES)
        kd_ref[0, :, sl] = _rope(k[:, sl], cos_d, sin_d, DIFF_HD // 4).astype(BF16)
    vd_ref[0] = mm(c0, DIFF_W).astype(BF16)
    c0 += DIFF_W

    cq = _rms(mm(c0, MLA_Q_RANK), qn_ref[...]).astype(BF16)
    c0 += MLA_Q_RANK
    qm = _dot(cq, wuq_ref[...])
    for h in range(MLA_HEADS):
        sl = slice(h * LANES, (h + 1) * LANES)
        qm_ref[0, :, sl] = (_rope(qm[:, sl], cos_m, sin_m, MLA_ROPE // 4) * qscale_m).astype(BF16)

    ckv = _rms(mm(c0, MLA_KV_RANK), kvn_ref[...]).astype(BF16)
    c0 += MLA_KV_RANK
    kv = _dot(ckv, wukv_ref[...])
    kr = _rope(mm(c0, LANES), cos_m, sin_m, MLA_ROPE // 4)
    c0 += LANES
    for h in range(MLA_HEADS):
        sl = slice(h * LANES, (h + 1) * LANES)
        km_ref[0, :, sl] = (kv[:, sl] + kr).astype(BF16)
    vm_ref[0] = kv[:, MLA_QK_W:].astype(BF16)

    f = mm(c0, FOURIER_W).astype(BF16)
    c0 += FOURIER_W
    for gi in range(FOURIER_GROUPS):
        sl = slice(gi * LANES, (gi + 1) * LANES)
        z = _dot(f[:, sl], cs_ref[...])
        zc_ref[0, :, sl] = z[:, :LANES].astype(BF16)
        zs_ref[0, :, sl] = z[:, LANES:].astype(BF16)

    d = x.shape[-1]
    for bi in range(N_BRANCHES):
        gt_ref[0, :, bi * d:(bi + 1) * d] = jax.nn.sigmoid(mm(c0 + bi * d, d)).astype(BF16)


def _proj_call(x, mod, g, w, wuq, wukv, qn, kvn, cs128, tab, tab_per_tile, tm=512):
    s, t, d = x.shape
    tm = _tile(t, tm)
    widths = (DIFF_W, DIFF_W, DIFF_W, MLA_QK_W, MLA_QK_W, MLA_V_W, FOURIER_W, FOURIER_W, N_BRANCHES * d)
    tok = lambda wd: pl.BlockSpec((1, tm, wd), lambda i, j: (i, j, 0))
    tab_map = (lambda i, j: (0, j, 0)) if tab_per_tile else (lambda i, j: (0, 0, 0))
    return pl.pallas_call(
        functools.partial(_proj_body,
                          qscale_d=DIFF_HD ** -0.5 * LOG2E,
                          qscale_m=(MLA_NOPE + MLA_ROPE) ** -0.5 * LOG2E),
        grid=(s, t // tm),
        in_specs=[tok(d),
                  pl.BlockSpec((1, N_MOD, d), lambda i, j: (i, 0, 0)),
                  pl.BlockSpec((1, d), lambda i, j: (0, 0)),
                  _resident(w.shape), _resident(wuq.shape), _resident(wukv.shape),
                  _resident(qn.shape), _resident(kvn.shape), _resident(cs128.shape),
                  pl.BlockSpec((4, tm, LANES), tab_map)],
        out_specs=[tok(wd) for wd in widths],
        out_shape=[jax.ShapeDtypeStruct((s, t, wd), BF16) for wd in widths],
        compiler_params=_params(2),
        name="in_proj",
    )(x, mod, g, w, wuq, wukv, qn, kvn, cs128, tab)


def _gather_kv(j, srcs, scr):
    if len(srcs) == 1:
        return srcs[0][0]
    @pl.when(j == 0)
    def _():
        r0 = 0
        for ref in srcs:
            n = ref.shape[1]
            scr[r0:r0 + n, :] = ref[0]
            r0 += n
    return scr[...]


def _softmax_parts(s):
    m = jnp.max(s, axis=-1, keepdims=True)
    e = jnp.exp2(s - m)
    return e, jnp.sum(e, axis=-1, keepdims=True)


def _qk(q, k):
    return lax.dot_general(q, k, (((1,), (1,)), ((), ())), preferred_element_type=F32)


def _diff_body(*refs, n_src, lam_init):
    q_ref, lam_ref, sg_ref = refs[0], refs[1], refs[2]
    k_srcs = refs[3:3 + n_src]
    v_srcs = refs[3 + n_src:3 + 2 * n_src]
    o_ref = refs[3 + 2 * n_src]
    scr = refs[4 + 2 * n_src:]
    j = pl.program_id(1)
    k_all = _gather_kv(j, k_srcs, scr[0] if scr else None)
    v_all = _gather_kv(j, v_srcs, scr[1] if scr else None)

    lp = lam_ref[...]
    lam = (jnp.exp(jnp.sum(lp[0:1] * lp[1:2], axis=-1, keepdims=True))
           - jnp.exp(jnp.sum(lp[2:3] * lp[3:4], axis=-1, keepdims=True)) + lam_init)
    q = q_ref[0]
    lane = lax.broadcasted_iota(jnp.int32, (q.shape[0], LANES), 1)
    first = lane < DIFF_HD
    zero = jnp.zeros((), BF16)
    for h in range(DIFF_HEADS):
        sl = slice(h * LANES, (h + 1) * LANES)
        qh, kh, vh = q[:, sl], k_all[:, sl], v_all[:, sl]
        e1, l1 = _softmax_parts(_qk(jnp.where(first, qh, zero), kh))
        e2, l2 = _softmax_parts(_qk(jnp.where(first, zero, qh), kh))
        w = e1 * (1.0 / l1) - e2 * (lam / l2)
        o = _dot(w.astype(BF16), vh)
        o_ref[0, :, sl] = (_rms(o, sg_ref[...]) * (1.0 - lam_init)).astype(BF16)


def _mla_body(*refs, n_src):
    q_ref = refs[0]
    k_srcs = refs[1:1 + n_src]
    v_srcs = refs[1 + n_src:1 + 2 * n_src]
    o_ref = refs[1 + 2 * n_src]
    scr = refs[2 + 2 * n_src:]
    j = pl.program_id(1)
    k_all = _gather_kv(j, k_srcs, scr[0] if scr else None)
    v_all = _gather_kv(j, v_srcs, scr[1] if scr else None)
    q = q_ref[0]
    outs = []
    for h in range(MLA_HEADS):
        e, l = _softmax_parts(_qk(q[:, h * LANES:(h + 1) * LANES], k_all[:, h * LANES:(h + 1) * LANES]))
        o = _dot(e.astype(BF16), v_all[:, h * MLA_VD:(h + 1) * MLA_VD])
        outs.append(o * (1.0 / l))
        if h % 2 == 1:
            o_ref[0, :, (h - 1) * MLA_VD:(h + 1) * MLA_VD] = jnp.concatenate(outs, axis=-1).astype(BF16)
            outs = []


def _attn_call(body, name, q, ks, vs, extra, out_w, tq=256):
    b, lq, wq = q.shape
    tq = _tile(lq, tq)
    n_src = len(ks)
    lk = sum(k.shape[1] for k in ks)
    full = lambda a: pl.BlockSpec((1,) + a.shape[1:], lambda i, j: (i, 0, 0))
    in_specs = ([pl.BlockSpec((1, tq, wq), lambda i, j: (i, j, 0))]
                + [_resident(e.shape) for e in extra]
                + [full(k) for k in ks] + [full(v) for v in vs])
    scratch = []
    if n_src > 1:
        scratch = [pltpu.VMEM((lk, ks[0].shape[2]), BF16), pltpu.VMEM((lk, vs[0].shape[2]), BF16)]
    return pl.pallas_call(
        functools.partial(body, n_src=n_src),
        grid=(b, lq // tq),
        in_specs=in_specs,
        out_specs=pl.BlockSpec((1, tq, out_w), lambda i, j: (i, j, 0)),
        out_shape=jax.ShapeDtypeStruct((b, lq, out_w), BF16),
        scratch_shapes=scratch,
        compiler_params=_params(2),
        name=name,
    )(q, *extra, *ks, *vs)


def _dft_body(c_ref, s_ref, zc_ref, zs_ref, o_ref):
    o_ref[0] = (_dot(c_ref[...], zc_ref[0]) + _dot(s_ref[...], zs_ref[0])).astype(BF16)


def _dft_call(cmat, smat, zc, zs, tm=512):
    b, t, w = zc.shape
    tm = _tile(t, tm)
    return pl.pallas_call(
        _dft_body,
        grid=(t // tm, b),
        in_specs=[pl.BlockSpec((tm, t), lambda i, j: (i, 0)),
                  pl.BlockSpec((tm, t), lambda i, j: (i, 0)),
                  pl.BlockSpec((1, t, w), lambda i, j: (j, 0, 0)),
                  pl.BlockSpec((1, t, w), lambda i, j: (j, 0, 0))],
        out_specs=pl.BlockSpec((1, tm, w), lambda i, j: (j, i, 0)),
        out_shape=jax.ShapeDtypeStruct((b, t, w), BF16),
        compiler_params=_params(2),
        name="pos_dft",
    )(cmat, smat, zc, zs)


def _merge_body(x_ref, mod_ref, yd_ref, ym_ref, yf_ref, gt_ref, wd_ref, wm_ref, wf_ref, wo_ref, o_ref):
    x = x_ref[0]
    d = x.shape[-1]
    merged = (gt_ref[0, :, 0:d].astype(F32) * _dot(yd_ref[0], wd_ref[...])
              + gt_ref[0, :, d:2 * d].astype(F32) * _dot(ym_ref[0], wm_ref[...])
              + gt_ref[0, :, 2 * d:3 * d].astype(F32) * _dot(yf_ref[0], wf_ref[...]))
    o_ref[0] = x + mod_ref[0, 5:6, :] * _dot(merged.astype(BF16), wo_ref[...])


def _merge_call(x, mod, yd, ym, yf, gt, wd, wm, wf, wo, tm=512):
    s, t, d = x.shape
    tm = _tile(t, tm)
    tok = lambda wd_: pl.BlockSpec((1, tm, wd_), lambda i, j: (i, j, 0))
    return pl.pallas_call(
        _merge_body,
        grid=(s, t // tm),
        in_specs=[tok(d), pl.BlockSpec((1, N_MOD, d), lambda i, j: (i, 0, 0)),
                  tok(yd.shape[2]), tok(ym.shape[2]), tok(yf.shape[2]), tok(gt.shape[2]),
                  _resident(wd.shape), _resident(wm.shape), _resident(wf.shape), _resident(wo.shape)],
        out_specs=tok(d),
        out_shape=jax.ShapeDtypeStruct(x.shape, F32),
        compiler_params=_params(2),
        name="merge_out",
    )(x, mod, yd, ym, yf, gt, wd, wm, wf, wo)


def _rope_tables(n_lat, tm_ctx):
    rows = np.arange(n_lat) // GRID_W
    cols = np.arange(n_lat) % GRID_W

    def cos_sin(dim):
        nf = dim // 4
        freqs = np.power(np.float32(ROPE_BASE), -np.arange(nf, dtype=np.float32) / np.float32(nf)).astype(np.float32)
        ar = rows.astype(np.float32)[:, None] * freqs[None, :]
        ac = cols.astype(np.float32)[:, None] * freqs[None, :]
        ang = np.concatenate([ar, ar, ac, ac], axis=-1).astype(np.float64)
        sign = np.concatenate([-np.ones(nf), np.ones(nf), -np.ones(nf), np.ones(nf)])
        return np.cos(ang), np.sin(ang) * sign[None, :]

    cd, sd = cos_sin(DIFF_HD)
    cos_d = np.tile(cd, (1, LANES // DIFF_HD))
    sin_d = np.tile(sd, (1, LANES // DIFF_HD))
    cm, sm = cos_sin(MLA_ROPE)
    cos_m = np.ones((n_lat, LANES))
    sin_m = np.zeros((n_lat, LANES))
    cos_m[:, MLA_NOPE:MLA_NOPE + MLA_ROPE] = cm
    sin_m[:, MLA_NOPE:MLA_NOPE + MLA_ROPE] = sm
    lat = np.stack([cos_d, sin_d, cos_m, sin_m]).astype(np.float32)
    ident = np.stack([np.ones((tm_ctx, LANES)), np.zeros((tm_ctx, LANES))] * 2).astype(np.float32)
    return jnp.asarray(lat), jnp.asarray(ident)


def _dft_tables(n, group):
    def cs(m):
        idx = (np.arange(m)[:, None] * np.arange(m)[None, :]) % m
        ang = 2.0 * np.pi * idx / m
        return np.cos(ang), np.sin(ang)
    cn, sn = cs(n)
    scale = 1.0 / math.sqrt(n * group)
    cg, sg = cs(group)
    return (jnp.asarray(cn * scale, BF16), jnp.asarray(-sn * scale, BF16),
            jnp.asarray(np.concatenate([cg, sg], axis=1), BF16))


def _prep_layer_weights(l, d, ffn_w_in, ffn_w_out, w_in, mla_w_uq, mla_w_ukv):
    d_ff = ffn_w_out.shape[2]
    nch = d_ff // FFN_CHUNK
    ffn = []
    for i in range(2):
        wi = ffn_w_in[l, i].astype(BF16)
        wg = wi[:, :d_ff].reshape(d, nch, FFN_CHUNK)
        wu = wi[:, d_ff:].reshape(d, nch, FFN_CHUNK)
        win = jnp.concatenate([wg, wu], axis=-1).transpose(1, 0, 2)
        wout = ffn_w_out[l, i].astype(BF16).reshape(nch, FFN_CHUNK, d)
        ffn.append((win, wout))

    w = w_in[l].astype(BF16)
    o_kr = 3 * DIFF_W + MLA_Q_RANK + MLA_KV_RANK
    kr = jnp.pad(w[:, o_kr:o_kr + MLA_ROPE], ((0, 0), (MLA_NOPE, LANES - MLA_NOPE - MLA_ROPE)))
    w_proj = jnp.concatenate([w[:, :o_kr], kr, w[:, o_kr + MLA_ROPE:]], axis=1)

    uq = mla_w_uq[l].astype(BF16).reshape(MLA_Q_RANK, MLA_HEADS, MLA_NOPE + MLA_ROPE)
    uq = jnp.pad(uq, ((0, 0), (0, 0), (0, LANES - MLA_NOPE - MLA_ROPE))).reshape(MLA_Q_RANK, MLA_QK_W)
    ukv = mla_w_ukv[l].astype(BF16).reshape(MLA_KV_RANK, MLA_HEADS, MLA_NOPE + MLA_VD)
    kn = jnp.pad(ukv[:, :, :MLA_NOPE], ((0, 0), (0, 0), (0, LANES - MLA_NOPE))).reshape(MLA_KV_RANK, MLA_QK_W)
    vv = ukv[:, :, MLA_NOPE:].reshape(MLA_KV_RANK, MLA_V_W)
    return ffn, w_proj, uq, jnp.concatenate([kn, vv], axis=1)


def kernel(x, c, ctx, c_ctx, ada_w, ada_b, norm_g, ffn_w_in, ffn_w_out, w_in, diff_lambda,
           diff_subln_g, mla_q_norm_g, mla_w_uq, mla_kv_norm_g, mla_w_ukv, w_branch_diff,
           w_branch_mla, w_branch_fourier, w_out, final_norm_g):
    b, n_lat, d = x.shape
    n_ctx = ctx.shape[1]
    depth = ada_w.shape[0]

    cond_rows = -(-(b + 1) // 8) * 8
    cond = jnp.concatenate([c, c_ctx[None, :], jnp.zeros((cond_rows - b - 1, d), F32)], axis=0)
    mods = _ada_call(cond, ada_w, ada_b).reshape(depth, cond_rows, N_MOD, d)

    tm_ctx = _tile(b * n_ctx, 512)
    tab_x, tab_c = _rope_tables(n_lat, tm_ctx)
    cl, sl_, cs128 = _dft_tables(n_lat, FOURIER_GROUP_DIM)
    cc, sc, _ = _dft_tables(n_ctx, FOURIER_GROUP_DIM)

    xs = x
    cs = ctx.reshape(1, b * n_ctx, d)
    per_batch = lambda a: a.reshape(b, n_ctx, a.shape[-1])
    for l in range(depth):
        last = l == depth - 1
        mx, mc = mods[l, :b], mods[l, b:b + 1]
        ffn, w_proj, uq, ukv = _prep_layer_weights(l, d, ffn_w_in, ffn_w_out, w_in, mla_w_uq, mla_w_ukv)
        g0, g1, g2 = (norm_g[l, i][None, :] for i in range(3))
        qn, kvn = mla_q_norm_g[l][None, :], mla_kv_norm_g[l][None, :]
        lam_init = 0.8 - 0.6 * math.exp(-0.3 * l)
        lam_p, sub_g = diff_lambda[l], diff_subln_g[l][None, :]
        wd, wm, wf, wo = (a[l].astype(BF16) for a in (w_branch_diff, w_branch_mla, w_branch_fourier, w_out))

        xs = _ffn_call(xs, mx, g0, ffn[0][0], ffn[0][1], 0)
        cs = _ffn_call(cs, mc, g0, ffn[0][0], ffn[0][1], 0)

        px = _proj_call(xs, mx, g1, w_proj, uq, ukv, qn, kvn, cs128, tab_x, True)
        pc = _proj_call(cs, mc, g1, w_proj, uq, ukv, qn, kvn, cs128, tab_c, False)
        qd_x, kd_x, vd_x, qm_x, km_x, vm_x, zc_x, zs_x, gt_x = px
        qd_c, kd_c, vd_c, qm_c, km_c, vm_c, zc_c, zs_c, gt_c = (per_batch(a[0]) for a in pc)

        diff = functools.partial(_diff_body, lam_init=lam_init)
        yd_x = _attn_call(diff, "diff_attn", qd_x, [kd_c, kd_x], [vd_c, vd_x], [lam_p, sub_g], DIFF_W)
        ym_x = _attn_call(_mla_body, "mla_attn", qm_x, [km_c, km_x], [vm_c, vm_x], [], MLA_V_W)
        yf_x = _dft_call(cl, sl_, zc_x, zs_x)
        xs = _merge_call(xs, mx, yd_x, ym_x, yf_x, gt_x, wd, wm, wf, wo)
        xs = _ffn_call(xs, mx, g2, ffn[1][0], ffn[1][1], 6, final_g=final_norm_g[None, :] if last else None)

        if not last:
            flat = lambda a: a.reshape(1, b * n_ctx, a.shape[-1])
            yd_c = _attn_call(diff, "diff_attn_ctx", qd_c, [kd_c], [vd_c], [lam_p, sub_g], DIFF_W)
            ym_c = _attn_call(_mla_body, "mla_attn_ctx", qm_c, [km_c], [vm_c], [], MLA_V_W)
            yf_c = _dft_call(cc, sc, zc_c, zs_c)
            cs = _merge_call(cs, mc, flat(yd_c), flat(ym_c), flat(yf_c), flat(gt_c), wd, wm, wf, wo)
            cs = _ffn_call(cs, mc, g2, ffn[1][0], ffn[1][1], 6)
    return xs
```

```python
import functools
import math

import numpy as np
import jax
import jax.numpy as jnp
from jax import lax
from jax.experimental import pallas as pl
from jax.experimental.pallas import tpu as pltpu

F32 = jnp.float32
BF16 = jnp.bfloat16

GRID_W = 64
DIFF_HEADS = 4
DIFF_HD = 64
DIFF_VD = 2 * DIFF_HD
MLA_HEADS = 8
MLA_NOPE = 64
MLA_ROPE = 32
MLA_VD = 64
MLA_Q_RANK = 384
MLA_KV_RANK = 256
FOURIER_GROUPS = 4
FOURIER_GROUP_DIM = 128
N_BRANCHES = 3
ROPE_BASE = 10000.0
RMS_EPS = 1e-6
N_MOD = 9

DIFF_W = DIFF_HEADS * 2 * DIFF_HD
MLA_QK_W = MLA_HEADS * 128
MLA_V_W = MLA_HEADS * MLA_VD
FOURIER_W = FOURIER_GROUPS * FOURIER_GROUP_DIM

LANES = 128
MXU_W = 256
FFN_CHUNK = MXU_W
VMEM_LIMIT = 56 * 1024 * 1024
LOG2E = math.log2(math.e)


def _tile(n, pref):
    t = min(n, pref)
    while n % t:
        t //= 2
    return t


def _fixed(block, idx):
    return pl.BlockSpec(block, lambda *_: idx, pipeline_mode=pl.Buffered(1))


def _whole(a):
    return _fixed(a.shape, (0,) * a.ndim)


def _layer(a, l):
    return _fixed((1,) + a.shape[1:], (l,) + (0,) * (a.ndim - 1))


def _params(n_axes):
    return pltpu.CompilerParams(dimension_semantics=("arbitrary",) * n_axes,
                                vmem_limit_bytes=VMEM_LIMIT)


def _norm_mod(x, g, shift, scale):
    y = x * lax.rsqrt(jnp.mean(x * x, axis=-1, keepdims=True) + RMS_EPS)
    return ((y * g) * (1.0 + scale) + shift).astype(BF16)


def _rms(x, g):
    return x * lax.rsqrt(jnp.mean(x * x, axis=-1, keepdims=True) + RMS_EPS) * g


def _dot(a, b):
    return jnp.dot(a, b, preferred_element_type=F32)


def _ada_body(s_ref, w_ref, b_ref, o_ref):
    s = s_ref[...]
    a = (s * jax.nn.sigmoid(s)).astype(BF16)
    o_ref[0] = _dot(a, w_ref[0].astype(BF16)) + b_ref[0]


def _ada_call(cond, ada_w, ada_b):
    depth, d, n = ada_w.shape
    rows = cond.shape[0]
    tn = _tile(n, 1024)
    return pl.pallas_call(
        _ada_body,
        grid=(depth, n // tn),
        in_specs=[pl.BlockSpec((rows, d), lambda l, j: (0, 0)),
                  pl.BlockSpec((1, d, tn), lambda l, j: (l, 0, j)),
                  pl.BlockSpec((1, 1, tn), lambda l, j: (l, 0, j))],
        out_specs=pl.BlockSpec((1, rows, tn), lambda l, j: (l, 0, j)),
        out_shape=jax.ShapeDtypeStruct((depth, rows, n), F32),
        compiler_params=_params(2),
        name="ada_mod",
    )(cond, ada_w, ada_b.reshape(depth, 1, n))


def _mods_block(mods, l, row0):
    d = mods.shape[-1]
    return pl.BlockSpec((1, 1, N_MOD, d), lambda i, j: (l, row0 + i, 0, 0))


def _ffn_body(x_ref, mod_ref, g_ref, win_ref, wout_ref, *rest, mod0, gi, final):
    if final:
        fg_ref, o_ref = rest
    else:
        (o_ref,) = rest
    x = x_ref[0]
    d_ff = wout_ref.shape[2]
    xm = _norm_mod(x, g_ref[0, gi:gi + 1, :], mod_ref[0, 0, mod0:mod0 + 1, :], mod_ref[0, 0, mod0 + 1:mod0 + 2, :])
    acc = jnp.zeros(x.shape, F32)
    for c in range(d_ff // FFN_CHUNK):
        lo = c * FFN_CHUNK
        gate = _dot(xm, win_ref[0, 0, :, lo:lo + FFN_CHUNK])
        up = _dot(xm, win_ref[0, 0, :, d_ff + lo:d_ff + lo + FFN_CHUNK])
        act = (gate * jax.nn.sigmoid(gate) * up).astype(BF16)
        acc = acc + _dot(act, wout_ref[0, 0, lo:lo + FFN_CHUNK, :])
    y = x + (0.5 * mod_ref[0, 0, mod0 + 2:mod0 + 3, :]) * acc
    if final:
        y = _rms(y, fg_ref[...])
    o_ref[0] = y


def _ffn_call(x, mods, l, row0, norm_g, gi, win, wout, which, mod0, final_g=None, tm=512):
    s, t, d = x.shape
    tm = _tile(t, tm)
    final = final_g is not None
    in_specs = [pl.BlockSpec((1, tm, d), lambda i, j: (i, j, 0)),
                _mods_block(mods, l, row0),
                _layer(norm_g, l),
                _fixed((1, 1) + win.shape[2:], (l, which, 0, 0)),
                _fixed((1, 1) + wout.shape[2:], (l, which, 0, 0))]
    args = [x, mods, norm_g, win, wout]
    if final:
        in_specs.append(_whole(final_g))
        args.append(final_g)
    return pl.pallas_call(
        functools.partial(_ffn_body, mod0=mod0, gi=gi, final=final),
        grid=(s, t // tm),
        in_specs=in_specs,
        out_specs=pl.BlockSpec((1, tm, d), lambda i, j: (i, j, 0)),
        out_shape=jax.ShapeDtypeStruct(x.shape, F32),
        compiler_params=_params(2),
        name="ffn",
    )(*args)


def _rope(x, cos, sin_signed, chunk):
    lane = lax.broadcasted_iota(jnp.int32, x.shape, 1)
    even = (lane % (2 * chunk)) < chunk
    partner = jnp.where(even, pltpu.roll(x, LANES - chunk, 1), pltpu.roll(x, chunk, 1))
    return x * cos + partner * sin_signed


def _proj_body(x_ref, mod_ref, g_ref, w_ref, wuq_ref, wukv_ref, qn_ref, kvn_ref, cs_ref, tab_ref,
               qd_ref, kd_ref, vd_ref, qm_ref, km_ref, vm_ref, zc_ref, zs_ref, gt_ref,
               *, qscale_d, qscale_m):
    x = x_ref[0]
    xm = _norm_mod(x, g_ref[0, 1:2, :], mod_ref[0, 0, 3:4, :], mod_ref[0, 0, 4:5, :])
    cos_d, sin_d, cos_m, sin_m = tab_ref[0], tab_ref[1], tab_ref[2], tab_ref[3]

    def mm(c0, width):
        return _dot(xm, w_ref[0, :, c0:c0 + width])

    c0 = 0
    q = mm(c0, DIFF_W)
    c0 += DIFF_W
    for h in range(DIFF_HEADS):
        sl = slice(h * LANES, (h + 1) * LANES)
        qd_ref[0, :, sl] = (_rope(q[:, sl], cos_d, sin_d, DIFF_HD // 4) * qscale_d).astype(BF16)
    k = mm(c0, DIFF_W)
    c0 += DIFF_W
    for h in range(DIFF_HEADS):
        sl = slice(h * LANES, (h + 1) * LANES)
        kd_ref[0, :, sl] = _rope(k[:, sl], cos_d, sin_d, DIFF_HD // 4).astype(BF16)
    vd_ref[0] = mm(c0, DIFF_W).astype(BF16)
    c0 += DIFF_W

    cq = _rms(mm(c0, MLA_Q_RANK), qn_ref[0]).astype(BF16)
    c0 += MLA_Q_RANK
    qm = _dot(cq, wuq_ref[0])
    for h in range(MLA_HEADS):
        sl = slice(h * LANES, (h + 1) * LANES)
        qm_ref[0, :, sl] = (_rope(qm[:, sl], cos_m, sin_m, MLA_ROPE // 4) * qscale_m).astype(BF16)

    ckv = _rms(mm(c0, MLA_KV_RANK), kvn_ref[0]).astype(BF16)
    c0 += MLA_KV_RANK
    kv = _dot(ckv, wukv_ref[0])
    kr = _rope(mm(c0, LANES), cos_m, sin_m, MLA_ROPE // 4)
    c0 += LANES
    for h in range(MLA_HEADS):
        sl = slice(h * LANES, (h + 1) * LANES)
        km_ref[0, :, sl] = (kv[:, sl] + kr).astype(BF16)
    vm_ref[0] = kv[:, MLA_QK_W:].astype(BF16)

    f = mm(c0, FOURIER_W).astype(BF16)
    c0 += FOURIER_W
    for gi in range(FOURIER_GROUPS):
        sl = slice(gi * LANES, (gi + 1) * LANES)
        z = _dot(f[:, sl], cs_ref[...])
        zc_ref[0, :, sl] = z[:, :LANES].astype(BF16)
        zs_ref[0, :, sl] = z[:, LANES:].astype(BF16)

    d = x.shape[-1]
    for bi in range(N_BRANCHES):
        gt_ref[0, :, bi * d:(bi + 1) * d] = jax.nn.sigmoid(mm(c0 + bi * d, d)).astype(BF16)


def _proj_call(x, mods, l, row0, norm_g, w, wuq, wukv, qn, kvn, cs128, tab, tab_per_tile, tm=512):
    s, t, d = x.shape
    tm = _tile(t, tm)
    widths = (DIFF_W, DIFF_W, DIFF_W, MLA_QK_W, MLA_QK_W, MLA_V_W, FOURIER_W, FOURIER_W, N_BRANCHES * d)
    tok = lambda wd: pl.BlockSpec((1, tm, wd), lambda i, j: (i, j, 0))
    tab_map = (lambda i, j: (0, j, 0)) if tab_per_tile else (lambda i, j: (0, 0, 0))
    return pl.pallas_call(
        functools.partial(_proj_body,
                          qscale_d=DIFF_HD ** -0.5 * LOG2E,
                          qscale_m=(MLA_NOPE + MLA_ROPE) ** -0.5 * LOG2E),
        grid=(s, t // tm),
        in_specs=[tok(d), _mods_block(mods, l, row0), _layer(norm_g, l),
                  _layer(w, l), _layer(wuq, l), _layer(wukv, l), _layer(qn, l), _layer(kvn, l),
                  _whole(cs128), pl.BlockSpec((4, tm, LANES), tab_map)],
        out_specs=[tok(wd) for wd in widths],
        out_shape=[jax.ShapeDtypeStruct((s, t, wd), BF16) for wd in widths],
        compiler_params=_params(2),
        name="in_proj",
    )(x, mods, norm_g, w, wuq, wukv, qn, kvn, cs128, tab)


def _fill_kv(srcs_k, srcs_v, k_scr, v_scr, heads, vd):
    vw = v_scr.shape[1] // heads
    r0 = 0
    for k_ref, v_ref in zip(srcs_k, srcs_v):
        n = k_ref.shape[1]
        k_scr[r0:r0 + n, :] = k_ref[0]
        lane = lax.broadcasted_iota(jnp.int32, (n, vw - vd), 1)
        ones_col = jnp.where(lane == 0, 1.0, 0.0).astype(BF16)
        for h in range(heads):
            v_scr[r0:r0 + n, h * vw:h * vw + vd] = v_ref[0, :, h * vd:(h + 1) * vd]
            v_scr[r0:r0 + n, h * vw + vd:(h + 1) * vw] = ones_col
        r0 += n


def _softmax_pv(s, v_aug, vd):
    m = jnp.max(s, axis=-1, keepdims=True)
    e = jnp.exp2(s - m).astype(BF16)
    r = _dot(e, v_aug)
    return r[:, :vd], r[:, vd:vd + 1]


def _qk(q, k):
    return lax.dot_general(q, k, (((1,), (1,)), ((), ())), preferred_element_type=F32)


def _diff_body(*refs, n_src, lam_init):
    q_ref, lam_ref, sg_ref = refs[0], refs[1], refs[2]
    k_srcs = refs[3:3 + n_src]
    v_srcs = refs[3 + n_src:3 + 2 * n_src]
    o_ref, k_scr, v_scr = refs[3 + 2 * n_src:]

    @pl.when(pl.program_id(1) == 0)
    def _():
        _fill_kv(k_srcs, v_srcs, k_scr, v_scr, DIFF_HEADS, DIFF_VD)

    lp = lam_ref[0]
    lam = (jnp.exp(jnp.sum(lp[0:1] * lp[1:2], axis=-1, keepdims=True))
           - jnp.exp(jnp.sum(lp[2:3] * lp[3:4], axis=-1, keepdims=True)) + lam_init)
    q = q_ref[0]
    lane = lax.broadcasted_iota(jnp.int32, (q.shape[0], LANES), 1)
    first = lane < DIFF_HD
    zero = jnp.zeros((), BF16)
    vw = v_scr.shape[1] // DIFF_HEADS
    for h in range(DIFF_HEADS):
        sl = slice(h * LANES, (h + 1) * LANES)
        qh, kh, vh = q[:, sl], k_scr[:, sl], v_scr[:, h * vw:(h + 1) * vw]
        o1, l1 = _softmax_pv(_qk(jnp.where(first, qh, zero), kh), vh, DIFF_VD)
        o2, l2 = _softmax_pv(_qk(jnp.where(first, zero, qh), kh), vh, DIFF_VD)
        o = o1 * (1.0 / l1) - o2 * (lam / l2)
        o_ref[0, :, sl] = (_rms(o, sg_ref[0]) * (1.0 - lam_init)).astype(BF16)


def _mla_body(*refs, n_src):
    q_ref = refs[0]
    k_srcs = refs[1:1 + n_src]
    v_srcs = refs[1 + n_src:1 + 2 * n_src]
    o_ref, k_scr, v_scr = refs[1 + 2 * n_src:]

    @pl.when(pl.program_id(1) == 0)
    def _():
        _fill_kv(k_srcs, v_srcs, k_scr, v_scr, MLA_HEADS, MLA_VD)

    q = q_ref[0]
    vw = v_scr.shape[1] // MLA_HEADS
    outs = []
    for h in range(MLA_HEADS):
        sl = slice(h * LANES, (h + 1) * LANES)
        o, l = _softmax_pv(_qk(q[:, sl], k_scr[:, sl]), v_scr[:, h * vw:(h + 1) * vw], MLA_VD)
        outs.append(o * (1.0 / l))
        if h % 2 == 1:
            o_ref[0, :, (h - 1) * MLA_VD:(h + 1) * MLA_VD] = jnp.concatenate(outs, axis=-1).astype(BF16)
            outs = []


def _attn_call(body, name, q, ks, vs, extra_specs, extra, heads, v_aug_w, out_w, tq=512):
    b, lq, wq = q.shape
    tq = _tile(lq, tq)
    n_src = len(ks)
    lk = sum(k.shape[1] for k in ks)
    full = lambda a: pl.BlockSpec((1,) + a.shape[1:], lambda i, j: (i, 0, 0))
    return pl.pallas_call(
        functools.partial(body, n_src=n_src),
        grid=(b, lq // tq),
        in_specs=([pl.BlockSpec((1, tq, wq), lambda i, j: (i, j, 0))] + extra_specs
                  + [full(k) for k in ks] + [full(v) for v in vs]),
        out_specs=pl.BlockSpec((1, tq, out_w), lambda i, j: (i, j, 0)),
        out_shape=jax.ShapeDtypeStruct((b, lq, out_w), BF16),
        scratch_shapes=[pltpu.VMEM((lk, ks[0].shape[2]), BF16),
                        pltpu.VMEM((lk, heads * v_aug_w), BF16)],
        compiler_params=_params(2),
        name=name,
    )(q, *extra, *ks, *vs)


def _dft_body(c_ref, s_ref, zc_ref, zs_ref, o_ref):
    o_ref[0] = (_dot(c_ref[...], zc_ref[0]) + _dot(s_ref[...], zs_ref[0])).astype(BF16)


def _dft_call(cmat, smat, zc, zs, tm=512):
    b, t, w = zc.shape
    tm = _tile(t, tm)
    return pl.pallas_call(
        _dft_body,
        grid=(t // tm, b),
        in_specs=[pl.BlockSpec((tm, t), lambda i, j: (i, 0)),
                  pl.BlockSpec((tm, t), lambda i, j: (i, 0)),
                  pl.BlockSpec((1, t, w), lambda i, j: (j, 0, 0)),
                  pl.BlockSpec((1, t, w), lambda i, j: (j, 0, 0))],
        out_specs=pl.BlockSpec((1, tm, w), lambda i, j: (j, i, 0)),
        out_shape=jax.ShapeDtypeStruct((b, t, w), BF16),
        compiler_params=_params(2),
        name="pos_dft",
    )(cmat, smat, zc, zs)


def _merge_body(x_ref, mod_ref, yd_ref, ym_ref, yf_ref, gt_ref, wd_ref, wm_ref, wf_ref, wo_ref, o_ref):
    x = x_ref[0]
    d = x.shape[-1]
    merged = (gt_ref[0, :, 0:d].astype(F32) * _dot(yd_ref[0], wd_ref[0])
              + gt_ref[0, :, d:2 * d].astype(F32) * _dot(ym_ref[0], wm_ref[0])
              + gt_ref[0, :, 2 * d:3 * d].astype(F32) * _dot(yf_ref[0], wf_ref[0]))
    o_ref[0] = x + mod_ref[0, 0, 5:6, :] * _dot(merged.astype(BF16), wo_ref[0])


def _merge_call(x, mods, l, row0, yd, ym, yf, gt, wd, wm, wf, wo, tm=512):
    s, t, d = x.shape
    tm = _tile(t, tm)
    tok = lambda wd_: pl.BlockSpec((1, tm, wd_), lambda i, j: (i, j, 0))
    return pl.pallas_call(
        _merge_body,
        grid=(s, t // tm),
        in_specs=[tok(d), _mods_block(mods, l, row0),
                  tok(yd.shape[2]), tok(ym.shape[2]), tok(yf.shape[2]), tok(gt.shape[2]),
                  _layer(wd, l), _layer(wm, l), _layer(wf, l), _layer(wo, l)],
        out_specs=tok(d),
        out_shape=jax.ShapeDtypeStruct(x.shape, F32),
        compiler_params=_params(2),
        name="merge_out",
    )(x, mods, yd, ym, yf, gt, wd, wm, wf, wo)


def _rope_tables(n_lat, tm_ctx):
    rows = np.arange(n_lat) // GRID_W
    cols = np.arange(n_lat) % GRID_W

    def cos_sin(dim):
        nf = dim // 4
        freqs = np.power(np.float32(ROPE_BASE), -np.arange(nf, dtype=np.float32) / np.float32(nf)).astype(np.float32)
        ar = rows.astype(np.float32)[:, None] * freqs[None, :]
        ac = cols.astype(np.float32)[:, None] * freqs[None, :]
        ang = np.concatenate([ar, ar, ac, ac], axis=-1).astype(np.float64)
        sign = np.concatenate([-np.ones(nf), np.ones(nf), -np.ones(nf), np.ones(nf)])
        return np.cos(ang), np.sin(ang) * sign[None, :]

    cd, sd = cos_sin(DIFF_HD)
    cos_d = np.tile(cd, (1, LANES // DIFF_HD))
    sin_d = np.tile(sd, (1, LANES // DIFF_HD))
    cm, sm = cos_sin(MLA_ROPE)
    cos_m = np.ones((n_lat, LANES))
    sin_m = np.zeros((n_lat, LANES))
    cos_m[:, MLA_NOPE:MLA_NOPE + MLA_ROPE] = cm
    sin_m[:, MLA_NOPE:MLA_NOPE + MLA_ROPE] = sm
    lat = np.stack([cos_d, sin_d, cos_m, sin_m]).astype(np.float32)
    ident = np.stack([np.ones((tm_ctx, LANES)), np.zeros((tm_ctx, LANES))] * 2).astype(np.float32)
    return jnp.asarray(lat), jnp.asarray(ident)


def _dft_tables(n, group):
    def cs(m):
        idx = (np.arange(m)[:, None] * np.arange(m)[None, :]) % m
        ang = 2.0 * np.pi * idx / m
        return np.cos(ang), np.sin(ang)
    cn, sn = cs(n)
    scale = 1.0 / math.sqrt(n * group)
    cg, sg = cs(group)
    return (jnp.asarray(cn * scale, BF16), jnp.asarray(-sn * scale, BF16),
            jnp.asarray(np.concatenate([cg, sg], axis=1), BF16))


def _prep_weights(w_in, mla_w_uq, mla_w_ukv):
    depth = w_in.shape[0]
    o_kr = 3 * DIFF_W + MLA_Q_RANK + MLA_KV_RANK
    kr = jnp.pad(w_in[:, :, o_kr:o_kr + MLA_ROPE], ((0, 0), (0, 0), (MLA_NOPE, LANES - MLA_NOPE - MLA_ROPE)))
    w_proj = jnp.concatenate([w_in[:, :, :o_kr], kr, w_in[:, :, o_kr + MLA_ROPE:]], axis=2).astype(BF16)

    uq = mla_w_uq.reshape(depth, MLA_Q_RANK, MLA_HEADS, MLA_NOPE + MLA_ROPE)
    uq = jnp.pad(uq, ((0, 0), (0, 0), (0, 0), (0, LANES - MLA_NOPE - MLA_ROPE)))
    uq = uq.reshape(depth, MLA_Q_RANK, MLA_QK_W).astype(BF16)
    ukv = mla_w_ukv.reshape(depth, MLA_KV_RANK, MLA_HEADS, MLA_NOPE + MLA_VD)
    kn = jnp.pad(ukv[..., :MLA_NOPE], ((0, 0), (0, 0), (0, 0), (0, LANES - MLA_NOPE)))
    kn = kn.reshape(depth, MLA_KV_RANK, MLA_QK_W)
    vv = ukv[..., MLA_NOPE:].reshape(depth, MLA_KV_RANK, MLA_V_W)
    return w_proj, uq, jnp.concatenate([kn, vv], axis=2).astype(BF16)


def kernel(x, c, ctx, c_ctx, ada_w, ada_b, norm_g, ffn_w_in, ffn_w_out, w_in, diff_lambda,
           diff_subln_g, mla_q_norm_g, mla_w_uq, mla_kv_norm_g, mla_w_ukv, w_branch_diff,
           w_branch_mla, w_branch_fourier, w_out, final_norm_g):
    b, n_lat, d = x.shape
    n_ctx = ctx.shape[1]
    depth = ada_w.shape[0]

    cond_rows = -(-(b + 1) // 8) * 8
    cond = jnp.concatenate([c, c_ctx[None, :], jnp.zeros((cond_rows - b - 1, d), F32)], axis=0)
    mods = _ada_call(cond, ada_w, ada_b).reshape(depth, cond_rows, N_MOD, d)

    tm_ctx = _tile(b * n_ctx, 512)
    tab_x, tab_c = _rope_tables(n_lat, tm_ctx)
    cl, sl_, cs128 = _dft_tables(n_lat, FOURIER_GROUP_DIM)
    cc, sc, _ = _dft_tables(n_ctx, FOURIER_GROUP_DIM)

    ffn_in, ffn_out = ffn_w_in.astype(BF16), ffn_w_out.astype(BF16)
    w_proj, uq, ukv = _prep_weights(w_in, mla_w_uq, mla_w_ukv)
    wd, wm, wf, wo = (a.astype(BF16) for a in (w_branch_diff, w_branch_mla, w_branch_fourier, w_out))
    qn, kvn, sub_g = (a[:, None, :] for a in (mla_q_norm_g, mla_kv_norm_g, diff_subln_g))
    final_g = final_norm_g[None, :]

    xs = x
    cs = ctx.reshape(1, b * n_ctx, d)
    per_batch = lambda a: a.reshape(b, n_ctx, a.shape[-1])
    flat = lambda a: a.reshape(1, b * n_ctx, a.shape[-1])
    for l in range(depth):
        last = l == depth - 1
        lam_init = 0.8 - 0.6 * math.exp(-0.3 * l)
        diff = functools.partial(_diff_body, lam_init=lam_init)
        diff_extra = ([_layer(diff_lambda, l), _layer(sub_g, l)], [diff_lambda, sub_g])

        xs = _ffn_call(xs, mods, l, 0, norm_g, 0, ffn_in, ffn_out, 0, 0)
        cs = _ffn_call(cs, mods, l, b, norm_g, 0, ffn_in, ffn_out, 0, 0)

        px = _proj_call(xs, mods, l, 0, norm_g, w_proj, uq, ukv, qn, kvn, cs128, tab_x, True)
        pc = _proj_call(cs, mods, l, b, norm_g, w_proj, uq, ukv, qn, kvn, cs128, tab_c, False)
        qd_x, kd_x, vd_x, qm_x, km_x, vm_x, zc_x, zs_x, gt_x = px
        qd_c, kd_c, vd_c, qm_c, km_c, vm_c, zc_c, zs_c, gt_c = (per_batch(a) for a in pc)

        yd_x = _attn_call(diff, "diff_attn", qd_x, [kd_c, kd_x], [vd_c, vd_x], *diff_extra,
                          DIFF_HEADS, MXU_W, DIFF_W)
        ym_x = _attn_call(_mla_body, "mla_attn", qm_x, [km_c, km_x], [vm_c, vm_x], [], [],
                          MLA_HEADS, LANES, MLA_V_W)
        yf_x = _dft_call(cl, sl_, zc_x, zs_x)
        xs = _merge_call(xs, mods, l, 0, yd_x, ym_x, yf_x, gt_x, wd, wm, wf, wo)
        xs = _ffn_call(xs, mods, l, 0, norm_g, 2, ffn_in, ffn_out, 1, 6, final_g=final_g if last else None)

        if not last:
            yd_c = _attn_call(diff, "diff_attn_ctx", qd_c, [kd_c], [vd_c], *diff_extra,
                              DIFF_HEADS, MXU_W, DIFF_W)
            ym_c = _attn_call(_mla_body, "mla_attn_ctx", qm_c, [km_c], [vm_c], [], [],
                              MLA_HEADS, LANES, MLA_V_W)
            yf_c = _dft_call(cc, sc, zc_c, zs_c)
            cs = _merge_call(cs, mods, l, b, flat(yd_c), flat(ym_c), flat(yf_c), flat(gt_c), wd, wm, wf, wo)
            cs = _ffn_call(cs, mods, l, b, norm_g, 2, ffn_in, ffn_out, 1, 6)
    return xs
```

```python
import functools
import math

import numpy as np
import jax
import jax.numpy as jnp
from jax import lax
from jax.experimental import pallas as pl
from jax.experimental.pallas import tpu as pltpu

F32 = jnp.float32
BF16 = jnp.bfloat16

GRID_W = 64
DIFF_HEADS = 4
DIFF_HD = 64
DIFF_VD = 2 * DIFF_HD
MLA_HEADS = 8
MLA_NOPE = 64
MLA_ROPE = 32
MLA_VD = 64
MLA_Q_RANK = 384
MLA_KV_RANK = 256
FOURIER_GROUPS = 4
FOURIER_GROUP_DIM = 128
N_BRANCHES = 3
ROPE_BASE = 10000.0
RMS_EPS = 1e-6
N_MOD = 9

DIFF_W = DIFF_HEADS * 2 * DIFF_HD
MLA_QK_W = MLA_HEADS * 128
MLA_V_W = MLA_HEADS * MLA_VD
FOURIER_W = FOURIER_GROUPS * FOURIER_GROUP_DIM

LANES = 128
MXU_W = 256
FFN_CHUNK = MXU_W
VMEM_LIMIT = 56 * 1024 * 1024
LOG2E = math.log2(math.e)


def _tile(n, pref):
    t = min(n, pref)
    while n % t:
        t //= 2
    return t


def _fixed(block, idx):
    return pl.BlockSpec(block, lambda *_: idx, pipeline_mode=pl.Buffered(1))


def _whole(a):
    return _fixed(a.shape, (0,) * a.ndim)


def _layer(a, l):
    return _fixed((1,) + a.shape[1:], (l,) + (0,) * (a.ndim - 1))


def _params(n_axes):
    return pltpu.CompilerParams(dimension_semantics=("arbitrary",) * n_axes,
                                vmem_limit_bytes=VMEM_LIMIT)


def _norm_mod(x, g, shift, scale):
    y = x * lax.rsqrt(jnp.mean(x * x, axis=-1, keepdims=True) + RMS_EPS)
    return (y * (g * (1.0 + scale)) + shift).astype(BF16)


def _rms(x, g):
    return x * lax.rsqrt(jnp.mean(x * x, axis=-1, keepdims=True) + RMS_EPS) * g


def _dot(a, b):
    return jnp.dot(a, b, preferred_element_type=F32)


def _ada_body(s_ref, w_ref, b_ref, o_ref):
    s = s_ref[...]
    a = (s * jax.nn.sigmoid(s)).astype(BF16)
    o_ref[0] = _dot(a, w_ref[0].astype(BF16)) + b_ref[0]


def _ada_call(cond, ada_w, ada_b):
    depth, d, n = ada_w.shape
    rows = cond.shape[0]
    tn = _tile(n, 1024)
    return pl.pallas_call(
        _ada_body,
        grid=(depth, n // tn),
        in_specs=[pl.BlockSpec((rows, d), lambda l, j: (0, 0)),
                  pl.BlockSpec((1, d, tn), lambda l, j: (l, 0, j)),
                  pl.BlockSpec((1, 1, tn), lambda l, j: (l, 0, j))],
        out_specs=pl.BlockSpec((1, rows, tn), lambda l, j: (l, 0, j)),
        out_shape=jax.ShapeDtypeStruct((depth, rows, n), F32),
        compiler_params=_params(2),
        name="ada_mod",
    )(cond, ada_w, ada_b.reshape(depth, 1, n))


def _mods_block(mods, l, row0):
    d = mods.shape[-1]
    return pl.BlockSpec((1, 1, N_MOD, d), lambda i, j: (l, row0 + i, 0, 0))


def _ffn_body(x_ref, mod_ref, g_ref, win_ref, wout_ref, *rest, mod0, gi, final):
    if final:
        fg_ref, o_ref = rest
    else:
        (o_ref,) = rest
    x = x_ref[0]
    d_ff = wout_ref.shape[2]
    xm = _norm_mod(x, g_ref[0, gi:gi + 1, :], mod_ref[0, 0, mod0:mod0 + 1, :], mod_ref[0, 0, mod0 + 1:mod0 + 2, :])
    acc = jnp.zeros(x.shape, F32)
    for c in range(d_ff // FFN_CHUNK):
        lo = c * FFN_CHUNK
        gate = _dot(xm, win_ref[0, 0, :, lo:lo + FFN_CHUNK])
        up = _dot(xm, win_ref[0, 0, :, d_ff + lo:d_ff + lo + FFN_CHUNK])
        act = (gate * jax.nn.sigmoid(gate) * up).astype(BF16)
        acc = acc + _dot(act, wout_ref[0, 0, lo:lo + FFN_CHUNK, :])
    y = x + (0.5 * mod_ref[0, 0, mod0 + 2:mod0 + 3, :]) * acc
    if final:
        y = _rms(y, fg_ref[...])
    o_ref[0] = y


def _ffn_call(x, mods, l, row0, norm_g, gi, win, wout, which, mod0, final_g=None, tm=1024):
    s, t, d = x.shape
    tm = _tile(t, tm)
    final = final_g is not None
    in_specs = [pl.BlockSpec((1, tm, d), lambda i, j: (i, j, 0)),
                _mods_block(mods, l, row0),
                _layer(norm_g, l),
                _fixed((1, 1) + win.shape[2:], (l, which, 0, 0)),
                _fixed((1, 1) + wout.shape[2:], (l, which, 0, 0))]
    args = [x, mods, norm_g, win, wout]
    if final:
        in_specs.append(_whole(final_g))
        args.append(final_g)
    return pl.pallas_call(
        functools.partial(_ffn_body, mod0=mod0, gi=gi, final=final),
        grid=(s, t // tm),
        in_specs=in_specs,
        out_specs=pl.BlockSpec((1, tm, d), lambda i, j: (i, j, 0)),
        out_shape=jax.ShapeDtypeStruct(x.shape, F32),
        compiler_params=_params(2),
        name="ffn",
    )(*args)


def _rope(x, cos, sin_signed, chunk):
    lane = lax.broadcasted_iota(jnp.int32, x.shape, 1)
    even = (lane % (2 * chunk)) < chunk
    partner = jnp.where(even, pltpu.roll(x, LANES - chunk, 1), pltpu.roll(x, chunk, 1))
    return x * cos + partner * sin_signed


def _proj_body(x_ref, mod_ref, g_ref, w_ref, wuq_ref, wukv_ref, qn_ref, kvn_ref, cs_ref, tab_ref,
               qd_ref, kd_ref, vd_ref, qm_ref, km_ref, vm_ref, zc_ref, zs_ref, gt_ref,
               *, qscale_d, qscale_m):
    x = x_ref[0]
    xm = _norm_mod(x, g_ref[0, 1:2, :], mod_ref[0, 0, 3:4, :], mod_ref[0, 0, 4:5, :])
    cos_d, sin_d, cos_m, sin_m = tab_ref[0], tab_ref[1], tab_ref[2], tab_ref[3]

    def mm(c0, width):
        return _dot(xm, w_ref[0, :, c0:c0 + width])

    c0 = 0
    q = mm(c0, DIFF_W)
    c0 += DIFF_W
    for h in range(DIFF_HEADS):
        sl = slice(h * LANES, (h + 1) * LANES)
        qd_ref[0, :, sl] = (_rope(q[:, sl], cos_d, sin_d, DIFF_HD // 4) * qscale_d).astype(BF16)
    k = mm(c0, DIFF_W)
    c0 += DIFF_W
    for h in range(DIFF_HEADS):
        sl = slice(h * LANES, (h + 1) * LANES)
        kd_ref[0, :, sl] = _rope(k[:, sl], cos_d, sin_d, DIFF_HD // 4).astype(BF16)
    vd_ref[0] = mm(c0, DIFF_W).astype(BF16)
    c0 += DIFF_W

    cq = _rms(mm(c0, MLA_Q_RANK), qn_ref[0]).astype(BF16)
    c0 += MLA_Q_RANK
    qm = _dot(cq, wuq_ref[0])
    for h in range(MLA_HEADS):
        sl = slice(h * LANES, (h + 1) * LANES)
        qm_ref[0, :, sl] = (_rope(qm[:, sl], cos_m, sin_m, MLA_ROPE // 4) * qscale_m).astype(BF16)

    ckv = _rms(mm(c0, MLA_KV_RANK), kvn_ref[0]).astype(BF16)
    c0 += MLA_KV_RANK
    kv = _dot(ckv, wukv_ref[0])
    kr = _rope(mm(c0, LANES), cos_m, sin_m, MLA_ROPE // 4)
    c0 += LANES
    for h in range(MLA_HEADS):
        sl = slice(h * LANES, (h + 1) * LANES)
        km_ref[0, :, sl] = (kv[:, sl] + kr).astype(BF16)
    vm_ref[0] = kv[:, MLA_QK_W:].astype(BF16)

    f = mm(c0, FOURIER_W).astype(BF16)
    c0 += FOURIER_W
    for gi in range(FOURIER_GROUPS):
        sl = slice(gi * LANES, (gi + 1) * LANES)
        z = _dot(f[:, sl], cs_ref[...])
        zc_ref[0, :, sl] = z[:, :LANES].astype(BF16)
        zs_ref[0, :, sl] = z[:, LANES:].astype(BF16)

    d = x.shape[-1]
    for bi in range(N_BRANCHES):
        gt_ref[0, :, bi * d:(bi + 1) * d] = jax.nn.sigmoid(mm(c0 + bi * d, d)).astype(BF16)


def _proj_call(x, mods, l, row0, norm_g, w, wuq, wukv, qn, kvn, cs128, tab, tab_per_tile, tm=512):
    s, t, d = x.shape
    tm = _tile(t, tm)
    widths = (DIFF_W, DIFF_W, DIFF_W, MLA_QK_W, MLA_QK_W, MLA_V_W, FOURIER_W, FOURIER_W, N_BRANCHES * d)
    tok = lambda wd: pl.BlockSpec((1, tm, wd), lambda i, j: (i, j, 0))
    tab_map = (lambda i, j: (0, j, 0)) if tab_per_tile else (lambda i, j: (0, 0, 0))
    return pl.pallas_call(
        functools.partial(_proj_body,
                          qscale_d=DIFF_HD ** -0.5 * LOG2E,
                          qscale_m=(MLA_NOPE + MLA_ROPE) ** -0.5 * LOG2E),
        grid=(s, t // tm),
        in_specs=[tok(d), _mods_block(mods, l, row0), _layer(norm_g, l),
                  _layer(w, l), _layer(wuq, l), _layer(wukv, l), _layer(qn, l), _layer(kvn, l),
                  _whole(cs128), pl.BlockSpec((4, tm, LANES), tab_map)],
        out_specs=[tok(wd) for wd in widths],
        out_shape=[jax.ShapeDtypeStruct((s, t, wd), BF16) for wd in widths],
        compiler_params=_params(2),
        name="in_proj",
    )(x, mods, norm_g, w, wuq, wukv, qn, kvn, cs128, tab)


def _fill_kv(srcs_k, srcs_v, k_scr, v_scr, heads, vd):
    vw = v_scr.shape[1] // heads
    r0 = 0
    for k_ref, v_ref in zip(srcs_k, srcs_v):
        n = k_ref.shape[1]
        k_scr[r0:r0 + n, :] = k_ref[0]
        lane = lax.broadcasted_iota(jnp.int32, (n, vw - vd), 1)
        ones_col = jnp.where(lane == 0, 1.0, 0.0).astype(BF16)
        for h in range(heads):
            v_scr[r0:r0 + n, h * vw:h * vw + vd] = v_ref[0, :, h * vd:(h + 1) * vd]
            v_scr[r0:r0 + n, h * vw + vd:(h + 1) * vw] = ones_col
        r0 += n


def _softmax_pv(s, v_aug, vd):
    m = jnp.max(s, axis=-1, keepdims=True)
    e = jnp.exp2(s - m).astype(BF16)
    r = _dot(e, v_aug)
    return r[:, :vd], r[:, vd:vd + 1]


SMALL_UNIT_ROWS = 128


def _head_row_units(heads, rows):
    if rows <= 2 * SMALL_UNIT_ROWS:
        return [(h, 0, rows) for h in range(heads)]
    units = [(0, 0, SMALL_UNIT_ROWS), (0, SMALL_UNIT_ROWS, rows)]
    units += [(h, 0, rows) for h in range(1, heads - 1)]
    units += [(heads - 1, 0, rows - SMALL_UNIT_ROWS), (heads - 1, rows - SMALL_UNIT_ROWS, rows)]
    return units


def _qk(q, k):
    return lax.dot_general(q, k, (((1,), (1,)), ((), ())), preferred_element_type=F32)


def _diff_body(*refs, n_src, lam_init):
    q_ref, lam_ref, sg_ref = refs[0], refs[1], refs[2]
    k_srcs = refs[3:3 + n_src]
    v_srcs = refs[3 + n_src:3 + 2 * n_src]
    o_ref, k_scr, v_scr = refs[3 + 2 * n_src:]

    @pl.when(pl.program_id(1) == 0)
    def _():
        _fill_kv(k_srcs, v_srcs, k_scr, v_scr, DIFF_HEADS, DIFF_VD)

    lp = lam_ref[0]
    lam = (jnp.exp(jnp.sum(lp[0:1] * lp[1:2], axis=-1, keepdims=True))
           - jnp.exp(jnp.sum(lp[2:3] * lp[3:4], axis=-1, keepdims=True)) + lam_init)
    zero = jnp.zeros((), BF16)
    vw = v_scr.shape[1] // DIFF_HEADS
    for h, r0, r1 in _head_row_units(DIFF_HEADS, q_ref.shape[1]):
        sl = slice(h * LANES, (h + 1) * LANES)
        qh, kh, vh = q_ref[0, r0:r1, sl], k_scr[:, sl], v_scr[:, h * vw:(h + 1) * vw]
        first = lax.broadcasted_iota(jnp.int32, qh.shape, 1) < DIFF_HD
        o1, l1 = _softmax_pv(_qk(jnp.where(first, qh, zero), kh), vh, DIFF_VD)
        o2, l2 = _softmax_pv(_qk(jnp.where(first, zero, qh), kh), vh, DIFF_VD)
        o = o1 * (1.0 / l1) - o2 * (lam / l2)
        o_ref[0, r0:r1, sl] = (_rms(o, sg_ref[0]) * (1.0 - lam_init)).astype(BF16)


def _mla_body(*refs, n_src):
    q_ref = refs[0]
    k_srcs = refs[1:1 + n_src]
    v_srcs = refs[1 + n_src:1 + 2 * n_src]
    o_ref, k_scr, v_scr = refs[1 + 2 * n_src:]

    @pl.when(pl.program_id(1) == 0)
    def _():
        _fill_kv(k_srcs, v_srcs, k_scr, v_scr, MLA_HEADS, MLA_VD)

    vw = v_scr.shape[1] // MLA_HEADS
    for h, r0, r1 in _head_row_units(MLA_HEADS, q_ref.shape[1]):
        sl = slice(h * LANES, (h + 1) * LANES)
        o, l = _softmax_pv(_qk(q_ref[0, r0:r1, sl], k_scr[:, sl]), v_scr[:, h * vw:(h + 1) * vw], MLA_VD)
        o_ref[0, r0:r1, h * MLA_VD:(h + 1) * MLA_VD] = (o * (1.0 / l)).astype(BF16)


def _attn_call(body, name, q, ks, vs, extra_specs, extra, heads, v_aug_w, out_w, tq=512):
    b, lq, wq = q.shape
    tq = _tile(lq, tq)
    n_src = len(ks)
    lk = sum(k.shape[1] for k in ks)
    full = lambda a: pl.BlockSpec((1,) + a.shape[1:], lambda i, j: (i, 0, 0))
    return pl.pallas_call(
        functools.partial(body, n_src=n_src),
        grid=(b, lq // tq),
        in_specs=([pl.BlockSpec((1, tq, wq), lambda i, j: (i, j, 0))] + extra_specs
                  + [full(k) for k in ks] + [full(v) for v in vs]),
        out_specs=pl.BlockSpec((1, tq, out_w), lambda i, j: (i, j, 0)),
        out_shape=jax.ShapeDtypeStruct((b, lq, out_w), BF16),
        scratch_shapes=[pltpu.VMEM((lk, ks[0].shape[2]), BF16),
                        pltpu.VMEM((lk, heads * v_aug_w), BF16)],
        compiler_params=_params(2),
        name=name,
    )(q, *extra, *ks, *vs)


def _dft_body(t_ref, zc_ref, zs_ref, o_ref):
    w = o_ref.shape[-1]
    even = _dot(t_ref[0], zc_ref[0, :, :w]) + _dot(t_ref[1], zs_ref[0, :, :w])
    odd = _dot(t_ref[2], zc_ref[0, :, w:]) + _dot(t_ref[3], zs_ref[0, :, w:])
    o_ref[0, 0] = (even + odd).astype(BF16)
    o_ref[0, 1] = (even - odd).astype(BF16)


def _dft_call(tables, zc, zs, tm=1024):
    b, t, w = zc.shape
    half = t // 2
    tm = _tile(half, tm)
    pairs = lambda a: a.reshape(b, half, 2 * w)
    out = pl.pallas_call(
        _dft_body,
        grid=(half // tm, b),
        in_specs=[pl.BlockSpec((4, tm, half), lambda i, j: (0, i, 0)),
                  pl.BlockSpec((1, half, 2 * w), lambda i, j: (j, 0, 0)),
                  pl.BlockSpec((1, half, 2 * w), lambda i, j: (j, 0, 0))],
        out_specs=pl.BlockSpec((1, 2, tm, w), lambda i, j: (j, 0, i, 0)),
        out_shape=jax.ShapeDtypeStruct((b, 2, half, w), BF16),
        compiler_params=_params(2),
        name="pos_dft",
    )(tables, pairs(zc), pairs(zs))
    return out.reshape(b, t, w)


def _merge_body(x_ref, mod_ref, yd_ref, ym_ref, yf_ref, gt_ref, wd_ref, wm_ref, wf_ref, wo_ref, o_ref):
    x = x_ref[0]
    d = x.shape[-1]
    merged = (gt_ref[0, :, 0:d].astype(F32) * _dot(yd_ref[0], wd_ref[0])
              + gt_ref[0, :, d:2 * d].astype(F32) * _dot(ym_ref[0], wm_ref[0])
              + gt_ref[0, :, 2 * d:3 * d].astype(F32) * _dot(yf_ref[0], wf_ref[0]))
    o_ref[0] = x + mod_ref[0, 0, 5:6, :] * _dot(merged.astype(BF16), wo_ref[0])


def _merge_call(x, mods, l, row0, yd, ym, yf, gt, wd, wm, wf, wo, tm=512):
    s, t, d = x.shape
    tm = _tile(t, tm)
    tok = lambda wd_: pl.BlockSpec((1, tm, wd_), lambda i, j: (i, j, 0))
    return pl.pallas_call(
        _merge_body,
        grid=(s, t // tm),
        in_specs=[tok(d), _mods_block(mods, l, row0),
                  tok(yd.shape[2]), tok(ym.shape[2]), tok(yf.shape[2]), tok(gt.shape[2]),
                  _layer(wd, l), _layer(wm, l), _layer(wf, l), _layer(wo, l)],
        out_specs=tok(d),
        out_shape=jax.ShapeDtypeStruct(x.shape, F32),
        compiler_params=_params(2),
        name="merge_out",
    )(x, mods, yd, ym, yf, gt, wd, wm, wf, wo)


def _rope_tables(n_lat, tm_ctx):
    rows = np.arange(n_lat) // GRID_W
    cols = np.arange(n_lat) % GRID_W

    def cos_sin(dim):
        nf = dim // 4
        freqs = np.power(np.float32(ROPE_BASE), -np.arange(nf, dtype=np.float32) / np.float32(nf)).astype(np.float32)
        ar = rows.astype(np.float32)[:, None] * freqs[None, :]
        ac = cols.astype(np.float32)[:, None] * freqs[None, :]
        ang = np.concatenate([ar, ar, ac, ac], axis=-1).astype(np.float64)
        sign = np.concatenate([-np.ones(nf), np.ones(nf), -np.ones(nf), np.ones(nf)])
        return np.cos(ang), np.sin(ang) * sign[None, :]

    cd, sd = cos_sin(DIFF_HD)
    cos_d = np.tile(cd, (1, LANES // DIFF_HD))
    sin_d = np.tile(sd, (1, LANES // DIFF_HD))
    cm, sm = cos_sin(MLA_ROPE)
    cos_m = np.ones((n_lat, LANES))
    sin_m = np.zeros((n_lat, LANES))
    cos_m[:, MLA_NOPE:MLA_NOPE + MLA_ROPE] = cm
    sin_m[:, MLA_NOPE:MLA_NOPE + MLA_ROPE] = sm
    lat = np.stack([cos_d, sin_d, cos_m, sin_m]).astype(np.float32)
    ident = np.stack([np.ones((tm_ctx, LANES)), np.zeros((tm_ctx, LANES))] * 2).astype(np.float32)
    return jnp.asarray(lat), jnp.asarray(ident)


def _angles(rows, cols, n):
    return 2.0 * np.pi * ((rows[:, None] * cols[None, :]) % n) / n


def _pos_dft_tables(n, group):
    j, m = np.arange(n // 2), np.arange(n // 2)
    scale = 1.0 / math.sqrt(n * group)
    ae, ao = _angles(j, 2 * m, n), _angles(j, 2 * m + 1, n)
    return jnp.asarray(np.stack([np.cos(ae), -np.sin(ae), np.cos(ao), -np.sin(ao)]) * scale, BF16)


def _chan_dft_table(group):
    a = _angles(np.arange(group), np.arange(group), group)
    return jnp.asarray(np.concatenate([np.cos(a), np.sin(a)], axis=1), BF16)


def _prep_weights(w_in, mla_w_uq, mla_w_ukv):
    depth = w_in.shape[0]
    o_kr = 3 * DIFF_W + MLA_Q_RANK + MLA_KV_RANK
    kr = jnp.pad(w_in[:, :, o_kr:o_kr + MLA_ROPE], ((0, 0), (0, 0), (MLA_NOPE, LANES - MLA_NOPE - MLA_ROPE)))
    w_proj = jnp.concatenate([w_in[:, :, :o_kr], kr, w_in[:, :, o_kr + MLA_ROPE:]], axis=2).astype(BF16)

    uq = mla_w_uq.reshape(depth, MLA_Q_RANK, MLA_HEADS, MLA_NOPE + MLA_ROPE)
    uq = jnp.pad(uq, ((0, 0), (0, 0), (0, 0), (0, LANES - MLA_NOPE - MLA_ROPE)))
    uq = uq.reshape(depth, MLA_Q_RANK, MLA_QK_W).astype(BF16)
    ukv = mla_w_ukv.reshape(depth, MLA_KV_RANK, MLA_HEADS, MLA_NOPE + MLA_VD)
    kn = jnp.pad(ukv[..., :MLA_NOPE], ((0, 0), (0, 0), (0, 0), (0, LANES - MLA_NOPE)))
    kn = kn.reshape(depth, MLA_KV_RANK, MLA_QK_W)
    vv = ukv[..., MLA_NOPE:].reshape(depth, MLA_KV_RANK, MLA_V_W)
    return w_proj, uq, jnp.concatenate([kn, vv], axis=2).astype(BF16)


def kernel(x, c, ctx, c_ctx, ada_w, ada_b, norm_g, ffn_w_in, ffn_w_out, w_in, diff_lambda,
           diff_subln_g, mla_q_norm_g, mla_w_uq, mla_kv_norm_g, mla_w_ukv, w_branch_diff,
           w_branch_mla, w_branch_fourier, w_out, final_norm_g):
    b, n_lat, d = x.shape
    n_ctx = ctx.shape[1]
    depth = ada_w.shape[0]

    cond_rows = -(-(b + 1) // 8) * 8
    cond = jnp.concatenate([c, c_ctx[None, :], jnp.zeros((cond_rows - b - 1, d), F32)], axis=0)
    mods = _ada_call(cond, ada_w, ada_b).reshape(depth, cond_rows, N_MOD, d)

    tm_ctx = _tile(b * n_ctx, 512)
    tab_x, tab_c = _rope_tables(n_lat, tm_ctx)
    dft_x = _pos_dft_tables(n_lat, FOURIER_GROUP_DIM)
    dft_c = _pos_dft_tables(n_ctx, FOURIER_GROUP_DIM)
    cs128 = _chan_dft_table(FOURIER_GROUP_DIM)

    ffn_in, ffn_out = ffn_w_in.astype(BF16), ffn_w_out.astype(BF16)
    w_proj, uq, ukv = _prep_weights(w_in, mla_w_uq, mla_w_ukv)
    wd, wm, wf, wo = (a.astype(BF16) for a in (w_branch_diff, w_branch_mla, w_branch_fourier, w_out))
    qn, kvn, sub_g = (a[:, None, :] for a in (mla_q_norm_g, mla_kv_norm_g, diff_subln_g))
    final_g = final_norm_g[None, :]

    xs = x
    cs = ctx.reshape(1, b * n_ctx, d)
    per_batch = lambda a: a.reshape(b, n_ctx, a.shape[-1])
    flat = lambda a: a.reshape(1, b * n_ctx, a.shape[-1])
    for l in range(depth):
        last = l == depth - 1
        lam_init = 0.8 - 0.6 * math.exp(-0.3 * l)
        diff = functools.partial(_diff_body, lam_init=lam_init)
        diff_extra = ([_layer(diff_lambda, l), _layer(sub_g, l)], [diff_lambda, sub_g])

        xs = _ffn_call(xs, mods, l, 0, norm_g, 0, ffn_in, ffn_out, 0, 0)
        cs = _ffn_call(cs, mods, l, b, norm_g, 0, ffn_in, ffn_out, 0, 0)

        px = _proj_call(xs, mods, l, 0, norm_g, w_proj, uq, ukv, qn, kvn, cs128, tab_x, True)
        pc = _proj_call(cs, mods, l, b, norm_g, w_proj, uq, ukv, qn, kvn, cs128, tab_c, False)
        qd_x, kd_x, vd_x, qm_x, km_x, vm_x, zc_x, zs_x, gt_x = px
        qd_c, kd_c, vd_c, qm_c, km_c, vm_c, zc_c, zs_c, gt_c = (per_batch(a) for a in pc)

        yd_x = _attn_call(diff, "diff_attn", qd_x, [kd_c, kd_x], [vd_c, vd_x], *diff_extra,
                          DIFF_HEADS, MXU_W, DIFF_W)
        ym_x = _attn_call(_mla_body, "mla_attn", qm_x, [km_c, km_x], [vm_c, vm_x], [], [],
                          MLA_HEADS, LANES, MLA_V_W)
        yf_x = _dft_call(dft_x, zc_x, zs_x)
        xs = _merge_call(xs, mods, l, 0, yd_x, ym_x, yf_x, gt_x, wd, wm, wf, wo)
        xs = _ffn_call(xs, mods, l, 0, norm_g, 2, ffn_in, ffn_out, 1, 6, final_g=final_g if last else None)

        if not last:
            yd_c = _attn_call(diff, "diff_attn_ctx", qd_c, [kd_c], [vd_c], *diff_extra,
                              DIFF_HEADS, MXU_W, DIFF_W)
            ym_c = _attn_call(_mla_body, "mla_attn_ctx", qm_c, [km_c], [vm_c], [], [],
                              MLA_HEADS, LANES, MLA_V_W)
            yf_c = _dft_call(dft_c, zc_c, zs_c)
            cs = _merge_call(cs, mods, l, b, flat(yd_c), flat(ym_c), flat(yf_c), flat(gt_c), wd, wm, wf, wo)
            cs = _ffn_call(cs, mods, l, b, norm_g, 2, ffn_in, ffn_out, 1, 6)
    return xs
```

```python
import functools
import math

import numpy as np
import jax
import jax.numpy as jnp
from jax import lax
from jax.experimental import pallas as pl
from jax.experimental.pallas import tpu as pltpu

F32 = jnp.float32
BF16 = jnp.bfloat16

GRID_W = 64
DIFF_HEADS = 4
DIFF_HD = 64
DIFF_VD = 2 * DIFF_HD
MLA_HEADS = 8
MLA_NOPE = 64
MLA_ROPE = 32
MLA_VD = 64
MLA_Q_RANK = 384
MLA_KV_RANK = 256
FOURIER_GROUPS = 4
FOURIER_GROUP_DIM = 128
N_BRANCHES = 3
ROPE_BASE = 10000.0
RMS_EPS = 1e-6
N_MOD = 9

DIFF_W = DIFF_HEADS * 2 * DIFF_HD
MLA_QK_W = MLA_HEADS * 128
MLA_V_W = MLA_HEADS * MLA_VD
FOURIER_W = FOURIER_GROUPS * FOURIER_GROUP_DIM

LANES = 128
MXU_W = 256
FFN_CHUNK = MXU_W
VMEM_LIMIT = 56 * 1024 * 1024
LOG2E = math.log2(math.e)


def _tile(n, pref):
    t = min(n, pref)
    while n % t:
        t //= 2
    return t


def _fixed(block, idx):
    return pl.BlockSpec(block, lambda *_: idx, pipeline_mode=pl.Buffered(1))


def _whole(a):
    return _fixed(a.shape, (0,) * a.ndim)


def _layer(a, l):
    return _fixed((1,) + a.shape[1:], (l,) + (0,) * (a.ndim - 1))


def _params(n_axes):
    return pltpu.CompilerParams(dimension_semantics=("arbitrary",) * n_axes,
                                vmem_limit_bytes=VMEM_LIMIT)


def _norm_mod(x, g, shift, scale):
    y = x * lax.rsqrt(jnp.mean(x * x, axis=-1, keepdims=True) + RMS_EPS)
    return (y * (g * (1.0 + scale)) + shift).astype(BF16)


def _rms(x, g):
    return x * lax.rsqrt(jnp.mean(x * x, axis=-1, keepdims=True) + RMS_EPS) * g


def _dot(a, b):
    return jnp.dot(a, b, preferred_element_type=F32)


def _ada_body(s_ref, w_ref, b_ref, o_ref):
    s = s_ref[...]
    a = (s * jax.nn.sigmoid(s)).astype(BF16)
    o_ref[0] = _dot(a, w_ref[0].astype(BF16)) + b_ref[0]


def _ada_call(cond, ada_w, ada_b):
    depth, d, n = ada_w.shape
    rows = cond.shape[0]
    tn = _tile(n, 1024)
    return pl.pallas_call(
        _ada_body,
        grid=(depth, n // tn),
        in_specs=[pl.BlockSpec((rows, d), lambda l, j: (0, 0)),
                  pl.BlockSpec((1, d, tn), lambda l, j: (l, 0, j)),
                  pl.BlockSpec((1, 1, tn), lambda l, j: (l, 0, j))],
        out_specs=pl.BlockSpec((1, rows, tn), lambda l, j: (l, 0, j)),
        out_shape=jax.ShapeDtypeStruct((depth, rows, n), F32),
        compiler_params=_params(2),
        name="ada_mod",
    )(cond, ada_w, ada_b.reshape(depth, 1, n))


def _mods_block(mods, l, row0):
    d = mods.shape[-1]
    return pl.BlockSpec((1, 1, N_MOD, d), lambda i, j: (l, row0 + i, 0, 0))


def _ffn_body(x_ref, mod_ref, g_ref, win_ref, wout_ref, *rest, mod0, gi, final):
    if final:
        fg_ref, o_ref = rest
    else:
        (o_ref,) = rest
    x = x_ref[0]
    d_ff = wout_ref.shape[2]
    xm = _norm_mod(x, g_ref[0, gi:gi + 1, :], mod_ref[0, 0, mod0:mod0 + 1, :], mod_ref[0, 0, mod0 + 1:mod0 + 2, :])
    acc = jnp.zeros(x.shape, F32)
    for c in range(d_ff // FFN_CHUNK):
        lo = c * FFN_CHUNK
        gate = _dot(xm, win_ref[0, 0, :, lo:lo + FFN_CHUNK])
        up = _dot(xm, win_ref[0, 0, :, d_ff + lo:d_ff + lo + FFN_CHUNK])
        act = (gate * jax.nn.sigmoid(gate) * up).astype(BF16)
        acc = acc + _dot(act, wout_ref[0, 0, lo:lo + FFN_CHUNK, :])
    y = x + (0.5 * mod_ref[0, 0, mod0 + 2:mod0 + 3, :]) * acc
    if final:
        y = _rms(y, fg_ref[...])
    o_ref[0] = y


def _ffn_call(x, mods, l, row0, norm_g, gi, win, wout, which, mod0, final_g=None, tm=1024):
    s, t, d = x.shape
    tm = _tile(t, tm)
    final = final_g is not None
    in_specs = [pl.BlockSpec((1, tm, d), lambda i, j: (i, j, 0)),
                _mods_block(mods, l, row0),
                _layer(norm_g, l),
                _fixed((1, 1) + win.shape[2:], (l, which, 0, 0)),
                _fixed((1, 1) + wout.shape[2:], (l, which, 0, 0))]
    args = [x, mods, norm_g, win, wout]
    if final:
        in_specs.append(_whole(final_g))
        args.append(final_g)
    return pl.pallas_call(
        functools.partial(_ffn_body, mod0=mod0, gi=gi, final=final),
        grid=(s, t // tm),
        in_specs=in_specs,
        out_specs=pl.BlockSpec((1, tm, d), lambda i, j: (i, j, 0)),
        out_shape=jax.ShapeDtypeStruct(x.shape, F32),
        compiler_params=_params(2),
        name="ffn",
    )(*args)


def _rope(x, cos, sin_signed, chunk):
    lane = lax.broadcasted_iota(jnp.int32, x.shape, 1)
    even = (lane % (2 * chunk)) < chunk
    partner = jnp.where(even, pltpu.roll(x, LANES - chunk, 1), pltpu.roll(x, chunk, 1))
    return x * cos + partner * sin_signed


def _proj_body(x_ref, mod_ref, g_ref, wa_ref, wb_ref, wuq_ref, wukv_ref, qn_ref, kvn_ref, cs_ref, tab_ref,
               qd_ref, kd_ref, vd_ref, qm_ref, km_ref, vm_ref, zc_ref, zs_ref, gt_ref, z_scr,
               *, qscale_d, qscale_m):
    x = x_ref[0]
    xm = _norm_mod(x, g_ref[0, 1:2, :], mod_ref[0, 0, 3:4, :], mod_ref[0, 0, 4:5, :])
    cos_d, sin_d, cos_m, sin_m = tab_ref[0], tab_ref[1], tab_ref[2], tab_ref[3]

    def mm(w_ref, c0, width):
        return _dot(xm, w_ref[0, :, c0:c0 + width])

    c0 = 0
    q = mm(wa_ref, c0, DIFF_W)
    c0 += DIFF_W
    for h in range(DIFF_HEADS):
        sl = slice(h * LANES, (h + 1) * LANES)
        qd_ref[0, :, sl] = (_rope(q[:, sl], cos_d, sin_d, DIFF_HD // 4) * qscale_d).astype(BF16)
    k = mm(wa_ref, c0, DIFF_W)
    c0 += DIFF_W
    for h in range(DIFF_HEADS):
        sl = slice(h * LANES, (h + 1) * LANES)
        kd_ref[0, :, sl] = _rope(k[:, sl], cos_d, sin_d, DIFF_HD // 4).astype(BF16)
    vd_ref[0] = mm(wa_ref, c0, DIFF_W).astype(BF16)
    c0 += DIFF_W

    cq = _rms(mm(wa_ref, c0, MLA_Q_RANK), qn_ref[0]).astype(BF16)
    c0 += MLA_Q_RANK
    qm = _dot(cq, wuq_ref[0])
    for h in range(MLA_HEADS):
        sl = slice(h * LANES, (h + 1) * LANES)
        qm_ref[0, :, sl] = (_rope(qm[:, sl], cos_m, sin_m, MLA_ROPE // 4) * qscale_m).astype(BF16)

    ckv = _rms(mm(wa_ref, c0, MLA_KV_RANK), kvn_ref[0]).astype(BF16)
    c0 += MLA_KV_RANK
    kv = _dot(ckv, wukv_ref[0])
    kr = mm(wa_ref, c0, LANES)
    kr = jnp.where(lax.broadcasted_iota(jnp.int32, kr.shape, 1) < MLA_ROPE, kr, 0.0)
    kr = _rope(kr, cos_m, sin_m, MLA_ROPE // 4)
    for h in range(MLA_HEADS):
        sl = slice(h * LANES, (h + 1) * LANES)
        km_ref[0, :, sl] = (kv[:, sl] + kr).astype(BF16)
    vm_ref[0] = kv[:, MLA_QK_W:].astype(BF16)

    f = mm(wb_ref, 0, FOURIER_W).astype(BF16)
    for gi in range(FOURIER_GROUPS):
        sl = slice(gi * LANES, (gi + 1) * LANES)
        z = _dot(f[:, sl], cs_ref[...])
        z_scr[gi] = z[:, :LANES]
        z_scr[FOURIER_GROUPS + gi] = z[:, LANES:]
    half = z_scr.shape[1] // 2
    for parity in range(2):
        for gi in range(FOURIER_GROUPS):
            sl = slice(parity * FOURIER_W + gi * LANES, parity * FOURIER_W + (gi + 1) * LANES)
            zc_ref[0, :, sl] = z_scr[gi, pl.ds(parity, half, stride=2), :].astype(BF16)
            zs_ref[0, :, sl] = z_scr[FOURIER_GROUPS + gi, pl.ds(parity, half, stride=2), :].astype(BF16)

    d = x.shape[-1]
    for bi in range(N_BRANCHES):
        gt_ref[0, :, bi * d:(bi + 1) * d] = jax.nn.sigmoid(mm(wb_ref, FOURIER_W + bi * d, d)).astype(BF16)


def _proj_call(x, mods, l, row0, norm_g, wa, wb, wuq, wukv, qn, kvn, cs128, tab, tab_per_tile, tm=512):
    s, t, d = x.shape
    tm = _tile(t, tm)
    tok = lambda wd, rows=tm: pl.BlockSpec((1, rows, wd), lambda i, j: (i, j, 0))
    tab_map = (lambda i, j: (0, j, 0)) if tab_per_tile else (lambda i, j: (0, 0, 0))
    full = [DIFF_W, DIFF_W, DIFF_W, MLA_QK_W, MLA_QK_W, MLA_V_W]
    out_specs = ([tok(wd) for wd in full] + [tok(2 * FOURIER_W, tm // 2)] * 2 + [tok(N_BRANCHES * d)])
    out_shape = ([jax.ShapeDtypeStruct((s, t, wd), BF16) for wd in full]
                 + [jax.ShapeDtypeStruct((s, t // 2, 2 * FOURIER_W), BF16)] * 2
                 + [jax.ShapeDtypeStruct((s, t, N_BRANCHES * d), BF16)])
    return pl.pallas_call(
        functools.partial(_proj_body,
                          qscale_d=DIFF_HD ** -0.5 * LOG2E,
                          qscale_m=(MLA_NOPE + MLA_ROPE) ** -0.5 * LOG2E),
        grid=(s, t // tm),
        in_specs=[tok(d), _mods_block(mods, l, row0), _layer(norm_g, l),
                  _layer(wa, l), _layer(wb, l), _layer(wuq, l), _layer(wukv, l), _layer(qn, l), _layer(kvn, l),
                  _whole(cs128), pl.BlockSpec((4, tm, LANES), tab_map)],
        out_specs=out_specs,
        out_shape=out_shape,
        scratch_shapes=[pltpu.VMEM((2 * FOURIER_GROUPS, tm, LANES), F32)],
        compiler_params=_params(2),
        name="in_proj",
    )(x, mods, norm_g, wa, wb, wuq, wukv, qn, kvn, cs128, tab)


def _fill_kv(srcs_k, srcs_v, k_scr, v_scr, heads, vd):
    vw = v_scr.shape[1] // heads
    r0 = 0
    for k_ref, v_ref in zip(srcs_k, srcs_v):
        n = k_ref.shape[1]
        k_scr[r0:r0 + n, :] = k_ref[0]
        lane = lax.broadcasted_iota(jnp.int32, (n, vw - vd), 1)
        ones_col = jnp.where(lane == 0, 1.0, 0.0).astype(BF16)
        for h in range(heads):
            v_scr[r0:r0 + n, h * vw:h * vw + vd] = v_ref[0, :, h * vd:(h + 1) * vd]
            v_scr[r0:r0 + n, h * vw + vd:(h + 1) * vw] = ones_col
        r0 += n


def _softmax_pv(s, v_aug, vd):
    m = jnp.max(s, axis=-1, keepdims=True)
    e = jnp.exp2(s - m).astype(BF16)
    r = _dot(e, v_aug)
    return r[:, :vd], r[:, vd:vd + 1]


SMALL_UNIT_ROWS = 128


def _head_row_units(heads, rows):
    if rows <= 2 * SMALL_UNIT_ROWS:
        return [(h, 0, rows) for h in range(heads)]
    units = [(0, 0, SMALL_UNIT_ROWS), (0, SMALL_UNIT_ROWS, rows)]
    units += [(h, 0, rows) for h in range(1, heads - 1)]
    units += [(heads - 1, 0, rows - SMALL_UNIT_ROWS), (heads - 1, rows - SMALL_UNIT_ROWS, rows)]
    return units


def _qk(q, k):
    return lax.dot_general(q, k, (((1,), (1,)), ((), ())), preferred_element_type=F32)


def _diff_body(*refs, n_src, lam_init):
    q_ref, lam_ref, sg_ref = refs[0], refs[1], refs[2]
    k_srcs = refs[3:3 + n_src]
    v_srcs = refs[3 + n_src:3 + 2 * n_src]
    o_ref, k_scr, v_scr = refs[3 + 2 * n_src:]

    @pl.when(pl.program_id(1) == 0)
    def _():
        _fill_kv(k_srcs, v_srcs, k_scr, v_scr, DIFF_HEADS, DIFF_VD)

    lp = lam_ref[0]
    lam = (jnp.exp(jnp.sum(lp[0:1] * lp[1:2], axis=-1, keepdims=True))
           - jnp.exp(jnp.sum(lp[2:3] * lp[3:4], axis=-1, keepdims=True)) + lam_init)
    zero = jnp.zeros((), BF16)
    vw = v_scr.shape[1] // DIFF_HEADS
    for h, r0, r1 in _head_row_units(DIFF_HEADS, q_ref.shape[1]):
        sl = slice(h * LANES, (h + 1) * LANES)
        qh, kh, vh = q_ref[0, r0:r1, sl], k_scr[:, sl], v_scr[:, h * vw:(h + 1) * vw]
        first = lax.broadcasted_iota(jnp.int32, qh.shape, 1) < DIFF_HD
        o1, l1 = _softmax_pv(_qk(jnp.where(first, qh, zero), kh), vh, DIFF_VD)
        o2, l2 = _softmax_pv(_qk(jnp.where(first, zero, qh), kh), vh, DIFF_VD)
        o = o1 * (1.0 / l1) - o2 * (lam / l2)
        o_ref[0, r0:r1, sl] = (_rms(o, sg_ref[0]) * (1.0 - lam_init)).astype(BF16)


def _mla_body(*refs, n_src):
    q_ref = refs[0]
    k_srcs = refs[1:1 + n_src]
    v_srcs = refs[1 + n_src:1 + 2 * n_src]
    o_ref, k_scr, v_scr = refs[1 + 2 * n_src:]

    @pl.when(pl.program_id(1) == 0)
    def _():
        _fill_kv(k_srcs, v_srcs, k_scr, v_scr, MLA_HEADS, MLA_VD)

    vw = v_scr.shape[1] // MLA_HEADS
    for h, r0, r1 in _head_row_units(MLA_HEADS, q_ref.shape[1]):
        sl = slice(h * LANES, (h + 1) * LANES)
        o, l = _softmax_pv(_qk(q_ref[0, r0:r1, sl], k_scr[:, sl]), v_scr[:, h * vw:(h + 1) * vw], MLA_VD)
        o_ref[0, r0:r1, h * MLA_VD:(h + 1) * MLA_VD] = (o * (1.0 / l)).astype(BF16)


def _attn_call(body, name, q, ks, vs, extra_specs, extra, heads, v_aug_w, out_w, tq=512):
    b, lq, wq = q.shape
    tq = _tile(lq, tq)
    n_src = len(ks)
    lk = sum(k.shape[1] for k in ks)
    full = lambda a: pl.BlockSpec((1,) + a.shape[1:], lambda i, j: (i, 0, 0))
    return pl.pallas_call(
        functools.partial(body, n_src=n_src),
        grid=(b, lq // tq),
        in_specs=([pl.BlockSpec((1, tq, wq), lambda i, j: (i, j, 0))] + extra_specs
                  + [full(k) for k in ks] + [full(v) for v in vs]),
        out_specs=pl.BlockSpec((1, tq, out_w), lambda i, j: (i, j, 0)),
        out_shape=jax.ShapeDtypeStruct((b, lq, out_w), BF16),
        scratch_shapes=[pltpu.VMEM((lk, ks[0].shape[2]), BF16),
                        pltpu.VMEM((lk, heads * v_aug_w), BF16)],
        compiler_params=_params(2),
        name=name,
    )(q, *extra, *ks, *vs)


def _dft_body(t_ref, zc_ref, zs_ref, o_ref):
    w = o_ref.shape[-1]
    even = _dot(t_ref[0], zc_ref[0, :, :w]) + _dot(t_ref[1], zs_ref[0, :, :w])
    odd = _dot(t_ref[2], zc_ref[0, :, w:]) + _dot(t_ref[3], zs_ref[0, :, w:])
    o_ref[0, 0] = (even + odd).astype(BF16)
    o_ref[0, 1] = (even - odd).astype(BF16)


def _dft_call(tables, zc, zs, tm=1024):
    b, half, w2 = zc.shape
    t, w = 2 * half, w2 // 2
    tm = _tile(half, tm)
    out = pl.pallas_call(
        _dft_body,
        grid=(half // tm, b),
        in_specs=[pl.BlockSpec((4, tm, half), lambda i, j: (0, i, 0)),
                  pl.BlockSpec((1, half, 2 * w), lambda i, j: (j, 0, 0)),
                  pl.BlockSpec((1, half, 2 * w), lambda i, j: (j, 0, 0))],
        out_specs=pl.BlockSpec((1, 2, tm, w), lambda i, j: (j, 0, i, 0)),
        out_shape=jax.ShapeDtypeStruct((b, 2, half, w), BF16),
        compiler_params=_params(2),
        name="pos_dft",
    )(tables, zc, zs)
    return out.reshape(b, t, w)


def _merge_body(x_ref, mod_ref, yd_ref, ym_ref, yf_ref, gt_ref, wd_ref, wm_ref, wf_ref, wo_ref, o_ref):
    x = x_ref[0]
    d = x.shape[-1]
    merged = (gt_ref[0, :, 0:d].astype(F32) * _dot(yd_ref[0], wd_ref[0])
              + gt_ref[0, :, d:2 * d].astype(F32) * _dot(ym_ref[0], wm_ref[0])
              + gt_ref[0, :, 2 * d:3 * d].astype(F32) * _dot(yf_ref[0], wf_ref[0]))
    o_ref[0] = x + mod_ref[0, 0, 5:6, :] * _dot(merged.astype(BF16), wo_ref[0])


def _merge_call(x, mods, l, row0, yd, ym, yf, gt, wd, wm, wf, wo, tm=512):
    s, t, d = x.shape
    tm = _tile(t, tm)
    tok = lambda wd_: pl.BlockSpec((1, tm, wd_), lambda i, j: (i, j, 0))
    return pl.pallas_call(
        _merge_body,
        grid=(s, t // tm),
        in_specs=[tok(d), _mods_block(mods, l, row0),
                  tok(yd.shape[2]), tok(ym.shape[2]), tok(yf.shape[2]), tok(gt.shape[2]),
                  _layer(wd, l), _layer(wm, l), _layer(wf, l), _layer(wo, l)],
        out_specs=tok(d),
        out_shape=jax.ShapeDtypeStruct(x.shape, F32),
        compiler_params=_params(2),
        name="merge_out",
    )(x, mods, yd, ym, yf, gt, wd, wm, wf, wo)


def _rope_tables(n_lat, tm_ctx):
    rows = np.arange(n_lat) // GRID_W
    cols = np.arange(n_lat) % GRID_W

    def cos_sin(dim):
        nf = dim // 4
        freqs = np.power(np.float32(ROPE_BASE), -np.arange(nf, dtype=np.float32) / np.float32(nf)).astype(np.float32)
        ar = rows.astype(np.float32)[:, None] * freqs[None, :]
        ac = cols.astype(np.float32)[:, None] * freqs[None, :]
        ang = np.concatenate([ar, ar, ac, ac], axis=-1).astype(np.float64)
        sign = np.concatenate([-np.ones(nf), np.ones(nf), -np.ones(nf), np.ones(nf)])
        return np.cos(ang), np.sin(ang) * sign[None, :]

    cd, sd = cos_sin(DIFF_HD)
    cos_d = np.tile(cd, (1, LANES // DIFF_HD))
    sin_d = np.tile(sd, (1, LANES // DIFF_HD))
    cm, sm = cos_sin(MLA_ROPE)
    cos_m = np.ones((n_lat, LANES))
    sin_m = np.zeros((n_lat, LANES))
    cos_m[:, :MLA_ROPE] = cm
    sin_m[:, :MLA_ROPE] = sm
    lat = np.stack([cos_d, sin_d, cos_m, sin_m]).astype(np.float32)
    ident = np.stack([np.ones((tm_ctx, LANES)), np.zeros((tm_ctx, LANES))] * 2).astype(np.float32)
    return jnp.asarray(lat), jnp.asarray(ident)


def _angles(rows, cols, n):
    return 2.0 * np.pi * ((rows[:, None] * cols[None, :]) % n) / n


def _pos_dft_tables(n, group):
    j, m = np.arange(n // 2), np.arange(n // 2)
    scale = 1.0 / math.sqrt(n * group)
    ae, ao = _angles(j, 2 * m, n), _angles(j, 2 * m + 1, n)
    return jnp.asarray(np.stack([np.cos(ae), -np.sin(ae), np.cos(ao), -np.sin(ao)]) * scale, BF16)


def _chan_dft_table(group):
    a = _angles(np.arange(group), np.arange(group), group)
    return jnp.asarray(np.concatenate([np.cos(a), np.sin(a)], axis=1), BF16)


def _prep_weights(w_in, mla_w_uq, mla_w_ukv):
    depth = w_in.shape[0]
    o_kr = 3 * DIFF_W + MLA_Q_RANK + MLA_KV_RANK
    wa = w_in[:, :, :o_kr + LANES].astype(BF16)
    wb = w_in[:, :, o_kr + MLA_ROPE:].astype(BF16)

    tail = LANES - MLA_NOPE - MLA_ROPE
    uq = mla_w_uq.reshape(depth, MLA_Q_RANK, MLA_HEADS, MLA_NOPE + MLA_ROPE)
    uq = jnp.concatenate([uq[..., MLA_NOPE:], uq[..., :MLA_NOPE], jnp.zeros(uq.shape[:3] + (tail,), uq.dtype)], axis=-1)
    uq = uq.reshape(depth, MLA_Q_RANK, MLA_QK_W).astype(BF16)
    ukv = mla_w_ukv.reshape(depth, MLA_KV_RANK, MLA_HEADS, MLA_NOPE + MLA_VD)
    kn = jnp.pad(ukv[..., :MLA_NOPE], ((0, 0), (0, 0), (0, 0), (MLA_ROPE, tail)))
    kn = kn.reshape(depth, MLA_KV_RANK, MLA_QK_W)
    vv = ukv[..., MLA_NOPE:].reshape(depth, MLA_KV_RANK, MLA_V_W)
    return wa, wb, uq, jnp.concatenate([kn, vv], axis=2).astype(BF16)


def kernel(x, c, ctx, c_ctx, ada_w, ada_b, norm_g, ffn_w_in, ffn_w_out, w_in, diff_lambda,
           diff_subln_g, mla_q_norm_g, mla_w_uq, mla_kv_norm_g, mla_w_ukv, w_branch_diff,
           w_branch_mla, w_branch_fourier, w_out, final_norm_g):
    b, n_lat, d = x.shape
    n_ctx = ctx.shape[1]
    depth = ada_w.shape[0]

    cond_rows = -(-(b + 1) // 8) * 8
    cond = jnp.concatenate([c, c_ctx[None, :], jnp.zeros((cond_rows - b - 1, d), F32)], axis=0)
    mods = _ada_call(cond, ada_w, ada_b).reshape(depth, cond_rows, N_MOD, d)

    tm_ctx = _tile(b * n_ctx, 512)
    tab_x, tab_c = _rope_tables(n_lat, tm_ctx)
    dft_x = _pos_dft_tables(n_lat, FOURIER_GROUP_DIM)
    dft_c = _pos_dft_tables(n_ctx, FOURIER_GROUP_DIM)
    cs128 = _chan_dft_table(FOURIER_GROUP_DIM)

    ffn_in, ffn_out = ffn_w_in.astype(BF16), ffn_w_out.astype(BF16)
    wa, wb, uq, ukv = _prep_weights(w_in, mla_w_uq, mla_w_ukv)
    wd, wm, wf, wo = (a.astype(BF16) for a in (w_branch_diff, w_branch_mla, w_branch_fourier, w_out))
    qn, kvn, sub_g = (a[:, None, :] for a in (mla_q_norm_g, mla_kv_norm_g, diff_subln_g))
    final_g = final_norm_g[None, :]

    xs = x
    cs = ctx.reshape(1, b * n_ctx, d)
    per_batch = lambda a: a.reshape(b, a.shape[1] // b, a.shape[-1])
    flat = lambda a: a.reshape(1, b * n_ctx, a.shape[-1])
    for l in range(depth):
        last = l == depth - 1
        lam_init = 0.8 - 0.6 * math.exp(-0.3 * l)
        diff = functools.partial(_diff_body, lam_init=lam_init)
        diff_extra = ([_layer(diff_lambda, l), _layer(sub_g, l)], [diff_lambda, sub_g])

        xs = _ffn_call(xs, mods, l, 0, norm_g, 0, ffn_in, ffn_out, 0, 0)
        cs = _ffn_call(cs, mods, l, b, norm_g, 0, ffn_in, ffn_out, 0, 0)

        px = _proj_call(xs, mods, l, 0, norm_g, wa, wb, uq, ukv, qn, kvn, cs128, tab_x, True)
        pc = _proj_call(cs, mods, l, b, norm_g, wa, wb, uq, ukv, qn, kvn, cs128, tab_c, False)
        qd_x, kd_x, vd_x, qm_x, km_x, vm_x, zc_x, zs_x, gt_x = px
        qd_c, kd_c, vd_c, qm_c, km_c, vm_c, zc_c, zs_c, gt_c = (per_batch(a) for a in pc)

        yd_x = _attn_call(diff, "diff_attn", qd_x, [kd_c, kd_x], [vd_c, vd_x], *diff_extra,
                          DIFF_HEADS, MXU_W, DIFF_W)
        ym_x = _attn_call(_mla_body, "mla_attn", qm_x, [km_c, km_x], [vm_c, vm_x], [], [],
                          MLA_HEADS, LANES, MLA_V_W)
        yf_x = _dft_call(dft_x, zc_x, zs_x)
        xs = _merge_call(xs, mods, l, 0, yd_x, ym_x, yf_x, gt_x, wd, wm, wf, wo)
        xs = _ffn_call(xs, mods, l, 0, norm_g, 2, ffn_in, ffn_out, 1, 6, final_g=final_g if last else None)

        if not last:
            yd_c = _attn_call(diff, "diff_attn_ctx", qd_c, [kd_c], [vd_c], *diff_extra,
                              DIFF_HEADS, MXU_W, DIFF_W)
            ym_c = _attn_call(_mla_body, "mla_attn_ctx", qm_c, [km_c], [vm_c], [], [],
                              MLA_HEADS, LANES, MLA_V_W)
            yf_c = _dft_call(dft_c, zc_c, zs_c)
            cs = _merge_call(cs, mods, l, b, flat(yd_c), flat(ym_c), flat(yf_c), flat(gt_c), wd, wm, wf, wo)
            cs = _ffn_call(cs, mods, l, b, norm_g, 2, ffn_in, ffn_out, 1, 6)
    return xs
```

```python
import functools
import math

import numpy as np
import jax
import jax.numpy as jnp
from jax import lax
from jax.experimental import pallas as pl
from jax.experimental.pallas import tpu as pltpu

F32 = jnp.float32
BF16 = jnp.bfloat16

GRID_W = 64
DIFF_HEADS = 4
DIFF_HD = 64
DIFF_VD = 2 * DIFF_HD
MLA_HEADS = 8
MLA_NOPE = 64
MLA_ROPE = 32
MLA_VD = 64
MLA_Q_RANK = 384
MLA_KV_RANK = 256
FOURIER_GROUPS = 4
FOURIER_GROUP_DIM = 128
N_BRANCHES = 3
ROPE_BASE = 10000.0
RMS_EPS = 1e-6
N_MOD = 9

DIFF_W = DIFF_HEADS * 2 * DIFF_HD
MLA_QK_W = MLA_HEADS * 128
MLA_V_W = MLA_HEADS * MLA_VD
FOURIER_W = FOURIER_GROUPS * FOURIER_GROUP_DIM

LANES = 128
MXU_W = 256
FFN_CHUNK = MXU_W
VMEM_LIMIT = 56 * 1024 * 1024
FFN_VMEM_LIMIT = 60 * 1024 * 1024
LOG2E = math.log2(math.e)


def _tile(n, pref):
    t = min(n, pref)
    while n % t:
        t //= 2
    return t


def _fixed(block, idx):
    return pl.BlockSpec(block, lambda *_: idx, pipeline_mode=pl.Buffered(1))


def _whole(a):
    return _fixed(a.shape, (0,) * a.ndim)


def _layer(a, l):
    return _fixed((1,) + a.shape[1:], (l,) + (0,) * (a.ndim - 1))


def _params(n_axes, vmem_limit=VMEM_LIMIT):
    return pltpu.CompilerParams(dimension_semantics=("arbitrary",) * n_axes,
                                vmem_limit_bytes=vmem_limit)


def _norm_mod(x, g, shift, scale):
    y = x * lax.rsqrt(jnp.mean(x * x, axis=-1, keepdims=True) + RMS_EPS)
    return (y * (g * (1.0 + scale)) + shift).astype(BF16)


def _rms(x, g):
    return x * lax.rsqrt(jnp.mean(x * x, axis=-1, keepdims=True) + RMS_EPS) * g


def _dot(a, b):
    return jnp.dot(a, b, preferred_element_type=F32)


def _ada_body(s_ref, w_ref, b_ref, o_ref):
    s = s_ref[...]
    a = (s * jax.nn.sigmoid(s)).astype(BF16)
    o_ref[0] = _dot(a, w_ref[0].astype(BF16)) + b_ref[0]


def _ada_call(cond, ada_w, ada_b):
    depth, d, n = ada_w.shape
    rows = cond.shape[0]
    tn = _tile(n, 1024)
    return pl.pallas_call(
        _ada_body,
        grid=(depth, n // tn),
        in_specs=[pl.BlockSpec((rows, d), lambda l, j: (0, 0)),
                  pl.BlockSpec((1, d, tn), lambda l, j: (l, 0, j)),
                  pl.BlockSpec((1, 1, tn), lambda l, j: (l, 0, j))],
        out_specs=pl.BlockSpec((1, rows, tn), lambda l, j: (l, 0, j)),
        out_shape=jax.ShapeDtypeStruct((depth, rows, n), F32),
        compiler_params=_params(2),
        name="ada_mod",
    )(cond, ada_w, ada_b.reshape(depth, 1, n))


def _mods_block(mods, l, row0):
    d = mods.shape[-1]
    return pl.BlockSpec((1, 1, N_MOD, d), lambda i, j: (l, row0 + i, 0, 0))


def _ffn_body(x_ref, mod_ref, g_ref, win_ref, wout_ref, *rest, mod0, gi, final):
    if final:
        fg_ref, o_ref = rest
    else:
        (o_ref,) = rest
    x = x_ref[0]
    d_ff = wout_ref.shape[2]
    xm = _norm_mod(x, g_ref[0, gi:gi + 1, :], mod_ref[0, 0, mod0:mod0 + 1, :], mod_ref[0, 0, mod0 + 1:mod0 + 2, :])
    acc = jnp.zeros(x.shape, F32)
    for c in range(d_ff // FFN_CHUNK):
        lo = c * FFN_CHUNK
        gate = _dot(xm, win_ref[0, 0, :, lo:lo + FFN_CHUNK])
        up = _dot(xm, win_ref[0, 0, :, d_ff + lo:d_ff + lo + FFN_CHUNK])
        act = (gate * jax.nn.sigmoid(gate) * up).astype(BF16)
        acc = acc + _dot(act, wout_ref[0, 0, lo:lo + FFN_CHUNK, :])
    y = x + (0.5 * mod_ref[0, 0, mod0 + 2:mod0 + 3, :]) * acc
    if final:
        y = _rms(y, fg_ref[...])
    o_ref[0] = y


def _ffn_call(x, mods, l, row0, norm_g, gi, win, wout, which, mod0, final_g=None, tm=1024):
    s, t, d = x.shape
    tm = _tile(t, tm)
    final = final_g is not None
    in_specs = [pl.BlockSpec((1, tm, d), lambda i, j: (i, j, 0)),
                _mods_block(mods, l, row0),
                _layer(norm_g, l),
                _fixed((1, 1) + win.shape[2:], (l, which, 0, 0)),
                _fixed((1, 1) + wout.shape[2:], (l, which, 0, 0))]
    args = [x, mods, norm_g, win, wout]
    if final:
        in_specs.append(_whole(final_g))
        args.append(final_g)
    return pl.pallas_call(
        functools.partial(_ffn_body, mod0=mod0, gi=gi, final=final),
        grid=(s, t // tm),
        in_specs=in_specs,
        out_specs=pl.BlockSpec((1, tm, d), lambda i, j: (i, j, 0)),
        out_shape=jax.ShapeDtypeStruct(x.shape, F32),
        compiler_params=_params(2, FFN_VMEM_LIMIT),
        name="ffn",
    )(*args)


def _rope(x, cos, sin_signed, chunk):
    lane = lax.broadcasted_iota(jnp.int32, x.shape, 1)
    even = (lane % (2 * chunk)) < chunk
    partner = jnp.where(even, pltpu.roll(x, LANES - chunk, 1), pltpu.roll(x, chunk, 1))
    return x * cos + partner * sin_signed


def _proj_body(x_ref, mod_ref, g_ref, wa_ref, wb_ref, wuq_ref, wukv_ref, qn_ref, kvn_ref, cs_ref, tab_ref,
               qd_ref, kd_ref, vd_ref, qm_ref, km_ref, vm_ref, zc_ref, zs_ref, gt_ref, z_scr,
               *, qscale_d, qscale_m):
    x = x_ref[0]
    xm = _norm_mod(x, g_ref[0, 1:2, :], mod_ref[0, 0, 3:4, :], mod_ref[0, 0, 4:5, :])
    cos_d, sin_d, cos_m, sin_m = tab_ref[0], tab_ref[1], tab_ref[2], tab_ref[3]

    def mm(w_ref, c0, width):
        return _dot(xm, w_ref[0, :, c0:c0 + width])

    o_cq = 3 * DIFF_W
    o_ckv = o_cq + MLA_Q_RANK
    o_kr = o_ckv + MLA_KV_RANK
    cq = _rms(mm(wa_ref, o_cq, MLA_Q_RANK), qn_ref[0]).astype(BF16)
    ckv = _rms(mm(wa_ref, o_ckv, MLA_KV_RANK), kvn_ref[0]).astype(BF16)
    kr = mm(wa_ref, o_kr, LANES)
    kr = jnp.where(lax.broadcasted_iota(jnp.int32, kr.shape, 1) < MLA_ROPE, kr, 0.0)
    kr = _rope(kr, cos_m, sin_m, MLA_ROPE // 4)
    qm = _dot(cq, wuq_ref[0])
    for h in range(MLA_HEADS):
        sl = slice(h * LANES, (h + 1) * LANES)
        qm_ref[0, :, sl] = (_rope(qm[:, sl], cos_m, sin_m, MLA_ROPE // 4) * qscale_m).astype(BF16)
    kv = _dot(ckv, wukv_ref[0])
    for h in range(MLA_HEADS):
        sl = slice(h * LANES, (h + 1) * LANES)
        km_ref[0, :, sl] = (kv[:, sl] + kr).astype(BF16)
    vm_ref[0] = kv[:, MLA_QK_W:].astype(BF16)

    q = mm(wa_ref, 0, DIFF_W)
    for h in range(DIFF_HEADS):
        sl = slice(h * LANES, (h + 1) * LANES)
        qd_ref[0, :, sl] = (_rope(q[:, sl], cos_d, sin_d, DIFF_HD // 4) * qscale_d).astype(BF16)
    k = mm(wa_ref, DIFF_W, DIFF_W)
    for h in range(DIFF_HEADS):
        sl = slice(h * LANES, (h + 1) * LANES)
        kd_ref[0, :, sl] = _rope(k[:, sl], cos_d, sin_d, DIFF_HD // 4).astype(BF16)

    d = x.shape[-1]
    for bi in range(N_BRANCHES):
        gt_ref[0, :, bi * d:(bi + 1) * d] = (0.5 * jnp.tanh(0.5 * mm(wb_ref, FOURIER_W + bi * d, d)) + 0.5).astype(BF16)

    f = mm(wb_ref, 0, FOURIER_W).astype(BF16)
    for gi in range(FOURIER_GROUPS):
        sl = slice(gi * LANES, (gi + 1) * LANES)
        z = _dot(f[:, sl], cs_ref[...])
        z_scr[gi] = z[:, :LANES]
        z_scr[FOURIER_GROUPS + gi] = z[:, LANES:]
    half = z_scr.shape[1] // 2
    for parity in range(2):
        for gi in range(FOURIER_GROUPS):
            sl = slice(parity * FOURIER_W + gi * LANES, parity * FOURIER_W + (gi + 1) * LANES)
            zc_ref[0, :, sl] = z_scr[gi, pl.ds(parity, half, stride=2), :].astype(BF16)
            zs_ref[0, :, sl] = z_scr[FOURIER_GROUPS + gi, pl.ds(parity, half, stride=2), :].astype(BF16)
    vd_ref[0] = mm(wa_ref, 2 * DIFF_W, DIFF_W).astype(BF16)


def _proj_call(x, mods, l, row0, norm_g, wa, wb, wuq, wukv, qn, kvn, cs128, tab, tab_per_tile, tm=512):
    s, t, d = x.shape
    tm = _tile(t, tm)
    tok = lambda wd, rows=tm: pl.BlockSpec((1, rows, wd), lambda i, j: (i, j, 0))
    tab_map = (lambda i, j: (0, j, 0)) if tab_per_tile else (lambda i, j: (0, 0, 0))
    full = [DIFF_W, DIFF_W, DIFF_W, MLA_QK_W, MLA_QK_W, MLA_V_W]
    out_specs = ([tok(wd) for wd in full] + [tok(2 * FOURIER_W, tm // 2)] * 2 + [tok(N_BRANCHES * d)])
    out_shape = ([jax.ShapeDtypeStruct((s, t, wd), BF16) for wd in full]
                 + [jax.ShapeDtypeStruct((s, t // 2, 2 * FOURIER_W), BF16)] * 2
                 + [jax.ShapeDtypeStruct((s, t, N_BRANCHES * d), BF16)])
    return pl.pallas_call(
        functools.partial(_proj_body,
                          qscale_d=DIFF_HD ** -0.5 * LOG2E,
                          qscale_m=(MLA_NOPE + MLA_ROPE) ** -0.5 * LOG2E),
        grid=(s, t // tm),
        in_specs=[tok(d), _mods_block(mods, l, row0), _layer(norm_g, l),
                  _layer(wa, l), _layer(wb, l), _layer(wuq, l), _layer(wukv, l), _layer(qn, l), _layer(kvn, l),
                  _whole(cs128), pl.BlockSpec((4, tm, LANES), tab_map)],
        out_specs=out_specs,
        out_shape=out_shape,
        scratch_shapes=[pltpu.VMEM((2 * FOURIER_GROUPS, tm, LANES), F32)],
        compiler_params=_params(2),
        name="in_proj",
    )(x, mods, norm_g, wa, wb, wuq, wukv, qn, kvn, cs128, tab)


def _fill_kv(srcs_k, srcs_v, k_scr, v_scr, heads, vd):
    vw = v_scr.shape[1] // heads
    r0 = 0
    for k_ref, v_ref in zip(srcs_k, srcs_v):
        n = k_ref.shape[1]
        k_scr[r0:r0 + n, :] = k_ref[0]
        lane = lax.broadcasted_iota(jnp.int32, (n, vw - vd), 1)
        ones_col = jnp.where(lane == 0, 1.0, 0.0).astype(BF16)
        for h in range(heads):
            v_scr[r0:r0 + n, h * vw:h * vw + vd] = v_ref[0, :, h * vd:(h + 1) * vd]
            v_scr[r0:r0 + n, h * vw + vd:(h + 1) * vw] = ones_col
        r0 += n


def _softmax_pv(s, v_aug, vd):
    m = jnp.max(s, axis=-1, keepdims=True)
    e = jnp.exp2(s - m).astype(BF16)
    r = _dot(e, v_aug)
    return r[:, :vd], r[:, vd:vd + 1]


SMALL_UNIT_ROWS = 128


def _head_row_units(heads, rows):
    if rows <= 2 * SMALL_UNIT_ROWS:
        return [(h, 0, rows) for h in range(heads)]
    units = [(0, 0, SMALL_UNIT_ROWS), (0, SMALL_UNIT_ROWS, rows)]
    units += [(h, 0, rows) for h in range(1, heads - 1)]
    units += [(heads - 1, 0, rows - SMALL_UNIT_ROWS), (heads - 1, rows - SMALL_UNIT_ROWS, rows)]
    return units


def _qk(q, k):
    return lax.dot_general(q, k, (((1,), (1,)), ((), ())), preferred_element_type=F32)


def _diff_body(*refs, n_src, lam_init):
    q_ref, lam_ref, sg_ref = refs[0], refs[1], refs[2]
    k_srcs = refs[3:3 + n_src]
    v_srcs = refs[3 + n_src:3 + 2 * n_src]
    o_ref, k_scr, v_scr = refs[3 + 2 * n_src:]

    @pl.when(pl.program_id(1) == 0)
    def _():
        _fill_kv(k_srcs, v_srcs, k_scr, v_scr, DIFF_HEADS, DIFF_VD)

    lp = lam_ref[0]
    lam = (jnp.exp(jnp.sum(lp[0:1] * lp[1:2], axis=-1, keepdims=True))
           - jnp.exp(jnp.sum(lp[2:3] * lp[3:4], axis=-1, keepdims=True)) + lam_init)
    zero = jnp.zeros((), BF16)
    vw = v_scr.shape[1] // DIFF_HEADS
    for h, r0, r1 in _head_row_units(DIFF_HEADS, q_ref.shape[1]):
        sl = slice(h * LANES, (h + 1) * LANES)
        qh, kh, vh = q_ref[0, r0:r1, sl], k_scr[:, sl], v_scr[:, h * vw:(h + 1) * vw]
        first = lax.broadcasted_iota(jnp.int32, qh.shape, 1) < DIFF_HD
        o1, l1 = _softmax_pv(_qk(jnp.where(first, qh, zero), kh), vh, DIFF_VD)
        o2, l2 = _softmax_pv(_qk(jnp.where(first, zero, qh), kh), vh, DIFF_VD)
        o = o1 * (1.0 / l1) - o2 * (lam / l2)
        o_ref[0, r0:r1, sl] = (_rms(o, sg_ref[0]) * (1.0 - lam_init)).astype(BF16)


def _mla_body(*refs, n_src):
    q_ref = refs[0]
    k_srcs = refs[1:1 + n_src]
    v_srcs = refs[1 + n_src:1 + 2 * n_src]
    o_ref, k_scr, v_scr = refs[1 + 2 * n_src:]

    @pl.when(pl.program_id(1) == 0)
    def _():
        _fill_kv(k_srcs, v_srcs, k_scr, v_scr, MLA_HEADS, MLA_VD)

    vw = v_scr.shape[1] // MLA_HEADS
    for h, r0, r1 in _head_row_units(MLA_HEADS, q_ref.shape[1]):
        sl = slice(h * LANES, (h + 1) * LANES)
        o, l = _softmax_pv(_qk(q_ref[0, r0:r1, sl], k_scr[:, sl]), v_scr[:, h * vw:(h + 1) * vw], MLA_VD)
        o_ref[0, r0:r1, h * MLA_VD:(h + 1) * MLA_VD] = (o * (1.0 / l)).astype(BF16)


def _attn_call(body, name, q, ks, vs, extra_specs, extra, heads, v_aug_w, out_w, tq=512):
    b, lq, wq = q.shape
    tq = _tile(lq, tq)
    n_src = len(ks)
    lk = sum(k.shape[1] for k in ks)
    full = lambda a: pl.BlockSpec((1,) + a.shape[1:], lambda i, j: (i, 0, 0))
    return pl.pallas_call(
        functools.partial(body, n_src=n_src),
        grid=(b, lq // tq),
        in_specs=([pl.BlockSpec((1, tq, wq), lambda i, j: (i, j, 0))] + extra_specs
                  + [full(k) for k in ks] + [full(v) for v in vs]),
        out_specs=pl.BlockSpec((1, tq, out_w), lambda i, j: (i, j, 0)),
        out_shape=jax.ShapeDtypeStruct((b, lq, out_w), BF16),
        scratch_shapes=[pltpu.VMEM((lk, ks[0].shape[2]), BF16),
                        pltpu.VMEM((lk, heads * v_aug_w), BF16)],
        compiler_params=_params(2),
        name=name,
    )(q, *extra, *ks, *vs)


def _dft_body(t_ref, zc_ref, zs_ref, o_ref):
    w = o_ref.shape[-1]
    even = _dot(t_ref[0], zc_ref[0, :, :w]) + _dot(t_ref[1], zs_ref[0, :, :w])
    odd = _dot(t_ref[2], zc_ref[0, :, w:]) + _dot(t_ref[3], zs_ref[0, :, w:])
    o_ref[0, 0] = (even + odd).astype(BF16)
    o_ref[0, 1] = (even - odd).astype(BF16)


def _dft_call(tables, zc, zs, tm=1024):
    b, half, w2 = zc.shape
    t, w = 2 * half, w2 // 2
    tm = _tile(half, tm)
    out = pl.pallas_call(
        _dft_body,
        grid=(half // tm, b),
        in_specs=[pl.BlockSpec((4, tm, half), lambda i, j: (0, i, 0)),
                  pl.BlockSpec((1, half, 2 * w), lambda i, j: (j, 0, 0)),
                  pl.BlockSpec((1, half, 2 * w), lambda i, j: (j, 0, 0))],
        out_specs=pl.BlockSpec((1, 2, tm, w), lambda i, j: (j, 0, i, 0)),
        out_shape=jax.ShapeDtypeStruct((b, 2, half, w), BF16),
        compiler_params=_params(2),
        name="pos_dft",
    )(tables, zc, zs)
    return out.reshape(b, t, w)


def _merge_body(x_ref, mod_ref, yd_ref, ym_ref, yf_ref, gt_ref, wd_ref, wm_ref, wf_ref, wo_ref, o_ref):
    x = x_ref[0]
    d = x.shape[-1]
    merged = (gt_ref[0, :, 0:d].astype(F32) * _dot(yd_ref[0], wd_ref[0])
              + gt_ref[0, :, d:2 * d].astype(F32) * _dot(ym_ref[0], wm_ref[0])
              + gt_ref[0, :, 2 * d:3 * d].astype(F32) * _dot(yf_ref[0], wf_ref[0]))
    o_ref[0] = x + mod_ref[0, 0, 5:6, :] * _dot(merged.astype(BF16), wo_ref[0])


def _merge_call(x, mods, l, row0, yd, ym, yf, gt, wd, wm, wf, wo, tm=512):
    s, t, d = x.shape
    tm = _tile(t, tm)
    tok = lambda wd_: pl.BlockSpec((1, tm, wd_), lambda i, j: (i, j, 0))
    return pl.pallas_call(
        _merge_body,
        grid=(s, t // tm),
        in_specs=[tok(d), _mods_block(mods, l, row0),
                  tok(yd.shape[2]), tok(ym.shape[2]), tok(yf.shape[2]), tok(gt.shape[2]),
                  _layer(wd, l), _layer(wm, l), _layer(wf, l), _layer(wo, l)],
        out_specs=tok(d),
        out_shape=jax.ShapeDtypeStruct(x.shape, F32),
        compiler_params=_params(2),
        name="merge_out",
    )(x, mods, yd, ym, yf, gt, wd, wm, wf, wo)


def _rope_tables(n_lat, tm_ctx):
    rows = np.arange(n_lat) // GRID_W
    cols = np.arange(n_lat) % GRID_W

    def cos_sin(dim):
        nf = dim // 4
        freqs = np.power(np.float32(ROPE_BASE), -np.arange(nf, dtype=np.float32) / np.float32(nf)).astype(np.float32)
        ar = rows.astype(np.float32)[:, None] * freqs[None, :]
        ac = cols.astype(np.float32)[:, None] * freqs[None, :]
        ang = np.concatenate([ar, ar, ac, ac], axis=-1).astype(np.float64)
        sign = np.concatenate([-np.ones(nf), np.ones(nf), -np.ones(nf), np.ones(nf)])
        return np.cos(ang), np.sin(ang) * sign[None, :]

    cd, sd = cos_sin(DIFF_HD)
    cos_d = np.tile(cd, (1, LANES // DIFF_HD))
    sin_d = np.tile(sd, (1, LANES // DIFF_HD))
    cm, sm = cos_sin(MLA_ROPE)
    cos_m = np.ones((n_lat, LANES))
    sin_m = np.zeros((n_lat, LANES))
    cos_m[:, :MLA_ROPE] = cm
    sin_m[:, :MLA_ROPE] = sm
    lat = np.stack([cos_d, sin_d, cos_m, sin_m]).astype(np.float32)
    ident = np.stack([np.ones((tm_ctx, LANES)), np.zeros((tm_ctx, LANES))] * 2).astype(np.float32)
    return jnp.asarray(lat), jnp.asarray(ident)


def _angles(rows, cols, n):
    return 2.0 * np.pi * ((rows[:, None] * cols[None, :]) % n) / n


def _pos_dft_tables(n, group):
    j, m = np.arange(n // 2), np.arange(n // 2)
    scale = 1.0 / math.sqrt(n * group)
    ae, ao = _angles(j, 2 * m, n), _angles(j, 2 * m + 1, n)
    return jnp.asarray(np.stack([np.cos(ae), -np.sin(ae), np.cos(ao), -np.sin(ao)]) * scale, BF16)


def _chan_dft_table(group):
    a = _angles(np.arange(group), np.arange(group), group)
    return jnp.asarray(np.concatenate([np.cos(a), np.sin(a)], axis=1), BF16)


def _prep_weights(w_in, mla_w_uq, mla_w_ukv):
    depth = w_in.shape[0]
    o_kr = 3 * DIFF_W + MLA_Q_RANK + MLA_KV_RANK
    wa = w_in[:, :, :o_kr + LANES]
    wb = w_in[:, :, o_kr + MLA_ROPE:]

    tail = LANES - MLA_NOPE - MLA_ROPE
    uq = mla_w_uq.reshape(depth, MLA_Q_RANK, MLA_HEADS, MLA_NOPE + MLA_ROPE)
    uq = jnp.concatenate([uq[..., MLA_NOPE:], uq[..., :MLA_NOPE], jnp.zeros(uq.shape[:3] + (tail,), uq.dtype)], axis=-1)
    uq = uq.reshape(depth, MLA_Q_RANK, MLA_QK_W).astype(BF16)
    ukv = mla_w_ukv.reshape(depth, MLA_KV_RANK, MLA_HEADS, MLA_NOPE + MLA_VD)
    kn = jnp.pad(ukv[..., :MLA_NOPE], ((0, 0), (0, 0), (0, 0), (MLA_ROPE, tail)))
    kn = kn.reshape(depth, MLA_KV_RANK, MLA_QK_W)
    vv = ukv[..., MLA_NOPE:].reshape(depth, MLA_KV_RANK, MLA_V_W)
    return wa, wb, uq, jnp.concatenate([kn, vv], axis=2).astype(BF16)


def kernel(x, c, ctx, c_ctx, ada_w, ada_b, norm_g, ffn_w_in, ffn_w_out, w_in, diff_lambda,
           diff_subln_g, mla_q_norm_g, mla_w_uq, mla_kv_norm_g, mla_w_ukv, w_branch_diff,
           w_branch_mla, w_branch_fourier, w_out, final_norm_g):
    b, n_lat, d = x.shape
    n_ctx = ctx.shape[1]
    depth = ada_w.shape[0]

    cond_rows = -(-(b + 1) // 8) * 8
    cond = jnp.concatenate([c, c_ctx[None, :], jnp.zeros((cond_rows - b - 1, d), F32)], axis=0)
    mods = _ada_call(cond, ada_w, ada_b).reshape(depth, cond_rows, N_MOD, d)

    tm_ctx = _tile(b * n_ctx, 512)
    tab_x, tab_c = _rope_tables(n_lat, tm_ctx)
    dft_x = _pos_dft_tables(n_lat, FOURIER_GROUP_DIM)
    dft_c = _pos_dft_tables(n_ctx, FOURIER_GROUP_DIM)
    cs128 = _chan_dft_table(FOURIER_GROUP_DIM)

    wa, wb, uq, ukv = _prep_weights(w_in, mla_w_uq, mla_w_ukv)
    wd, wm, wf, wo = w_branch_diff, w_branch_mla, w_branch_fourier, w_out
    qn, kvn, sub_g = (a[:, None, :] for a in (mla_q_norm_g, mla_kv_norm_g, diff_subln_g))
    final_g = final_norm_g[None, :]

    xs = x
    cs = ctx.reshape(1, b * n_ctx, d)
    per_batch = lambda a: a.reshape(b, a.shape[1] // b, a.shape[-1])
    flat = lambda a: a.reshape(1, b * n_ctx, a.shape[-1])
    for l in range(depth):
        last = l == depth - 1
        lam_init = 0.8 - 0.6 * math.exp(-0.3 * l)
        diff = functools.partial(_diff_body, lam_init=lam_init)
        diff_extra = ([_layer(diff_lambda, l), _layer(sub_g, l)], [diff_lambda, sub_g])

        xs = _ffn_call(xs, mods, l, 0, norm_g, 0, ffn_w_in, ffn_w_out, 0, 0)
        cs = _ffn_call(cs, mods, l, b, norm_g, 0, ffn_w_in, ffn_w_out, 0, 0)

        px = _proj_call(xs, mods, l, 0, norm_g, wa, wb, uq, ukv, qn, kvn, cs128, tab_x, True)
        pc = _proj_call(cs, mods, l, b, norm_g, wa, wb, uq, ukv, qn, kvn, cs128, tab_c, False)
        qd_x, kd_x, vd_x, qm_x, km_x, vm_x, zc_x, zs_x, gt_x = px
        qd_c, kd_c, vd_c, qm_c, km_c, vm_c, zc_c, zs_c, gt_c = (per_batch(a) for a in pc)

        yd_x = _attn_call(diff, "diff_attn", qd_x, [kd_c, kd_x], [vd_c, vd_x], *diff_extra,
                          DIFF_HEADS, MXU_W, DIFF_W)
        ym_x = _attn_call(_mla_body, "mla_attn", qm_x, [km_c, km_x], [vm_c, vm_x], [], [],
                          MLA_HEADS, LANES, MLA_V_W)
        yf_x = _dft_call(dft_x, zc_x, zs_x)
        xs = _merge_call(xs, mods, l, 0, yd_x, ym_x, yf_x, gt_x, wd, wm, wf, wo)
        xs = _ffn_call(xs, mods, l, 0, norm_g, 2, ffn_w_in, ffn_w_out, 1, 6, final_g=final_g if last else None)

        if not last:
            yd_c = _attn_call(diff, "diff_attn_ctx", qd_c, [kd_c], [vd_c], *diff_extra,
                              DIFF_HEADS, MXU_W, DIFF_W)
            ym_c = _attn_call(_mla_body, "mla_attn_ctx", qm_c, [km_c], [vm_c], [], [],
                              MLA_HEADS, LANES, MLA_V_W)
            yf_c = _dft_call(dft_c, zc_c, zs_c)
            cs = _merge_call(cs, mods, l, b, flat(yd_c), flat(ym_c), flat(yf_c), flat(gt_c), wd, wm, wf, wo)
            cs = _ffn_call(cs, mods, l, b, norm_g, 2, ffn_w_in, ffn_w_out, 1, 6)
    return xs
```

```python
import functools
import math

import numpy as np
import jax
import jax.numpy as jnp
from jax import lax
from jax.experimental import pallas as pl
from jax.experimental.pallas import tpu as pltpu

F32 = jnp.float32
BF16 = jnp.bfloat16

GRID_W = 64
DIFF_HEADS = 4
DIFF_HD = 64
DIFF_VD = 2 * DIFF_HD
MLA_HEADS = 8
MLA_NOPE = 64
MLA_ROPE = 32
MLA_VD = 64
MLA_Q_RANK = 384
MLA_KV_RANK = 256
FOURIER_GROUPS = 4
FOURIER_GROUP_DIM = 128
N_BRANCHES = 3
ROPE_BASE = 10000.0
RMS_EPS = 1e-6
N_MOD = 9

DIFF_W = DIFF_HEADS * 2 * DIFF_HD
MLA_QK_W = MLA_HEADS * 128
MLA_V_W = MLA_HEADS * MLA_VD
FOURIER_W = FOURIER_GROUPS * FOURIER_GROUP_DIM

LANES = 128
MXU_W = 256
FFN_CHUNK = MXU_W
VMEM_LIMIT = 56 * 1024 * 1024
FFN_VMEM_LIMIT = 60 * 1024 * 1024
LOG2E = math.log2(math.e)
PROJ_TILE = 512
PROJ_KR_COL = 3 * DIFF_W + MLA_Q_RANK + MLA_KV_RANK
PROJ_WA_COLS = PROJ_KR_COL + LANES


def _tile(n, pref):
    t = min(n, pref)
    while n % t:
        t //= 2
    return t


def _fixed(block, idx):
    return pl.BlockSpec(block, lambda *_: idx, pipeline_mode=pl.Buffered(1))


def _whole(a):
    return _fixed(a.shape, (0,) * a.ndim)


def _layer(a, l):
    return _fixed((1,) + a.shape[1:], (l,) + (0,) * (a.ndim - 1))


def _params(n_axes, vmem_limit=VMEM_LIMIT):
    return pltpu.CompilerParams(dimension_semantics=("arbitrary",) * n_axes,
                                vmem_limit_bytes=vmem_limit)


def _norm_mod(x, g, shift, scale):
    y = x * lax.rsqrt(jnp.mean(x * x, axis=-1, keepdims=True) + RMS_EPS)
    return (y * (g * (1.0 + scale)) + shift).astype(BF16)


def _rms(x, g):
    return x * lax.rsqrt(jnp.mean(x * x, axis=-1, keepdims=True) + RMS_EPS) * g


def _dot(a, b):
    return jnp.dot(a, b, preferred_element_type=F32)


def _ada_body(s_ref, w_ref, b_ref, o_ref):
    s = s_ref[...]
    a = (s * jax.nn.sigmoid(s)).astype(BF16)
    o_ref[0] = _dot(a, w_ref[0].astype(BF16)) + b_ref[0]


def _ada_call(cond, ada_w, ada_b):
    depth, d, n = ada_w.shape
    rows = cond.shape[0]
    tn = _tile(n, 1024)
    return pl.pallas_call(
        _ada_body,
        grid=(depth, n // tn),
        in_specs=[pl.BlockSpec((rows, d), lambda l, j: (0, 0)),
                  pl.BlockSpec((1, d, tn), lambda l, j: (l, 0, j)),
                  pl.BlockSpec((1, 1, tn), lambda l, j: (l, 0, j))],
        out_specs=pl.BlockSpec((1, rows, tn), lambda l, j: (l, 0, j)),
        out_shape=jax.ShapeDtypeStruct((depth, rows, n), F32),
        compiler_params=_params(2),
        name="ada_mod",
    )(cond, ada_w, ada_b.reshape(depth, 1, n))


def _mods_block(mods, l):
    d = mods.shape[-1]
    return pl.BlockSpec((1, 1, N_MOD, d), lambda i, j: (l, i, 0, 0))


def _lat_or_ctx_specs(rows, width, n_lat_seg, n_tiles):
    lat = pl.BlockSpec((1, rows, width), lambda i, j: (jnp.minimum(i, n_lat_seg - 1),
                                                       jnp.where(i < n_lat_seg, j, n_tiles - 1), 0))
    ctx = pl.BlockSpec((1, rows, width), lambda i, j: (0, jnp.where(i < n_lat_seg, 0, j), 0))
    return lat, ctx


def _pick(n_lat_seg, lat_ref, ctx_ref):
    return jnp.where(pl.program_id(0) < n_lat_seg, lat_ref[0], ctx_ref[0])


def _ffn_body(x_ref, *rest, mod0, gi, final, n_lat_seg):
    if n_lat_seg is not None:
        x = _pick(n_lat_seg, x_ref, rest[0])
        rest = rest[1:]
    else:
        x = x_ref[0]
    mod_ref, g_ref, win_ref, wout_ref = rest[:4]
    if final:
        fg_ref, o_ref = rest[4:]
    else:
        (o_ref,) = rest[4:]
    d_ff = wout_ref.shape[2]
    xm = _norm_mod(x, g_ref[0, gi:gi + 1, :], mod_ref[0, 0, mod0:mod0 + 1, :], mod_ref[0, 0, mod0 + 1:mod0 + 2, :])
    acc = jnp.zeros(x.shape, F32)
    for c in range(d_ff // FFN_CHUNK):
        lo = c * FFN_CHUNK
        gate = _dot(xm, win_ref[0, 0, :, lo:lo + FFN_CHUNK])
        up = _dot(xm, win_ref[0, 0, :, d_ff + lo:d_ff + lo + FFN_CHUNK])
        act = (gate * jax.nn.sigmoid(gate) * up).astype(BF16)
        acc = acc + _dot(act, wout_ref[0, 0, lo:lo + FFN_CHUNK, :])
    y = x + (0.5 * mod_ref[0, 0, mod0 + 2:mod0 + 3, :]) * acc
    if final:
        y = _rms(y, fg_ref[...])
    o_ref[0] = y


def _ffn_call(x, mods, l, norm_g, gi, win, wout, which, mod0, ctx=None, final_g=None, tm=1024):
    s, t, d = x.shape
    tm = _tile(t, tm if ctx is None else tm // 2)
    final = final_g is not None
    n_seg = s + 1 if ctx is not None else s
    if ctx is not None:
        in_specs = list(_lat_or_ctx_specs(tm, d, s, t // tm))
        args = [x, ctx]
    else:
        in_specs = [pl.BlockSpec((1, tm, d), lambda i, j: (i, j, 0))]
        args = [x]
    in_specs += [_mods_block(mods, l), _layer(norm_g, l),
                 _fixed((1, 1) + win.shape[2:], (l, which, 0, 0)),
                 _fixed((1, 1) + wout.shape[2:], (l, which, 0, 0))]
    args += [mods, norm_g, win, wout]
    if final:
        in_specs.append(_whole(final_g))
        args.append(final_g)
    return pl.pallas_call(
        functools.partial(_ffn_body, mod0=mod0, gi=gi, final=final, n_lat_seg=s if ctx is not None else None),
        grid=(n_seg, t // tm),
        in_specs=in_specs,
        out_specs=pl.BlockSpec((1, tm, d), lambda i, j: (i, j, 0)),
        out_shape=jax.ShapeDtypeStruct((n_seg, t, d), F32),
        compiler_params=_params(2, FFN_VMEM_LIMIT),
        name="ffn",
    )(*args)


def _rope(x, cos, sin_signed, chunk):
    lane = lax.broadcasted_iota(jnp.int32, x.shape, 1)
    even = (lane % (2 * chunk)) < chunk
    partner = jnp.where(even, pltpu.roll(x, LANES - chunk, 1), pltpu.roll(x, chunk, 1))
    return x * cos + partner * sin_signed


def _proj_body(x_ref, mod_ref, g_ref, wa_ref, wb_ref, wuq_ref, wukv_ref, qn_ref, kvn_ref, cs_ref, tab_ref,
               qd_ref, kd_ref, vd_ref, qm_ref, km_ref, vm_ref, zc_ref, zs_ref, gt_ref, z_scr,
               *, qscale_d, qscale_m):
    x = x_ref[0]
    xm = _norm_mod(x, g_ref[0, 1:2, :], mod_ref[0, 0, 3:4, :], mod_ref[0, 0, 4:5, :])
    cos_d, sin_d, cos_m, sin_m = tab_ref[0], tab_ref[1], tab_ref[2], tab_ref[3]

    def mm(w_ref, c0, width):
        return _dot(xm, w_ref[0, :, c0:c0 + width])

    o_cq = 3 * DIFF_W
    o_ckv = o_cq + MLA_Q_RANK
    o_kr = o_ckv + MLA_KV_RANK
    cq = _rms(mm(wa_ref, o_cq, MLA_Q_RANK), qn_ref[0]).astype(BF16)
    ckv = _rms(mm(wa_ref, o_ckv, MLA_KV_RANK), kvn_ref[0]).astype(BF16)
    kr = mm(wa_ref, o_kr, LANES)
    kr = jnp.where(lax.broadcasted_iota(jnp.int32, kr.shape, 1) < MLA_ROPE, kr, 0.0)
    kr = _rope(kr, cos_m, sin_m, MLA_ROPE // 4)
    qm = _dot(cq, wuq_ref[0])
    for h in range(MLA_HEADS):
        sl = slice(h * LANES, (h + 1) * LANES)
        qm_ref[0, :, sl] = (_rope(qm[:, sl], cos_m, sin_m, MLA_ROPE // 4) * qscale_m).astype(BF16)
    kv = _dot(ckv, wukv_ref[0])
    for h in range(MLA_HEADS):
        sl = slice(h * LANES, (h + 1) * LANES)
        km_ref[0, :, sl] = (kv[:, sl] + kr).astype(BF16)
    vm_ref[0] = kv[:, MLA_QK_W:].astype(BF16)

    q = mm(wa_ref, 0, DIFF_W)
    for h in range(DIFF_HEADS):
        sl = slice(h * LANES, (h + 1) * LANES)
        qd_ref[0, :, sl] = (_rope(q[:, sl], cos_d, sin_d, DIFF_HD // 4) * qscale_d).astype(BF16)
    k = mm(wa_ref, DIFF_W, DIFF_W)
    for h in range(DIFF_HEADS):
        sl = slice(h * LANES, (h + 1) * LANES)
        kd_ref[0, :, sl] = _rope(k[:, sl], cos_d, sin_d, DIFF_HD // 4).astype(BF16)

    d = x.shape[-1]
    for bi in range(N_BRANCHES):
        gt_ref[0, :, bi * d:(bi + 1) * d] = (0.5 * jnp.tanh(0.5 * mm(wb_ref, FOURIER_W + bi * d, d)) + 0.5).astype(BF16)

    f = mm(wb_ref, 0, FOURIER_W).astype(BF16)
    for gi in range(FOURIER_GROUPS):
        sl = slice(gi * LANES, (gi + 1) * LANES)
        z = _dot(f[:, sl], cs_ref[...])
        z_scr[gi] = z[:, :LANES]
        z_scr[FOURIER_GROUPS + gi] = z[:, LANES:]
    half = z_scr.shape[1] // 2
    for parity in range(2):
        for gi in range(FOURIER_GROUPS):
            sl = slice(parity * FOURIER_W + gi * LANES, parity * FOURIER_W + (gi + 1) * LANES)
            zc_ref[0, :, sl] = z_scr[gi, pl.ds(parity, half, stride=2), :].astype(BF16)
            zs_ref[0, :, sl] = z_scr[FOURIER_GROUPS + gi, pl.ds(parity, half, stride=2), :].astype(BF16)
    vd_ref[0] = mm(wa_ref, 2 * DIFF_W, DIFF_W).astype(BF16)


def _proj_call(x, mods, l, norm_g, w_in, wb, wuq, wukv, qn, kvn, cs128, tab, n_lat_seg, tm=PROJ_TILE):
    s, t, d = x.shape
    assert t % tm == 0 and tab.shape[1] == t + tm
    tok = lambda wd, rows=tm: pl.BlockSpec((1, rows, wd), lambda i, j: (i, j, 0))
    tab_map = lambda i, j: (0, jnp.where(i < n_lat_seg, j, t // tm), 0)
    wa_spec = _fixed((1, d, PROJ_WA_COLS), (l, 0, 0))
    full = [DIFF_W, DIFF_W, DIFF_W, MLA_QK_W, MLA_QK_W, MLA_V_W]
    out_specs = ([tok(wd) for wd in full] + [tok(2 * FOURIER_W, tm // 2)] * 2 + [tok(N_BRANCHES * d)])
    out_shape = ([jax.ShapeDtypeStruct((s, t, wd), BF16) for wd in full]
                 + [jax.ShapeDtypeStruct((s, t // 2, 2 * FOURIER_W), BF16)] * 2
                 + [jax.ShapeDtypeStruct((s, t, N_BRANCHES * d), BF16)])
    return pl.pallas_call(
        functools.partial(_proj_body,
                          qscale_d=DIFF_HD ** -0.5 * LOG2E,
                          qscale_m=(MLA_NOPE + MLA_ROPE) ** -0.5 * LOG2E),
        grid=(s, t // tm),
        in_specs=[tok(d), _mods_block(mods, l), _layer(norm_g, l),
                  wa_spec, _layer(wb, l), _layer(wuq, l), _layer(wukv, l), _layer(qn, l), _layer(kvn, l),
                  _whole(cs128), pl.BlockSpec((4, tm, LANES), tab_map)],
        out_specs=out_specs,
        out_shape=out_shape,
        scratch_shapes=[pltpu.VMEM((2 * FOURIER_GROUPS, tm, LANES), F32)],
        compiler_params=_params(2),
        name="in_proj",
    )(x, mods, norm_g, w_in, wb, wuq, wukv, qn, kvn, cs128, tab)


def _fill_kv(srcs_k, srcs_v, k_scr, v_scr, heads, vd):
    vw = v_scr.shape[1] // heads
    r0 = 0
    for k_ref, v_ref in zip(srcs_k, srcs_v):
        n = k_ref.shape[1]
        k_scr[r0:r0 + n, :] = k_ref[0]
        lane = lax.broadcasted_iota(jnp.int32, (n, vw - vd), 1)
        ones_col = jnp.where(lane == 0, 1.0, 0.0).astype(BF16)
        for h in range(heads):
            v_scr[r0:r0 + n, h * vw:h * vw + vd] = v_ref[0, :, h * vd:(h + 1) * vd]
            v_scr[r0:r0 + n, h * vw + vd:(h + 1) * vw] = ones_col
        r0 += n


def _softmax_pv(s, v_aug, vd):
    m = jnp.max(s, axis=-1, keepdims=True)
    e = jnp.exp2(s - m).astype(BF16)
    r = _dot(e, v_aug)
    return r[:, :vd], r[:, vd:vd + 1]


SMALL_UNIT_ROWS = 128


def _head_row_units(heads, rows):
    if rows <= 2 * SMALL_UNIT_ROWS:
        return [(h, 0, rows) for h in range(heads)]
    units = [(0, 0, SMALL_UNIT_ROWS), (0, SMALL_UNIT_ROWS, rows)]
    units += [(h, 0, rows) for h in range(1, heads - 1)]
    units += [(heads - 1, 0, rows - SMALL_UNIT_ROWS), (heads - 1, rows - SMALL_UNIT_ROWS, rows)]
    return units


def _qk(q, k):
    return lax.dot_general(q, k, (((1,), (1,)), ((), ())), preferred_element_type=F32)


def _diff_body(*refs, n_src, lam_init):
    q_ref, lam_ref, sg_ref = refs[0], refs[1], refs[2]
    k_srcs = refs[3:3 + n_src]
    v_srcs = refs[3 + n_src:3 + 2 * n_src]
    o_ref, k_scr, v_scr = refs[3 + 2 * n_src:]

    @pl.when(pl.program_id(1) == 0)
    def _():
        _fill_kv(k_srcs, v_srcs, k_scr, v_scr, DIFF_HEADS, DIFF_VD)

    lp = lam_ref[0]
    lam = (jnp.exp(jnp.sum(lp[0:1] * lp[1:2], axis=-1, keepdims=True))
           - jnp.exp(jnp.sum(lp[2:3] * lp[3:4], axis=-1, keepdims=True)) + lam_init)
    zero = jnp.zeros((), BF16)
    vw = v_scr.shape[1] // DIFF_HEADS
    for h, r0, r1 in _head_row_units(DIFF_HEADS, q_ref.shape[1]):
        sl = slice(h * LANES, (h + 1) * LANES)
        qh, kh, vh = q_ref[0, r0:r1, sl], k_scr[:, sl], v_scr[:, h * vw:(h + 1) * vw]
        first = lax.broadcasted_iota(jnp.int32, qh.shape, 1) < DIFF_HD
        o1, l1 = _softmax_pv(_qk(jnp.where(first, qh, zero), kh), vh, DIFF_VD)
        o2, l2 = _softmax_pv(_qk(jnp.where(first, zero, qh), kh), vh, DIFF_VD)
        o = o1 * (1.0 / l1) - o2 * (lam / l2)
        o_ref[0, r0:r1, sl] = (_rms(o, sg_ref[0]) * (1.0 - lam_init)).astype(BF16)


def _mla_body(*refs, n_src):
    q_ref = refs[0]
    k_srcs = refs[1:1 + n_src]
    v_srcs = refs[1 + n_src:1 + 2 * n_src]
    o_ref, k_scr, v_scr = refs[1 + 2 * n_src:]

    @pl.when(pl.program_id(1) == 0)
    def _():
        _fill_kv(k_srcs, v_srcs, k_scr, v_scr, MLA_HEADS, MLA_VD)

    vw = v_scr.shape[1] // MLA_HEADS
    for h, r0, r1 in _head_row_units(MLA_HEADS, q_ref.shape[1]):
        sl = slice(h * LANES, (h + 1) * LANES)
        o, l = _softmax_pv(_qk(q_ref[0, r0:r1, sl], k_scr[:, sl]), v_scr[:, h * vw:(h + 1) * vw], MLA_VD)
        o_ref[0, r0:r1, h * MLA_VD:(h + 1) * MLA_VD] = (o * (1.0 / l)).astype(BF16)


def _attn_call(body, name, q, k, v, extra_specs, extra, heads, v_aug_w, out_w, b, n_lat, n_ctx, latent, tq=512):
    ctx_rows = lambda a: pl.BlockSpec((1, n_ctx, a.shape[2]), lambda i, j: (b, i, 0))
    lat_rows = lambda a: pl.BlockSpec((1, n_lat, a.shape[2]), lambda i, j: (i, 0, 0))
    if latent:
        lq, tq = n_lat, _tile(n_lat, tq)
        q_spec = pl.BlockSpec((1, tq, q.shape[2]), lambda i, j: (i, j, 0))
        k_specs, v_specs, kv_args, lk = [ctx_rows(k), lat_rows(k)], [ctx_rows(v), lat_rows(v)], [k, k, v, v], n_ctx + n_lat
    else:
        lq, tq = n_ctx, n_ctx
        q_spec = ctx_rows(q)
        k_specs, v_specs, kv_args, lk = [ctx_rows(k)], [ctx_rows(v)], [k, v], n_ctx
    return pl.pallas_call(
        functools.partial(body, n_src=len(k_specs)),
        grid=(b, lq // tq),
        in_specs=[q_spec] + extra_specs + k_specs + v_specs,
        out_specs=pl.BlockSpec((1, tq, out_w), lambda i, j: (i, j, 0)),
        out_shape=jax.ShapeDtypeStruct((b, lq, out_w), BF16),
        scratch_shapes=[pltpu.VMEM((lk, k.shape[2]), BF16),
                        pltpu.VMEM((lk, heads * v_aug_w), BF16)],
        compiler_params=_params(2),
        name=name,
    )(q, *extra, *kv_args)


def _dft_body(t_ref, zc_ref, zs_ref, o_ref):
    w = o_ref.shape[-1]
    even = _dot(t_ref[0], zc_ref[0, :, :w]) + _dot(t_ref[1], zs_ref[0, :, :w])
    odd = _dot(t_ref[2], zc_ref[0, :, w:]) + _dot(t_ref[3], zs_ref[0, :, w:])
    o_ref[0, 0] = (even + odd).astype(BF16)
    o_ref[0, 1] = (even - odd).astype(BF16)


def _dft_call(tables, zc, zs, b, latent, tm=1024):
    half, w = tables.shape[1], zc.shape[2] // 2
    t = 2 * half
    tm = _tile(half, tm)
    z_spec = pl.BlockSpec((1, half, 2 * w), (lambda i, j: (j, 0, 0)) if latent else (lambda i, j: (b, j, 0)))
    out = pl.pallas_call(
        _dft_body,
        grid=(half // tm, b),
        in_specs=[pl.BlockSpec((4, tm, half), lambda i, j: (0, i, 0)), z_spec, z_spec],
        out_specs=pl.BlockSpec((1, 2, tm, w), lambda i, j: (j, 0, i, 0)),
        out_shape=jax.ShapeDtypeStruct((b, 2, half, w), BF16),
        compiler_params=_params(2),
        name="pos_dft",
    )(tables, zc, zs)
    return out.reshape(b, t, w)


def _merge_body(x_ref, mod_ref, gt_ref, *rest, n_lat_seg):
    if n_lat_seg is None:
        (yd_ref, ym_ref, yf_ref), rest = rest[:3], rest[3:]
        yd, ym, yf = yd_ref[0], ym_ref[0], yf_ref[0]
    else:
        ys, rest = rest[:6], rest[6:]
        yd, ym, yf = (_pick(n_lat_seg, ys[2 * i], ys[2 * i + 1]) for i in range(N_BRANCHES))
    wd_ref, wm_ref, wf_ref, wo_ref, o_ref = rest
    x = x_ref[0]
    d = x.shape[-1]
    merged = (gt_ref[0, :, 0:d].astype(F32) * _dot(yd, wd_ref[0])
              + gt_ref[0, :, d:2 * d].astype(F32) * _dot(ym, wm_ref[0])
              + gt_ref[0, :, 2 * d:3 * d].astype(F32) * _dot(yf, wf_ref[0]))
    o_ref[0] = x + mod_ref[0, 0, 5:6, :] * _dot(merged.astype(BF16), wo_ref[0])


def _merge_call(x, mods, l, gt, y_lat, y_ctx, wd, wm, wf, wo, tm=512):
    b, t = y_lat[0].shape[:2]
    d = x.shape[2]
    tm = _tile(t, tm)
    tok = lambda wd_: pl.BlockSpec((1, tm, wd_), lambda i, j: (i, j, 0))
    if y_ctx is None:
        n_seg, y_specs, y_args = b, [tok(a.shape[2]) for a in y_lat], list(y_lat)
    else:
        n_seg, y_specs, y_args = b + 1, [], []
        for yl, yc in zip(y_lat, y_ctx):
            y_specs += list(_lat_or_ctx_specs(tm, yl.shape[2], b, t // tm))
            y_args += [yl, yc]
    return pl.pallas_call(
        functools.partial(_merge_body, n_lat_seg=None if y_ctx is None else b),
        grid=(n_seg, t // tm),
        in_specs=[tok(d), _mods_block(mods, l), tok(gt.shape[2])] + y_specs
                 + [_layer(wd, l), _layer(wm, l), _layer(wf, l), _layer(wo, l)],
        out_specs=tok(d),
        out_shape=jax.ShapeDtypeStruct((n_seg, t, d), F32),
        compiler_params=_params(2),
        name="merge_out",
    )(x, mods, gt, *y_args, wd, wm, wf, wo)


def _rope_tables(n_lat, n_ident):
    rows = np.arange(n_lat) // GRID_W
    cols = np.arange(n_lat) % GRID_W

    def cos_sin(dim):
        nf = dim // 4
        freqs = np.power(np.float32(ROPE_BASE), -np.arange(nf, dtype=np.float32) / np.float32(nf)).astype(np.float32)
        ar = rows.astype(np.float32)[:, None] * freqs[None, :]
        ac = cols.astype(np.float32)[:, None] * freqs[None, :]
        ang = np.concatenate([ar, ar, ac, ac], axis=-1).astype(np.float64)
        sign = np.concatenate([-np.ones(nf), np.ones(nf), -np.ones(nf), np.ones(nf)])
        return np.cos(ang), np.sin(ang) * sign[None, :]

    cd, sd = cos_sin(DIFF_HD)
    cos_d = np.tile(cd, (1, LANES // DIFF_HD))
    sin_d = np.tile(sd, (1, LANES // DIFF_HD))
    cm, sm = cos_sin(MLA_ROPE)
    cos_m = np.ones((n_lat, LANES))
    sin_m = np.zeros((n_lat, LANES))
    cos_m[:, :MLA_ROPE] = cm
    sin_m[:, :MLA_ROPE] = sm
    lat = np.stack([cos_d, sin_d, cos_m, sin_m])
    ident = np.stack([np.ones((n_ident, LANES)), np.zeros((n_ident, LANES))] * 2)
    return jnp.asarray(np.concatenate([lat, ident], axis=1), F32)


def _angles(rows, cols, n):
    return 2.0 * np.pi * ((rows[:, None] * cols[None, :]) % n) / n


def _pos_dft_tables(n, group):
    j, m = np.arange(n // 2), np.arange(n // 2)
    scale = 1.0 / math.sqrt(n * group)
    ae, ao = _angles(j, 2 * m, n), _angles(j, 2 * m + 1, n)
    return jnp.asarray(np.stack([np.cos(ae), -np.sin(ae), np.cos(ao), -np.sin(ao)]) * scale, BF16)


def _chan_dft_table(group):
    a = _angles(np.arange(group), np.arange(group), group)
    return jnp.asarray(np.concatenate([np.cos(a), np.sin(a)], axis=1), BF16)


def _cols_body(w_ref, o_ref, *, start):
    o_ref[0] = w_ref[0, :, start:start + o_ref.shape[2]].astype(BF16)


def _cols_call(w, start, width, tr=128):
    depth, rows, cols = w.shape
    return pl.pallas_call(
        functools.partial(_cols_body, start=start),
        grid=(depth, rows // tr),
        in_specs=[pl.BlockSpec((1, tr, cols), lambda l, i: (l, i, 0))],
        out_specs=pl.BlockSpec((1, tr, width), lambda l, i: (l, i, 0)),
        out_shape=jax.ShapeDtypeStruct((depth, rows, width), BF16),
        compiler_params=_params(2),
        name="weight_cols",
    )(w)


def _prep_mla_weights(mla_w_uq, mla_w_ukv):
    depth = mla_w_uq.shape[0]
    tail = LANES - MLA_NOPE - MLA_ROPE
    uq = mla_w_uq.reshape(depth, MLA_Q_RANK, MLA_HEADS, MLA_NOPE + MLA_ROPE)
    uq = jnp.concatenate([uq[..., MLA_NOPE:], uq[..., :MLA_NOPE], jnp.zeros(uq.shape[:3] + (tail,), uq.dtype)], axis=-1)
    uq = uq.reshape(depth, MLA_Q_RANK, MLA_QK_W).astype(BF16)
    ukv = mla_w_ukv.reshape(depth, MLA_KV_RANK, MLA_HEADS, MLA_NOPE + MLA_VD)
    kn = jnp.pad(ukv[..., :MLA_NOPE], ((0, 0), (0, 0), (0, 0), (MLA_ROPE, tail)))
    kn = kn.reshape(depth, MLA_KV_RANK, MLA_QK_W)
    vv = ukv[..., MLA_NOPE:].reshape(depth, MLA_KV_RANK, MLA_V_W)
    return uq, jnp.concatenate([kn, vv], axis=2).astype(BF16)


def kernel(x, c, ctx, c_ctx, ada_w, ada_b, norm_g, ffn_w_in, ffn_w_out, w_in, diff_lambda,
           diff_subln_g, mla_q_norm_g, mla_w_uq, mla_kv_norm_g, mla_w_ukv, w_branch_diff,
           w_branch_mla, w_branch_fourier, w_out, final_norm_g):
    b, n_lat, d = x.shape
    n_ctx = ctx.shape[1]
    depth = ada_w.shape[0]
    assert b * n_ctx == n_lat, "context tokens of all batches must fill exactly one latent-length segment"

    cond_rows = -(-(b + 1) // 8) * 8
    cond = jnp.concatenate([c, c_ctx[None, :], jnp.zeros((cond_rows - b - 1, d), F32)], axis=0)
    mods = _ada_call(cond, ada_w, ada_b).reshape(depth, cond_rows, N_MOD, d)

    tab = _rope_tables(n_lat, PROJ_TILE)
    dft_x = _pos_dft_tables(n_lat, FOURIER_GROUP_DIM)
    dft_c = _pos_dft_tables(n_ctx, FOURIER_GROUP_DIM)
    cs128 = _chan_dft_table(FOURIER_GROUP_DIM)

    wb = _cols_call(w_in, PROJ_KR_COL + MLA_ROPE, w_in.shape[2] - PROJ_KR_COL - MLA_ROPE)
    uq, ukv = _prep_mla_weights(mla_w_uq, mla_w_ukv)
    wd, wm, wf, wo = w_branch_diff, w_branch_mla, w_branch_fourier, w_out
    qn, kvn, sub_g = (a[:, None, :] for a in (mla_q_norm_g, mla_kv_norm_g, diff_subln_g))
    final_g = final_norm_g[None, :]
    flat = lambda a: a.reshape(1, b * n_ctx, a.shape[-1])

    h, h_ctx = x, ctx.reshape(1, b * n_ctx, d)
    for l in range(depth):
        last = l == depth - 1
        lam_init = 0.8 - 0.6 * math.exp(-0.3 * l)
        diff = functools.partial(_diff_body, lam_init=lam_init)
        diff_extra = ([_layer(diff_lambda, l), _layer(sub_g, l)], [diff_lambda, sub_g])
        sizes = (b, n_lat, n_ctx)

        h = _ffn_call(h, mods, l, norm_g, 0, ffn_w_in, ffn_w_out, 0, 0, ctx=h_ctx)
        h_ctx = None
        qd, kd, vd, qm, km, vm, zc, zs, gt = _proj_call(h, mods, l, norm_g, w_in, wb, uq, ukv, qn, kvn, cs128, tab, b)

        y_lat = (_attn_call(diff, "diff_attn", qd, kd, vd, *diff_extra, DIFF_HEADS, MXU_W, DIFF_W, *sizes, True),
                 _attn_call(_mla_body, "mla_attn", qm, km, vm, [], [], MLA_HEADS, LANES, MLA_V_W, *sizes, True),
                 _dft_call(dft_x, zc, zs, b, True))
        y_ctx = None
        if not last:
            y_ctx = (_attn_call(diff, "diff_attn_ctx", qd, kd, vd, *diff_extra, DIFF_HEADS, MXU_W, DIFF_W, *sizes, False),
                     _attn_call(_mla_body, "mla_attn_ctx", qm, km, vm, [], [], MLA_HEADS, LANES, MLA_V_W, *sizes, False),
                     _dft_call(dft_c, zc, zs, b, False))
            y_ctx = tuple(flat(a) for a in y_ctx)
        h = _merge_call(h, mods, l, gt, y_lat, y_ctx, wd, wm, wf, wo)
        h = _ffn_call(h, mods, l, norm_g, 2, ffn_w_in, ffn_w_out, 1, 6, final_g=final_g if last else None)
    return h
```

```python
import functools
import math

import numpy as np
import jax
import jax.numpy as jnp
from jax import lax
from jax.experimental import pallas as pl
from jax.experimental.pallas import tpu as pltpu

F32 = jnp.float32
BF16 = jnp.bfloat16

GRID_W = 64
DIFF_HEADS = 4
DIFF_HD = 64
DIFF_VD = 2 * DIFF_HD
MLA_HEADS = 8
MLA_NOPE = 64
MLA_ROPE = 32
MLA_VD = 64
MLA_Q_RANK = 384
MLA_KV_RANK = 256
FOURIER_GROUPS = 4
FOURIER_GROUP_DIM = 128
N_BRANCHES = 3
ROPE_BASE = 10000.0
RMS_EPS = 1e-6
N_MOD = 9

DIFF_W = DIFF_HEADS * 2 * DIFF_HD
MLA_QK_W = MLA_HEADS * 128
MLA_V_W = MLA_HEADS * MLA_VD
FOURIER_W = FOURIER_GROUPS * FOURIER_GROUP_DIM

LANES = 128
MXU_W = 256
FFN_CHUNK = MXU_W
VMEM_LIMIT = 56 * 1024 * 1024
FFN_VMEM_LIMIT = 60 * 1024 * 1024
LOG2E = math.log2(math.e)
PROJ_TILE = 512
PROJ_KR_COL = 3 * DIFF_W + MLA_Q_RANK + MLA_KV_RANK
PROJ_WA_COLS = PROJ_KR_COL + LANES


def _tile(n, pref):
    t = min(n, pref)
    while n % t:
        t //= 2
    return t


def _fixed(block, idx):
    return pl.BlockSpec(block, lambda *_: idx, pipeline_mode=pl.Buffered(1))


def _whole(a):
    return _fixed(a.shape, (0,) * a.ndim)


def _layer(a, l):
    return _fixed((1,) + a.shape[1:], (l,) + (0,) * (a.ndim - 1))


def _params(n_axes, vmem_limit=VMEM_LIMIT):
    return pltpu.CompilerParams(dimension_semantics=("arbitrary",) * n_axes,
                                vmem_limit_bytes=vmem_limit)


def _norm_mod(x, g, shift, scale):
    y = x * lax.rsqrt(jnp.mean(x * x, axis=-1, keepdims=True) + RMS_EPS)
    return (y * (g * (1.0 + scale)) + shift).astype(BF16)


def _rms(x, g):
    return x * lax.rsqrt(jnp.mean(x * x, axis=-1, keepdims=True) + RMS_EPS) * g


def _dot(a, b):
    return jnp.dot(a, b, preferred_element_type=F32)


def _ada_body(s_ref, w_ref, b_ref, o_ref):
    s = s_ref[...]
    a = (s * jax.nn.sigmoid(s)).astype(BF16)
    o_ref[0] = _dot(a, w_ref[0].astype(BF16)) + b_ref[0]


def _ada_call(cond, ada_w, ada_b):
    depth, d, n = ada_w.shape
    rows = cond.shape[0]
    tn = _tile(n, 1024)
    return pl.pallas_call(
        _ada_body,
        grid=(depth, n // tn),
        in_specs=[pl.BlockSpec((rows, d), lambda l, j: (0, 0)),
                  pl.BlockSpec((1, d, tn), lambda l, j: (l, 0, j)),
                  pl.BlockSpec((1, 1, tn), lambda l, j: (l, 0, j))],
        out_specs=pl.BlockSpec((1, rows, tn), lambda l, j: (l, 0, j)),
        out_shape=jax.ShapeDtypeStruct((depth, rows, n), F32),
        compiler_params=_params(2),
        name="ada_mod",
    )(cond, ada_w, ada_b.reshape(depth, 1, n))


def _mods_block(mods, l):
    d = mods.shape[-1]
    return pl.BlockSpec((1, 1, N_MOD, d), lambda i, j: (l, i, 0, 0))


def _lat_or_ctx_specs(rows, width, n_lat_seg, n_tiles):
    lat = pl.BlockSpec((1, rows, width), lambda i, j: (jnp.minimum(i, n_lat_seg - 1),
                                                       jnp.where(i < n_lat_seg, j, n_tiles - 1), 0))
    ctx = pl.BlockSpec((1, rows, width), lambda i, j: (0, jnp.where(i < n_lat_seg, 0, j), 0))
    return lat, ctx


def _pick(n_lat_seg, lat_ref, ctx_ref):
    return jnp.where(pl.program_id(0) < n_lat_seg, lat_ref[0], ctx_ref[0])


def _ffn_body(x_ref, *rest, mod0, gi, final, n_lat_seg):
    if n_lat_seg is not None:
        x = _pick(n_lat_seg, x_ref, rest[0])
        rest = rest[1:]
    else:
        x = x_ref[0]
    mod_ref, g_ref, win_ref, wout_ref = rest[:4]
    if final:
        fg_ref, o_ref = rest[4:]
    else:
        (o_ref,) = rest[4:]
    d_ff = wout_ref.shape[2]
    xm = _norm_mod(x, g_ref[0, gi:gi + 1, :], mod_ref[0, 0, mod0:mod0 + 1, :], mod_ref[0, 0, mod0 + 1:mod0 + 2, :])
    acc = jnp.zeros(x.shape, F32)
    for c in range(d_ff // FFN_CHUNK):
        lo = c * FFN_CHUNK
        gate = _dot(xm, win_ref[0, 0, :, lo:lo + FFN_CHUNK])
        up = _dot(xm, win_ref[0, 0, :, d_ff + lo:d_ff + lo + FFN_CHUNK])
        act = (gate * jax.nn.sigmoid(gate) * up).astype(BF16)
        acc = acc + _dot(act, wout_ref[0, 0, lo:lo + FFN_CHUNK, :])
    y = x + (0.5 * mod_ref[0, 0, mod0 + 2:mod0 + 3, :]) * acc
    if final:
        y = _rms(y, fg_ref[...])
    o_ref[0] = y


def _ffn_call(x, mods, l, norm_g, gi, win, wout, which, mod0, ctx=None, final_g=None, tm=1024):
    s, t, d = x.shape
    tm = _tile(t, tm if ctx is None else tm // 2)
    final = final_g is not None
    n_seg = s + 1 if ctx is not None else s
    if ctx is not None:
        in_specs = list(_lat_or_ctx_specs(tm, d, s, t // tm))
        args = [x, ctx]
    else:
        in_specs = [pl.BlockSpec((1, tm, d), lambda i, j: (i, j, 0))]
        args = [x]
    in_specs += [_mods_block(mods, l), _layer(norm_g, l),
                 _fixed((1, 1) + win.shape[2:], (l, which, 0, 0)),
                 _fixed((1, 1) + wout.shape[2:], (l, which, 0, 0))]
    args += [mods, norm_g, win, wout]
    if final:
        in_specs.append(_whole(final_g))
        args.append(final_g)
    return pl.pallas_call(
        functools.partial(_ffn_body, mod0=mod0, gi=gi, final=final, n_lat_seg=s if ctx is not None else None),
        grid=(n_seg, t // tm),
        in_specs=in_specs,
        out_specs=pl.BlockSpec((1, tm, d), lambda i, j: (i, j, 0)),
        out_shape=jax.ShapeDtypeStruct((n_seg, t, d), F32),
        compiler_params=_params(2, FFN_VMEM_LIMIT),
        name="ffn",
    )(*args)


def _rope(x, cos, sin_signed, chunk):
    lane = lax.broadcasted_iota(jnp.int32, x.shape, 1)
    even = (lane % (2 * chunk)) < chunk
    partner = jnp.where(even, pltpu.roll(x, LANES - chunk, 1), pltpu.roll(x, chunk, 1))
    return x * cos + partner * sin_signed


def _proj_body(x_ref, mod_ref, g_ref, w_ref, wuq_ref, wukv_ref, qn_ref, kvn_ref, cs_ref, tab_ref,
               qd_ref, kd_ref, vd_ref, qm_ref, km_ref, vm_ref, zc_ref, zs_ref, gt_ref, z_scr,
               *, qscale_d, qscale_m):
    x = x_ref[0]
    xm = _norm_mod(x, g_ref[0, 1:2, :], mod_ref[0, 0, 3:4, :], mod_ref[0, 0, 4:5, :])
    cos_d, sin_d, cos_m, sin_m = tab_ref[0], tab_ref[1], tab_ref[2], tab_ref[3]

    def mm(part, c0, width):
        c0 += PROJ_WA_COLS * part
        return _dot(xm, w_ref[0, :, c0:c0 + width])

    HEAD, TAIL = 0, 1

    o_cq = 3 * DIFF_W
    o_ckv = o_cq + MLA_Q_RANK
    o_kr = o_ckv + MLA_KV_RANK
    cq = _rms(mm(HEAD,o_cq, MLA_Q_RANK), qn_ref[0]).astype(BF16)
    ckv = _rms(mm(HEAD,o_ckv, MLA_KV_RANK), kvn_ref[0]).astype(BF16)
    kr = mm(HEAD,o_kr, LANES)
    kr = jnp.where(lax.broadcasted_iota(jnp.int32, kr.shape, 1) < MLA_ROPE, kr, 0.0)
    kr = _rope(kr, cos_m, sin_m, MLA_ROPE // 4)
    qm = _dot(cq, wuq_ref[0])
    for h in range(MLA_HEADS):
        sl = slice(h * LANES, (h + 1) * LANES)
        qm_ref[0, :, sl] = (_rope(qm[:, sl], cos_m, sin_m, MLA_ROPE // 4) * qscale_m).astype(BF16)
    kv = _dot(ckv, wukv_ref[0])
    for h in range(MLA_HEADS):
        sl = slice(h * LANES, (h + 1) * LANES)
        km_ref[0, :, sl] = (kv[:, sl] + kr).astype(BF16)
    vm_ref[0] = kv[:, MLA_QK_W:].astype(BF16)

    q = mm(HEAD,0, DIFF_W)
    for h in range(DIFF_HEADS):
        sl = slice(h * LANES, (h + 1) * LANES)
        qd_ref[0, :, sl] = (_rope(q[:, sl], cos_d, sin_d, DIFF_HD // 4) * qscale_d).astype(BF16)
    k = mm(HEAD,DIFF_W, DIFF_W)
    for h in range(DIFF_HEADS):
        sl = slice(h * LANES, (h + 1) * LANES)
        kd_ref[0, :, sl] = _rope(k[:, sl], cos_d, sin_d, DIFF_HD // 4).astype(BF16)

    d = x.shape[-1]
    for bi in range(N_BRANCHES):
        gt_ref[0, :, bi * d:(bi + 1) * d] = (0.5 * jnp.tanh(0.5 * mm(TAIL,FOURIER_W + bi * d, d)) + 0.5).astype(BF16)

    f = mm(TAIL,0, FOURIER_W).astype(BF16)
    for gi in range(FOURIER_GROUPS):
        sl = slice(gi * LANES, (gi + 1) * LANES)
        z = _dot(f[:, sl], cs_ref[...])
        z_scr[gi] = z[:, :LANES]
        z_scr[FOURIER_GROUPS + gi] = z[:, LANES:]
    half = z_scr.shape[1] // 2
    for parity in range(2):
        for gi in range(FOURIER_GROUPS):
            sl = slice(parity * FOURIER_W + gi * LANES, parity * FOURIER_W + (gi + 1) * LANES)
            zc_ref[0, :, sl] = z_scr[gi, pl.ds(parity, half, stride=2), :].astype(BF16)
            zs_ref[0, :, sl] = z_scr[FOURIER_GROUPS + gi, pl.ds(parity, half, stride=2), :].astype(BF16)
    vd_ref[0] = mm(HEAD,2 * DIFF_W, DIFF_W).astype(BF16)


def _proj_call(x, mods, l, norm_g, w, wuq, wukv, qn, kvn, cs128, tab, n_lat_seg, tm=PROJ_TILE):
    s, t, d = x.shape
    assert t % tm == 0 and tab.shape[1] == t + tm
    tok = lambda wd, rows=tm: pl.BlockSpec((1, rows, wd), lambda i, j: (i, j, 0))
    tab_map = lambda i, j: (0, jnp.where(i < n_lat_seg, j, t // tm), 0)
    full = [DIFF_W, DIFF_W, DIFF_W, MLA_QK_W, MLA_QK_W, MLA_V_W]
    out_specs = ([tok(wd) for wd in full] + [tok(2 * FOURIER_W, tm // 2)] * 2 + [tok(N_BRANCHES * d)])
    out_shape = ([jax.ShapeDtypeStruct((s, t, wd), BF16) for wd in full]
                 + [jax.ShapeDtypeStruct((s, t // 2, 2 * FOURIER_W), BF16)] * 2
                 + [jax.ShapeDtypeStruct((s, t, N_BRANCHES * d), BF16)])
    return pl.pallas_call(
        functools.partial(_proj_body,
                          qscale_d=DIFF_HD ** -0.5 * LOG2E,
                          qscale_m=(MLA_NOPE + MLA_ROPE) ** -0.5 * LOG2E),
        grid=(s, t // tm),
        in_specs=[tok(d), _mods_block(mods, l), _layer(norm_g, l),
                  _layer(w, l), _layer(wuq, l), _layer(wukv, l), _layer(qn, l), _layer(kvn, l),
                  _whole(cs128), pl.BlockSpec((4, tm, LANES), tab_map)],
        out_specs=out_specs,
        out_shape=out_shape,
        scratch_shapes=[pltpu.VMEM((2 * FOURIER_GROUPS, tm, LANES), F32)],
        compiler_params=_params(2),
        name="in_proj",
    )(x, mods, norm_g, w, wuq, wukv, qn, kvn, cs128, tab)


def _fill_kv(srcs_k, srcs_v, k_scr, v_scr, heads, vd):
    vw = v_scr.shape[1] // heads
    r0 = 0
    for k_ref, v_ref in zip(srcs_k, srcs_v):
        n = k_ref.shape[1]
        k_scr[r0:r0 + n, :] = k_ref[0]
        lane = lax.broadcasted_iota(jnp.int32, (n, vw - vd), 1)
        ones_col = jnp.where(lane == 0, 1.0, 0.0).astype(BF16)
        for h in range(heads):
            v_scr[r0:r0 + n, h * vw:h * vw + vd] = v_ref[0, :, h * vd:(h + 1) * vd]
            v_scr[r0:r0 + n, h * vw + vd:(h + 1) * vw] = ones_col
        r0 += n


def _softmax_pv(s, v_aug, vd):
    m = jnp.max(s, axis=-1, keepdims=True)
    e = jnp.exp2(s - m).astype(BF16)
    r = _dot(e, v_aug)
    return r[:, :vd], r[:, vd:vd + 1]


SMALL_UNIT_ROWS = 128


def _head_row_units(heads, rows):
    if rows <= 2 * SMALL_UNIT_ROWS:
        return [(h, 0, rows) for h in range(heads)]
    units = [(0, 0, SMALL_UNIT_ROWS), (0, SMALL_UNIT_ROWS, rows)]
    units += [(h, 0, rows) for h in range(1, heads - 1)]
    units += [(heads - 1, 0, rows - SMALL_UNIT_ROWS), (heads - 1, rows - SMALL_UNIT_ROWS, rows)]
    return units


def _qk(q, k):
    return lax.dot_general(q, k, (((1,), (1,)), ((), ())), preferred_element_type=F32)


def _diff_body(*refs, n_src, lam_init):
    q_ref, lam_ref, sg_ref = refs[0], refs[1], refs[2]
    k_srcs = refs[3:3 + n_src]
    v_srcs = refs[3 + n_src:3 + 2 * n_src]
    o_ref, k_scr, v_scr = refs[3 + 2 * n_src:]

    @pl.when(pl.program_id(1) == 0)
    def _():
        _fill_kv(k_srcs, v_srcs, k_scr, v_scr, DIFF_HEADS, DIFF_VD)

    lp = lam_ref[0]
    lam = (jnp.exp(jnp.sum(lp[0:1] * lp[1:2], axis=-1, keepdims=True))
           - jnp.exp(jnp.sum(lp[2:3] * lp[3:4], axis=-1, keepdims=True)) + lam_init)
    zero = jnp.zeros((), BF16)
    vw = v_scr.shape[1] // DIFF_HEADS
    for h, r0, r1 in _head_row_units(DIFF_HEADS, q_ref.shape[1]):
        sl = slice(h * LANES, (h + 1) * LANES)
        qh, kh, vh = q_ref[0, r0:r1, sl], k_scr[:, sl], v_scr[:, h * vw:(h + 1) * vw]
        first = lax.broadcasted_iota(jnp.int32, qh.shape, 1) < DIFF_HD
        o1, l1 = _softmax_pv(_qk(jnp.where(first, qh, zero), kh), vh, DIFF_VD)
        o2, l2 = _softmax_pv(_qk(jnp.where(first, zero, qh), kh), vh, DIFF_VD)
        o = o1 * (1.0 / l1) - o2 * (lam / l2)
        o_ref[0, r0:r1, sl] = (_rms(o, sg_ref[0]) * (1.0 - lam_init)).astype(BF16)


def _mla_body(*refs, n_src):
    q_ref = refs[0]
    k_srcs = refs[1:1 + n_src]
    v_srcs = refs[1 + n_src:1 + 2 * n_src]
    o_ref, k_scr, v_scr = refs[1 + 2 * n_src:]

    @pl.when(pl.program_id(1) == 0)
    def _():
        _fill_kv(k_srcs, v_srcs, k_scr, v_scr, MLA_HEADS, MLA_VD)

    vw = v_scr.shape[1] // MLA_HEADS
    for h, r0, r1 in _head_row_units(MLA_HEADS, q_ref.shape[1]):
        sl = slice(h * LANES, (h + 1) * LANES)
        o, l = _softmax_pv(_qk(q_ref[0, r0:r1, sl], k_scr[:, sl]), v_scr[:, h * vw:(h + 1) * vw], MLA_VD)
        o_ref[0, r0:r1, h * MLA_VD:(h + 1) * MLA_VD] = (o * (1.0 / l)).astype(BF16)


def _ctx_rows_spec(a, b, n_ctx):
    return pl.BlockSpec((1, n_ctx, a.shape[2]), lambda i, j: (b, i, 0))


def _attn_call(body, name, q, k, v, extra_specs, extra, heads, v_aug_w, out_w, b, n_lat, n_ctx, tq=512):
    lat_rows = lambda a: pl.BlockSpec((1, n_lat, a.shape[2]), lambda i, j: (i, 0, 0))
    tq = _tile(n_lat, tq)
    return pl.pallas_call(
        functools.partial(body, n_src=2),
        grid=(b, n_lat // tq),
        in_specs=([pl.BlockSpec((1, tq, q.shape[2]), lambda i, j: (i, j, 0))] + extra_specs
                  + [_ctx_rows_spec(k, b, n_ctx), lat_rows(k), _ctx_rows_spec(v, b, n_ctx), lat_rows(v)]),
        out_specs=pl.BlockSpec((1, tq, out_w), lambda i, j: (i, j, 0)),
        out_shape=jax.ShapeDtypeStruct((b, n_lat, out_w), BF16),
        scratch_shapes=[pltpu.VMEM((n_ctx + n_lat, k.shape[2]), BF16),
                        pltpu.VMEM((n_ctx + n_lat, heads * v_aug_w), BF16)],
        compiler_params=_params(2),
        name=name,
    )(q, *extra, k, k, v, v)


def _ctx_attn_body(qd_ref, lam_ref, sg_ref, kd_ref, vd_ref, qm_ref, km_ref, vm_ref, od_ref, om_ref,
                   kd_scr, vd_scr, km_scr, vm_scr, *, lam_init):
    _diff_body(qd_ref, lam_ref, sg_ref, kd_ref, vd_ref, od_ref, kd_scr, vd_scr, n_src=1, lam_init=lam_init)
    _mla_body(qm_ref, km_ref, vm_ref, om_ref, km_scr, vm_scr, n_src=1)


def _ctx_attn_call(lam_init, diff_extra_specs, diff_extra, qd, kd, vd, qm, km, vm, b, n_ctx):
    rows = lambda a: _ctx_rows_spec(a, b, n_ctx)
    out = lambda w: pl.BlockSpec((1, n_ctx, w), lambda i, j: (i, 0, 0))
    return pl.pallas_call(
        functools.partial(_ctx_attn_body, lam_init=lam_init),
        grid=(b, 1),
        in_specs=[rows(qd)] + diff_extra_specs + [rows(kd), rows(vd), rows(qm), rows(km), rows(vm)],
        out_specs=[out(DIFF_W), out(MLA_V_W)],
        out_shape=[jax.ShapeDtypeStruct((b, n_ctx, DIFF_W), BF16),
                   jax.ShapeDtypeStruct((b, n_ctx, MLA_V_W), BF16)],
        scratch_shapes=[pltpu.VMEM((n_ctx, kd.shape[2]), BF16), pltpu.VMEM((n_ctx, DIFF_HEADS * MXU_W), BF16),
                        pltpu.VMEM((n_ctx, km.shape[2]), BF16), pltpu.VMEM((n_ctx, MLA_HEADS * LANES), BF16)],
        compiler_params=_params(2),
        name="ctx_attn",
    )(qd, *diff_extra, kd, vd, qm, km, vm)


def _dft_body(t_ref, zc_ref, zs_ref, o_ref):
    w = o_ref.shape[-1]
    even = _dot(t_ref[0], zc_ref[0, :, :w]) + _dot(t_ref[1], zs_ref[0, :, :w])
    odd = _dot(t_ref[2], zc_ref[0, :, w:]) + _dot(t_ref[3], zs_ref[0, :, w:])
    o_ref[0, 0] = (even + odd).astype(BF16)
    o_ref[0, 1] = (even - odd).astype(BF16)


def _dft_call(tables, zc, zs, b, latent, tm=1024):
    half, w = tables.shape[1], zc.shape[2] // 2
    t = 2 * half
    tm = _tile(half, tm)
    z_spec = pl.BlockSpec((1, half, 2 * w), (lambda i, j: (j, 0, 0)) if latent else (lambda i, j: (b, j, 0)))
    out = pl.pallas_call(
        _dft_body,
        grid=(half // tm, b),
        in_specs=[pl.BlockSpec((4, tm, half), lambda i, j: (0, i, 0)), z_spec, z_spec],
        out_specs=pl.BlockSpec((1, 2, tm, w), lambda i, j: (j, 0, i, 0)),
        out_shape=jax.ShapeDtypeStruct((b, 2, half, w), BF16),
        compiler_params=_params(2),
        name="pos_dft",
    )(tables, zc, zs)
    return out.reshape(b, t, w)


def _merge_body(x_ref, mod_ref, gt_ref, *rest, n_lat_seg):
    if n_lat_seg is None:
        (yd_ref, ym_ref, yf_ref), rest = rest[:3], rest[3:]
        yd, ym, yf = yd_ref[0], ym_ref[0], yf_ref[0]
    else:
        ys, rest = rest[:6], rest[6:]
        yd, ym, yf = (_pick(n_lat_seg, ys[2 * i], ys[2 * i + 1]) for i in range(N_BRANCHES))
    wd_ref, wm_ref, wf_ref, wo_ref, o_ref = rest
    x = x_ref[0]
    d = x.shape[-1]
    merged = (gt_ref[0, :, 0:d].astype(F32) * _dot(yd, wd_ref[0])
              + gt_ref[0, :, d:2 * d].astype(F32) * _dot(ym, wm_ref[0])
              + gt_ref[0, :, 2 * d:3 * d].astype(F32) * _dot(yf, wf_ref[0]))
    o_ref[0] = x + mod_ref[0, 0, 5:6, :] * _dot(merged.astype(BF16), wo_ref[0])


def _merge_call(x, mods, l, gt, y_lat, y_ctx, wd, wm, wf, wo, tm=512):
    b, t = y_lat[0].shape[:2]
    d = x.shape[2]
    tm = _tile(t, tm)
    tok = lambda wd_: pl.BlockSpec((1, tm, wd_), lambda i, j: (i, j, 0))
    if y_ctx is None:
        n_seg, y_specs, y_args = b, [tok(a.shape[2]) for a in y_lat], list(y_lat)
    else:
        n_seg, y_specs, y_args = b + 1, [], []
        for yl, yc in zip(y_lat, y_ctx):
            y_specs += list(_lat_or_ctx_specs(tm, yl.shape[2], b, t // tm))
            y_args += [yl, yc]
    return pl.pallas_call(
        functools.partial(_merge_body, n_lat_seg=None if y_ctx is None else b),
        grid=(n_seg, t // tm),
        in_specs=[tok(d), _mods_block(mods, l), tok(gt.shape[2])] + y_specs
                 + [_layer(wd, l), _layer(wm, l), _layer(wf, l), _layer(wo, l)],
        out_specs=tok(d),
        out_shape=jax.ShapeDtypeStruct((n_seg, t, d), F32),
        compiler_params=_params(2),
        name="merge_out",
    )(x, mods, gt, *y_args, wd, wm, wf, wo)


def _rope_tables(n_lat, n_ident):
    rows = np.arange(n_lat) // GRID_W
    cols = np.arange(n_lat) % GRID_W

    def cos_sin(dim):
        nf = dim // 4
        freqs = np.power(np.float32(ROPE_BASE), -np.arange(nf, dtype=np.float32) / np.float32(nf)).astype(np.float32)
        ar = rows.astype(np.float32)[:, None] * freqs[None, :]
        ac = cols.astype(np.float32)[:, None] * freqs[None, :]
        ang = np.concatenate([ar, ar, ac, ac], axis=-1).astype(np.float64)
        sign = np.concatenate([-np.ones(nf), np.ones(nf), -np.ones(nf), np.ones(nf)])
        return np.cos(ang), np.sin(ang) * sign[None, :]

    cd, sd = cos_sin(DIFF_HD)
    cos_d = np.tile(cd, (1, LANES // DIFF_HD))
    sin_d = np.tile(sd, (1, LANES // DIFF_HD))
    cm, sm = cos_sin(MLA_ROPE)
    cos_m = np.ones((n_lat, LANES))
    sin_m = np.zeros((n_lat, LANES))
    cos_m[:, :MLA_ROPE] = cm
    sin_m[:, :MLA_ROPE] = sm
    lat = np.stack([cos_d, sin_d, cos_m, sin_m])
    ident = np.stack([np.ones((n_ident, LANES)), np.zeros((n_ident, LANES))] * 2)
    return jnp.asarray(np.concatenate([lat, ident], axis=1), F32)


def _angles(rows, cols, n):
    return 2.0 * np.pi * ((rows[:, None] * cols[None, :]) % n) / n


def _pos_dft_tables(n, group):
    j, m = np.arange(n // 2), np.arange(n // 2)
    scale = 1.0 / math.sqrt(n * group)
    ae, ao = _angles(j, 2 * m, n), _angles(j, 2 * m + 1, n)
    return jnp.asarray(np.stack([np.cos(ae), -np.sin(ae), np.cos(ao), -np.sin(ao)]) * scale, BF16)


def _chan_dft_table(group):
    a = _angles(np.arange(group), np.arange(group), group)
    return jnp.asarray(np.concatenate([np.cos(a), np.sin(a)], axis=1), BF16)


def _proj_weight_body(w_ref, o_ref):
    o_ref[0] = w_ref[0].T.astype(BF16)


def _proj_weight_call(w_t, n_head, tail_start, blk=MXU_W):
    depth, cols, d = w_t.shape
    head_blocks, tail_blocks = n_head // blk, (cols - tail_start) // blk
    assert n_head % blk == 0 and (cols - tail_start) % blk == 0 and tail_start % 32 == 0
    start = lambda j: pl.multiple_of(jnp.where(j < head_blocks, j * blk, tail_start + (j - head_blocks) * blk), 32)
    return pl.pallas_call(
        _proj_weight_body,
        grid=(depth, head_blocks + tail_blocks),
        in_specs=[pl.BlockSpec((pl.Element(1), pl.Element(blk), pl.Element(d)), lambda l, j: (l, start(j), 0))],
        out_specs=pl.BlockSpec((1, d, blk), lambda l, j: (l, 0, j)),
        out_shape=jax.ShapeDtypeStruct((depth, d, (head_blocks + tail_blocks) * blk), BF16),
        compiler_params=_params(2),
        name="proj_weight",
    )(w_t)


def _prep_mla_weights(mla_w_uq, mla_w_ukv):
    depth = mla_w_uq.shape[0]
    tail = LANES - MLA_NOPE - MLA_ROPE
    uq = mla_w_uq.reshape(depth, MLA_Q_RANK, MLA_HEADS, MLA_NOPE + MLA_ROPE)
    uq = jnp.concatenate([uq[..., MLA_NOPE:], uq[..., :MLA_NOPE], jnp.zeros(uq.shape[:3] + (tail,), uq.dtype)], axis=-1)
    uq = uq.reshape(depth, MLA_Q_RANK, MLA_QK_W).astype(BF16)
    ukv = mla_w_ukv.reshape(depth, MLA_KV_RANK, MLA_HEADS, MLA_NOPE + MLA_VD)
    kn = jnp.pad(ukv[..., :MLA_NOPE], ((0, 0), (0, 0), (0, 0), (MLA_ROPE, tail)))
    kn = kn.reshape(depth, MLA_KV_RANK, MLA_QK_W)
    vv = ukv[..., MLA_NOPE:].reshape(depth, MLA_KV_RANK, MLA_V_W)
    return uq, jnp.concatenate([kn, vv], axis=2).astype(BF16)


def kernel(x, c, ctx, c_ctx, ada_w, ada_b, norm_g, ffn_w_in, ffn_w_out, w_in, diff_lambda,
           diff_subln_g, mla_q_norm_g, mla_w_uq, mla_kv_norm_g, mla_w_ukv, w_branch_diff,
           w_branch_mla, w_branch_fourier, w_out, final_norm_g):
    b, n_lat, d = x.shape
    n_ctx = ctx.shape[1]
    depth = ada_w.shape[0]
    assert b * n_ctx == n_lat, "context tokens of all batches must fill exactly one latent-length segment"

    cond_rows = -(-(b + 1) // 8) * 8
    cond = jnp.concatenate([c, c_ctx[None, :], jnp.zeros((cond_rows - b - 1, d), F32)], axis=0)
    mods = _ada_call(cond, ada_w, ada_b).reshape(depth, cond_rows, N_MOD, d)

    tab = _rope_tables(n_lat, PROJ_TILE)
    dft_x = _pos_dft_tables(n_lat, FOURIER_GROUP_DIM)
    dft_c = _pos_dft_tables(n_ctx, FOURIER_GROUP_DIM)
    cs128 = _chan_dft_table(FOURIER_GROUP_DIM)

    w_proj = _proj_weight_call(jnp.swapaxes(w_in, 1, 2), PROJ_WA_COLS, PROJ_KR_COL + MLA_ROPE)
    uq, ukv = _prep_mla_weights(mla_w_uq, mla_w_ukv)
    wd, wm, wf, wo = w_branch_diff, w_branch_mla, w_branch_fourier, w_out
    qn, kvn, sub_g = (a[:, None, :] for a in (mla_q_norm_g, mla_kv_norm_g, diff_subln_g))
    final_g = final_norm_g[None, :]
    flat = lambda a: a.reshape(1, b * n_ctx, a.shape[-1])

    h, h_ctx = x, ctx.reshape(1, b * n_ctx, d)
    for l in range(depth):
        last = l == depth - 1
        lam_init = 0.8 - 0.6 * math.exp(-0.3 * l)
        diff = functools.partial(_diff_body, lam_init=lam_init)
        diff_extra = ([_layer(diff_lambda, l), _layer(sub_g, l)], [diff_lambda, sub_g])
        sizes = (b, n_lat, n_ctx)

        h = _ffn_call(h, mods, l, norm_g, 0, ffn_w_in, ffn_w_out, 0, 0, ctx=h_ctx)
        h_ctx = None
        qd, kd, vd, qm, km, vm, zc, zs, gt = _proj_call(h, mods, l, norm_g, w_proj, uq, ukv, qn, kvn, cs128, tab, b)

        y_lat = (_attn_call(diff, "diff_attn", qd, kd, vd, *diff_extra, DIFF_HEADS, MXU_W, DIFF_W, *sizes),
                 _attn_call(_mla_body, "mla_attn", qm, km, vm, [], [], MLA_HEADS, LANES, MLA_V_W, *sizes),
                 _dft_call(dft_x, zc, zs, b, True))
        y_ctx = None
        if not last:
            yd_c, ym_c = _ctx_attn_call(lam_init, *diff_extra, qd, kd, vd, qm, km, vm, b, n_ctx)
            y_ctx = tuple(flat(a) for a in (yd_c, ym_c, _dft_call(dft_c, zc, zs, b, False)))
        h = _merge_call(h, mods, l, gt, y_lat, y_ctx, wd, wm, wf, wo)
        h = _ffn_call(h, mods, l, norm_g, 2, ffn_w_in, ffn_w_out, 1, 6, final_g=final_g if last else None)
    return h
```

```python
import functools
import math

import numpy as np
import jax
import jax.numpy as jnp
from jax import lax
from jax.experimental import pallas as pl
from jax.experimental.pallas import tpu as pltpu

F32 = jnp.float32
BF16 = jnp.bfloat16

GRID_W = 64
DIFF_HEADS = 4
DIFF_HD = 64
DIFF_VD = 2 * DIFF_HD
MLA_HEADS = 8
MLA_NOPE = 64
MLA_ROPE = 32
MLA_VD = 64
MLA_Q_RANK = 384
MLA_KV_RANK = 256
FOURIER_GROUPS = 4
FOURIER_GROUP_DIM = 128
N_BRANCHES = 3
ROPE_BASE = 10000.0
RMS_EPS = 1e-6
N_MOD = 9

DIFF_W = DIFF_HEADS * 2 * DIFF_HD
MLA_QK_W = MLA_HEADS * 128
MLA_V_W = MLA_HEADS * MLA_VD
FOURIER_W = FOURIER_GROUPS * FOURIER_GROUP_DIM

LANES = 128
MXU_W = 256
FFN_CHUNK = MXU_W
VMEM_LIMIT = 56 * 1024 * 1024
FFN_VMEM_LIMIT = 60 * 1024 * 1024
LOG2E = math.log2(math.e)
PROJ_TILE = 512
PROJ_KR_COL = 3 * DIFF_W + MLA_Q_RANK + MLA_KV_RANK
PROJ_WA_COLS = PROJ_KR_COL + 3 * LANES


def _tile(n, pref):
    t = min(n, pref)
    while n % t:
        t //= 2
    return t


def _fixed(block, idx):
    return pl.BlockSpec(block, lambda *_: idx, pipeline_mode=pl.Buffered(1))


def _whole(a):
    return _fixed(a.shape, (0,) * a.ndim)


def _layer(a, l):
    return _fixed((1,) + a.shape[1:], (l,) + (0,) * (a.ndim - 1))


def _params(n_axes, vmem_limit=VMEM_LIMIT):
    return pltpu.CompilerParams(dimension_semantics=("arbitrary",) * n_axes,
                                vmem_limit_bytes=vmem_limit)


def _norm_mod(x, g, shift, scale):
    y = x * lax.rsqrt(jnp.mean(x * x, axis=-1, keepdims=True) + RMS_EPS)
    return (y * (g * (1.0 + scale)) + shift).astype(BF16)


def _rms(x, g):
    return x * lax.rsqrt(jnp.mean(x * x, axis=-1, keepdims=True) + RMS_EPS) * g


def _dot(a, b):
    return jnp.dot(a, b, preferred_element_type=F32)


def _ada_body(s_ref, w_ref, b_ref, o_ref):
    s = s_ref[...]
    a = (s * jax.nn.sigmoid(s)).astype(BF16)
    o_ref[0] = _dot(a, w_ref[0].astype(BF16)) + b_ref[0]


def _ada_call(cond, ada_w, ada_b):
    depth, d, n = ada_w.shape
    rows = cond.shape[0]
    tn = _tile(n, 1024)
    return pl.pallas_call(
        _ada_body,
        grid=(depth, n // tn),
        in_specs=[pl.BlockSpec((rows, d), lambda l, j: (0, 0)),
                  pl.BlockSpec((1, d, tn), lambda l, j: (l, 0, j)),
                  pl.BlockSpec((1, 1, tn), lambda l, j: (l, 0, j))],
        out_specs=pl.BlockSpec((1, rows, tn), lambda l, j: (l, 0, j)),
        out_shape=jax.ShapeDtypeStruct((depth, rows, n), F32),
        compiler_params=_params(2),
        name="ada_mod",
    )(cond, ada_w, ada_b.reshape(depth, 1, n))


def _mods_block(mods, l):
    d = mods.shape[-1]
    return pl.BlockSpec((1, 1, N_MOD, d), lambda i, j: (l, i, 0, 0))


def _lat_or_ctx_specs(rows, width, n_lat_seg, n_tiles):
    lat = pl.BlockSpec((1, rows, width), lambda i, j: (jnp.minimum(i, n_lat_seg - 1),
                                                       jnp.where(i < n_lat_seg, j, n_tiles - 1), 0))
    ctx = pl.BlockSpec((1, rows, width), lambda i, j: (0, jnp.where(i < n_lat_seg, 0, j), 0))
    return lat, ctx


def _pick(n_lat_seg, lat_ref, ctx_ref):
    return jnp.where(pl.program_id(0) < n_lat_seg, lat_ref[0], ctx_ref[0])


def _ffn_body(x_ref, *rest, mod0, gi, final, n_lat_seg):
    if n_lat_seg is not None:
        x = _pick(n_lat_seg, x_ref, rest[0])
        rest = rest[1:]
    else:
        x = x_ref[0]
    mod_ref, g_ref, win_ref, wout_ref = rest[:4]
    if final:
        fg_ref, o_ref = rest[4:]
    else:
        (o_ref,) = rest[4:]
    d_ff = wout_ref.shape[2]
    xm = _norm_mod(x, g_ref[0, gi:gi + 1, :], mod_ref[0, 0, mod0:mod0 + 1, :], mod_ref[0, 0, mod0 + 1:mod0 + 2, :])
    acc = jnp.zeros(x.shape, F32)
    for c in range(d_ff // FFN_CHUNK):
        lo = c * FFN_CHUNK
        gate = _dot(xm, win_ref[0, 0, :, lo:lo + FFN_CHUNK])
        up = _dot(xm, win_ref[0, 0, :, d_ff + lo:d_ff + lo + FFN_CHUNK])
        act = (gate * jax.nn.sigmoid(gate) * up).astype(BF16)
        acc = acc + _dot(act, wout_ref[0, 0, lo:lo + FFN_CHUNK, :])
    y = x + (0.5 * mod_ref[0, 0, mod0 + 2:mod0 + 3, :]) * acc
    if final:
        y = _rms(y, fg_ref[...])
    o_ref[0] = y


def _ffn_call(x, mods, l, norm_g, gi, win, wout, which, mod0, ctx=None, final_g=None, tm=1024):
    s, t, d = x.shape
    tm = _tile(t, tm if ctx is None else tm // 2)
    final = final_g is not None
    n_seg = s + 1 if ctx is not None else s
    if ctx is not None:
        in_specs = list(_lat_or_ctx_specs(tm, d, s, t // tm))
        args = [x, ctx]
    else:
        in_specs = [pl.BlockSpec((1, tm, d), lambda i, j: (i, j, 0))]
        args = [x]
    in_specs += [_mods_block(mods, l), _layer(norm_g, l),
                 _fixed((1, 1) + win.shape[2:], (l, which, 0, 0)),
                 _fixed((1, 1) + wout.shape[2:], (l, which, 0, 0))]
    args += [mods, norm_g, win, wout]
    if final:
        in_specs.append(_whole(final_g))
        args.append(final_g)
    return pl.pallas_call(
        functools.partial(_ffn_body, mod0=mod0, gi=gi, final=final, n_lat_seg=s if ctx is not None else None),
        grid=(n_seg, t // tm),
        in_specs=in_specs,
        out_specs=pl.BlockSpec((1, tm, d), lambda i, j: (i, j, 0)),
        out_shape=jax.ShapeDtypeStruct((n_seg, t, d), F32),
        compiler_params=_params(2, FFN_VMEM_LIMIT),
        name="ffn",
    )(*args)


def _rope(x, cos, sin_signed, chunk):
    lane = lax.broadcasted_iota(jnp.int32, x.shape, 1)
    even = (lane % (2 * chunk)) < chunk
    partner = jnp.where(even, pltpu.roll(x, LANES - chunk, 1), pltpu.roll(x, chunk, 1))
    return x * cos + partner * sin_signed


def _proj_body(x_ref, mod_ref, g_ref, w_ref, wuq_ref, wukv_ref, qn_ref, kvn_ref, cs_ref, tab_ref,
               qd_ref, kd_ref, vd_ref, qm_ref, km_ref, vm_ref, zc_ref, zs_ref, gt_ref, z_scr,
               *, qscale_d, qscale_m):
    x = x_ref[0]
    xm = _norm_mod(x, g_ref[0, 1:2, :], mod_ref[0, 0, 3:4, :], mod_ref[0, 0, 4:5, :])
    cos_d, sin_d, cos_m, sin_m = tab_ref[0], tab_ref[1], tab_ref[2], tab_ref[3]

    def mm(part, c0, width):
        c0 += PROJ_WA_COLS * part
        return _dot(xm, w_ref[0, :, c0:c0 + width])

    HEAD, TAIL = 0, 1

    o_cq = 3 * DIFF_W
    o_ckv = o_cq + MLA_Q_RANK
    o_kr = o_ckv + MLA_KV_RANK
    cq = _rms(mm(HEAD, o_cq, MLA_Q_RANK), qn_ref[0]).astype(BF16)
    ckv = _rms(mm(HEAD, o_ckv, MLA_KV_RANK), kvn_ref[0]).astype(BF16)
    kr = mm(HEAD, o_kr, LANES)
    kr = jnp.where(lax.broadcasted_iota(jnp.int32, kr.shape, 1) < MLA_ROPE, kr, 0.0)
    kr = _rope(kr, cos_m, sin_m, MLA_ROPE // 4)
    qm = _dot(cq, wuq_ref[0])
    for h in range(MLA_HEADS):
        sl = slice(h * LANES, (h + 1) * LANES)
        qm_ref[0, :, sl] = (_rope(qm[:, sl], cos_m, sin_m, MLA_ROPE // 4) * qscale_m).astype(BF16)
    kv = _dot(ckv, wukv_ref[0])
    for h in range(MLA_HEADS):
        sl = slice(h * LANES, (h + 1) * LANES)
        km_ref[0, :, sl] = (kv[:, sl] + kr).astype(BF16)
    vm_ref[0] = kv[:, MLA_QK_W:].astype(BF16)

    q = mm(HEAD, 0, DIFF_W)
    for h in range(DIFF_HEADS):
        sl = slice(h * LANES, (h + 1) * LANES)
        qd_ref[0, :, sl] = (_rope(q[:, sl], cos_d, sin_d, DIFF_HD // 4) * qscale_d).astype(BF16)
    k = mm(HEAD, DIFF_W, DIFF_W)
    for h in range(DIFF_HEADS):
        sl = slice(h * LANES, (h + 1) * LANES)
        kd_ref[0, :, sl] = _rope(k[:, sl], cos_d, sin_d, DIFF_HD // 4).astype(BF16)

    d = x.shape[-1]
    for bi in range(N_BRANCHES):
        gt_ref[0, :, bi * d:(bi + 1) * d] = (0.5 * jnp.tanh(0.5 * mm(TAIL, FOURIER_W + bi * d, d)) + 0.5).astype(BF16)

    f = mm(TAIL, 0, FOURIER_W).astype(BF16)
    for gi in range(FOURIER_GROUPS):
        sl = slice(gi * LANES, (gi + 1) * LANES)
        z = _dot(f[:, sl], cs_ref[...])
        z_scr[gi] = z[:, :LANES]
        z_scr[FOURIER_GROUPS + gi] = z[:, LANES:]
    half = z_scr.shape[1] // 2
    for parity in range(2):
        for gi in range(FOURIER_GROUPS):
            sl = slice(parity * FOURIER_W + gi * LANES, parity * FOURIER_W + (gi + 1) * LANES)
            zc_ref[0, :, sl] = z_scr[gi, pl.ds(parity, half, stride=2), :].astype(BF16)
            zs_ref[0, :, sl] = z_scr[FOURIER_GROUPS + gi, pl.ds(parity, half, stride=2), :].astype(BF16)
    vd_ref[0] = mm(HEAD, 2 * DIFF_W, DIFF_W).astype(BF16)


def _proj_call(x, mods, l, norm_g, w, wuq, wukv, qn, kvn, cs128, tab, n_lat_seg, tm=PROJ_TILE):
    s, t, d = x.shape
    assert t % tm == 0 and tab.shape[1] == t + tm
    tok = lambda wd, rows=tm: pl.BlockSpec((1, rows, wd), lambda i, j: (i, j, 0))
    tab_map = lambda i, j: (0, jnp.where(i < n_lat_seg, j, t // tm), 0)
    full = [DIFF_W, DIFF_W, DIFF_W, MLA_QK_W, MLA_QK_W, MLA_V_W]
    out_specs = ([tok(wd) for wd in full] + [tok(2 * FOURIER_W, tm // 2)] * 2 + [tok(N_BRANCHES * d)])
    out_shape = ([jax.ShapeDtypeStruct((s, t, wd), BF16) for wd in full]
                 + [jax.ShapeDtypeStruct((s, t // 2, 2 * FOURIER_W), BF16)] * 2
                 + [jax.ShapeDtypeStruct((s, t, N_BRANCHES * d), BF16)])
    return pl.pallas_call(
        functools.partial(_proj_body,
                          qscale_d=DIFF_HD ** -0.5 * LOG2E,
                          qscale_m=(MLA_NOPE + MLA_ROPE) ** -0.5 * LOG2E),
        grid=(s, t // tm),
        in_specs=[tok(d), _mods_block(mods, l), _layer(norm_g, l),
                  _layer(w, l), _layer(wuq, l), _layer(wukv, l), _layer(qn, l), _layer(kvn, l),
                  _whole(cs128), pl.BlockSpec((4, tm, LANES), tab_map)],
        out_specs=out_specs,
        out_shape=out_shape,
        scratch_shapes=[pltpu.VMEM((2 * FOURIER_GROUPS, tm, LANES), F32)],
        compiler_params=_params(2),
        name="in_proj",
    )(x, mods, norm_g, w, wuq, wukv, qn, kvn, cs128, tab)


def _fill_kv(srcs_k, srcs_v, k_scr, v_scr, heads, vd):
    vw = v_scr.shape[1] // heads
    r0 = 0
    for k_ref, v_ref in zip(srcs_k, srcs_v):
        n = k_ref.shape[1]
        k_scr[r0:r0 + n, :] = k_ref[0]
        lane = lax.broadcasted_iota(jnp.int32, (n, vw - vd), 1)
        ones_col = jnp.where(lane == 0, 1.0, 0.0).astype(BF16)
        for h in range(heads):
            v_scr[r0:r0 + n, h * vw:h * vw + vd] = v_ref[0, :, h * vd:(h + 1) * vd]
            v_scr[r0:r0 + n, h * vw + vd:(h + 1) * vw] = ones_col
        r0 += n


def _softmax_pv(s, v_aug, vd):
    m = jnp.max(s, axis=-1, keepdims=True)
    e = jnp.exp2(s - m).astype(BF16)
    r = _dot(e, v_aug)
    return r[:, :vd], r[:, vd:vd + 1]


SMALL_UNIT_ROWS = 128


def _head_row_units(heads, rows):
    if rows <= 2 * SMALL_UNIT_ROWS:
        return [(h, 0, rows) for h in range(heads)]
    units = [(0, 0, SMALL_UNIT_ROWS), (0, SMALL_UNIT_ROWS, rows)]
    units += [(h, 0, rows) for h in range(1, heads - 1)]
    units += [(heads - 1, 0, rows - SMALL_UNIT_ROWS), (heads - 1, rows - SMALL_UNIT_ROWS, rows)]
    return units


def _qk(q, k):
    return lax.dot_general(q, k, (((1,), (1,)), ((), ())), preferred_element_type=F32)


def _diff_body(*refs, n_src, lam_init):
    q_ref, lam_ref, sg_ref = refs[0], refs[1], refs[2]
    k_srcs = refs[3:3 + n_src]
    v_srcs = refs[3 + n_src:3 + 2 * n_src]
    o_ref, k_scr, v_scr = refs[3 + 2 * n_src:]

    @pl.when(pl.program_id(1) == 0)
    def _():
        _fill_kv(k_srcs, v_srcs, k_scr, v_scr, DIFF_HEADS, DIFF_VD)

    lp = lam_ref[0]
    lam = (jnp.exp(jnp.sum(lp[0:1] * lp[1:2], axis=-1, keepdims=True))
           - jnp.exp(jnp.sum(lp[2:3] * lp[3:4], axis=-1, keepdims=True)) + lam_init)
    zero = jnp.zeros((), BF16)
    vw = v_scr.shape[1] // DIFF_HEADS
    for h, r0, r1 in _head_row_units(DIFF_HEADS, q_ref.shape[1]):
        sl = slice(h * LANES, (h + 1) * LANES)
        qh, kh, vh = q_ref[0, r0:r1, sl], k_scr[:, sl], v_scr[:, h * vw:(h + 1) * vw]
        first = lax.broadcasted_iota(jnp.int32, qh.shape, 1) < DIFF_HD
        o1, l1 = _softmax_pv(_qk(jnp.where(first, qh, zero), kh), vh, DIFF_VD)
        o2, l2 = _softmax_pv(_qk(jnp.where(first, zero, qh), kh), vh, DIFF_VD)
        o = o1 * (1.0 / l1) - o2 * (lam / l2)
        o_ref[0, r0:r1, sl] = (_rms(o, sg_ref[0]) * (1.0 - lam_init)).astype(BF16)


def _mla_body(*refs, n_src):
    q_ref = refs[0]
    k_srcs = refs[1:1 + n_src]
    v_srcs = refs[1 + n_src:1 + 2 * n_src]
    o_ref, k_scr, v_scr = refs[1 + 2 * n_src:]

    @pl.when(pl.program_id(1) == 0)
    def _():
        _fill_kv(k_srcs, v_srcs, k_scr, v_scr, MLA_HEADS, MLA_VD)

    vw = v_scr.shape[1] // MLA_HEADS
    for h, r0, r1 in _head_row_units(MLA_HEADS, q_ref.shape[1]):
        sl = slice(h * LANES, (h + 1) * LANES)
        o, l = _softmax_pv(_qk(q_ref[0, r0:r1, sl], k_scr[:, sl]), v_scr[:, h * vw:(h + 1) * vw], MLA_VD)
        o_ref[0, r0:r1, h * MLA_VD:(h + 1) * MLA_VD] = (o * (1.0 / l)).astype(BF16)


def _ctx_rows_spec(a, b, n_ctx):
    return pl.BlockSpec((1, n_ctx, a.shape[2]), lambda i, j: (b, i, 0))


def _attn_call(body, name, q, k, v, extra_specs, extra, heads, v_aug_w, out_w, b, n_lat, n_ctx, tq=512):
    lat_rows = lambda a: pl.BlockSpec((1, n_lat, a.shape[2]), lambda i, j: (i, 0, 0))
    tq = _tile(n_lat, tq)
    return pl.pallas_call(
        functools.partial(body, n_src=2),
        grid=(b, n_lat // tq),
        in_specs=([pl.BlockSpec((1, tq, q.shape[2]), lambda i, j: (i, j, 0))] + extra_specs
                  + [_ctx_rows_spec(k, b, n_ctx), lat_rows(k), _ctx_rows_spec(v, b, n_ctx), lat_rows(v)]),
        out_specs=pl.BlockSpec((1, tq, out_w), lambda i, j: (i, j, 0)),
        out_shape=jax.ShapeDtypeStruct((b, n_lat, out_w), BF16),
        scratch_shapes=[pltpu.VMEM((n_ctx + n_lat, k.shape[2]), BF16),
                        pltpu.VMEM((n_ctx + n_lat, heads * v_aug_w), BF16)],
        compiler_params=_params(2),
        name=name,
    )(q, *extra, k, k, v, v)


def _ctx_attn_body(qd_ref, lam_ref, sg_ref, kd_ref, vd_ref, qm_ref, km_ref, vm_ref, od_ref, om_ref,
                   kd_scr, vd_scr, km_scr, vm_scr, *, lam_init):
    _diff_body(qd_ref, lam_ref, sg_ref, kd_ref, vd_ref, od_ref, kd_scr, vd_scr, n_src=1, lam_init=lam_init)
    _mla_body(qm_ref, km_ref, vm_ref, om_ref, km_scr, vm_scr, n_src=1)


def _ctx_attn_call(lam_init, diff_extra_specs, diff_extra, qd, kd, vd, qm, km, vm, b, n_ctx):
    rows = lambda a: _ctx_rows_spec(a, b, n_ctx)
    out = lambda w: pl.BlockSpec((1, n_ctx, w), lambda i, j: (i, 0, 0))
    return pl.pallas_call(
        functools.partial(_ctx_attn_body, lam_init=lam_init),
        grid=(b, 1),
        in_specs=[rows(qd)] + diff_extra_specs + [rows(kd), rows(vd), rows(qm), rows(km), rows(vm)],
        out_specs=[out(DIFF_W), out(MLA_V_W)],
        out_shape=[jax.ShapeDtypeStruct((b, n_ctx, DIFF_W), BF16),
                   jax.ShapeDtypeStruct((b, n_ctx, MLA_V_W), BF16)],
        scratch_shapes=[pltpu.VMEM((n_ctx, kd.shape[2]), BF16), pltpu.VMEM((n_ctx, DIFF_HEADS * MXU_W), BF16),
                        pltpu.VMEM((n_ctx, km.shape[2]), BF16), pltpu.VMEM((n_ctx, MLA_HEADS * LANES), BF16)],
        compiler_params=_params(2),
        name="ctx_attn",
    )(qd, *diff_extra, kd, vd, qm, km, vm)


def _dft_body(t_ref, zc_ref, zs_ref, o_ref):
    w = o_ref.shape[-1]
    even = _dot(t_ref[0], zc_ref[0, :, :w]) + _dot(t_ref[1], zs_ref[0, :, :w])
    odd = _dot(t_ref[2], zc_ref[0, :, w:]) + _dot(t_ref[3], zs_ref[0, :, w:])
    o_ref[0, 0] = (even + odd).astype(BF16)
    o_ref[0, 1] = (even - odd).astype(BF16)


def _dft_call(tables, zc, zs, b, latent, tm=1024):
    half, w = tables.shape[1], zc.shape[2] // 2
    t = 2 * half
    tm = _tile(half, tm)
    z_spec = pl.BlockSpec((1, half, 2 * w), (lambda i, j: (j, 0, 0)) if latent else (lambda i, j: (b, j, 0)))
    out = pl.pallas_call(
        _dft_body,
        grid=(half // tm, b),
        in_specs=[pl.BlockSpec((4, tm, half), lambda i, j: (0, i, 0)), z_spec, z_spec],
        out_specs=pl.BlockSpec((1, 2, tm, w), lambda i, j: (j, 0, i, 0)),
        out_shape=jax.ShapeDtypeStruct((b, 2, half, w), BF16),
        compiler_params=_params(2),
        name="pos_dft",
    )(tables, zc, zs)
    return out.reshape(b, t, w)


def _merge_body(x_ref, mod_ref, gt_ref, *rest, n_lat_seg):
    if n_lat_seg is None:
        (yd_ref, ym_ref, yf_ref), rest = rest[:3], rest[3:]
        yd, ym, yf = yd_ref[0], ym_ref[0], yf_ref[0]
    else:
        ys, rest = rest[:6], rest[6:]
        yd, ym, yf = (_pick(n_lat_seg, ys[2 * i], ys[2 * i + 1]) for i in range(N_BRANCHES))
    wd_ref, wm_ref, wf_ref, wo_ref, o_ref = rest
    x = x_ref[0]
    d = x.shape[-1]
    merged = (gt_ref[0, :, 0:d].astype(F32) * _dot(yd, wd_ref[0])
              + gt_ref[0, :, d:2 * d].astype(F32) * _dot(ym, wm_ref[0])
              + gt_ref[0, :, 2 * d:3 * d].astype(F32) * _dot(yf, wf_ref[0]))
    o_ref[0] = x + mod_ref[0, 0, 5:6, :] * _dot(merged.astype(BF16), wo_ref[0])


def _merge_call(x, mods, l, gt, y_lat, y_ctx, wd, wm, wf, wo, tm=512):
    b, t = y_lat[0].shape[:2]
    d = x.shape[2]
    tm = _tile(t, tm)
    tok = lambda wd_: pl.BlockSpec((1, tm, wd_), lambda i, j: (i, j, 0))
    if y_ctx is None:
        n_seg, y_specs, y_args = b, [tok(a.shape[2]) for a in y_lat], list(y_lat)
    else:
        n_seg, y_specs, y_args = b + 1, [], []
        for yl, yc in zip(y_lat, y_ctx):
            y_specs += list(_lat_or_ctx_specs(tm, yl.shape[2], b, t // tm))
            y_args += [yl, yc]
    return pl.pallas_call(
        functools.partial(_merge_body, n_lat_seg=None if y_ctx is None else b),
        grid=(n_seg, t // tm),
        in_specs=[tok(d), _mods_block(mods, l), tok(gt.shape[2])] + y_specs
                 + [_layer(wd, l), _layer(wm, l), _layer(wf, l), _layer(wo, l)],
        out_specs=tok(d),
        out_shape=jax.ShapeDtypeStruct((n_seg, t, d), F32),
        compiler_params=_params(2),
        name="merge_out",
    )(x, mods, gt, *y_args, wd, wm, wf, wo)


def _rope_tables(n_lat, n_ident):
    rows = np.arange(n_lat) // GRID_W
    cols = np.arange(n_lat) % GRID_W

    def cos_sin(dim):
        nf = dim // 4
        freqs = np.power(ROPE_BASE, -np.arange(nf, dtype=np.float64) / nf)
        ar = rows.astype(np.float64)[:, None] * freqs[None, :]
        ac = cols.astype(np.float64)[:, None] * freqs[None, :]
        ang = np.concatenate([ar, ar, ac, ac], axis=-1)
        sign = np.concatenate([-np.ones(nf), np.ones(nf), -np.ones(nf), np.ones(nf)])
        return np.cos(ang), np.sin(ang) * sign[None, :]

    cd, sd = cos_sin(DIFF_HD)
    cos_d = np.tile(cd, (1, LANES // DIFF_HD))
    sin_d = np.tile(sd, (1, LANES // DIFF_HD))
    cm, sm = cos_sin(MLA_ROPE)
    cos_m = np.ones((n_lat, LANES))
    sin_m = np.zeros((n_lat, LANES))
    cos_m[:, :MLA_ROPE] = cm
    sin_m[:, :MLA_ROPE] = sm
    lat = np.stack([cos_d, sin_d, cos_m, sin_m])
    ident = np.stack([np.ones((n_ident, LANES)), np.zeros((n_ident, LANES))] * 2)
    return jnp.asarray(np.concatenate([lat, ident], axis=1), F32)


def _angles(rows, cols, n):
    return 2.0 * np.pi * ((rows[:, None] * cols[None, :]) % n) / n


def _pos_dft_tables(n, group):
    j, m = np.arange(n // 2), np.arange(n // 2)
    scale = 1.0 / math.sqrt(n * group)
    ae, ao = _angles(j, 2 * m, n), _angles(j, 2 * m + 1, n)
    return jnp.asarray(np.stack([np.cos(ae), -np.sin(ae), np.cos(ao), -np.sin(ao)]) * scale, F32).astype(BF16)


def _chan_dft_table(group):
    a = _angles(np.arange(group), np.arange(group), group)
    return jnp.asarray(np.concatenate([np.cos(a), np.sin(a)], axis=1), F32).astype(BF16)


def _proj_weight_body(*refs):
    *w_refs, o_ref = refs
    blk = w_refs[0].shape[1]
    for k, w_ref in enumerate(w_refs):
        o_ref[0, :, k * blk:(k + 1) * blk] = w_ref[0].T.astype(BF16)


def _proj_weight_call(w_t, n_head, tail_start, blk=MXU_W, per_step=4):
    depth, cols, d = w_t.shape
    head_blocks, tail_blocks = n_head // blk, (cols - tail_start) // blk
    assert n_head % blk == 0 and (cols - tail_start) % blk == 0 and tail_start % 32 == 0
    assert (head_blocks + tail_blocks) % per_step == 0

    def window(k):
        def index(l, j):
            m = j * per_step + k
            return l, pl.multiple_of(jnp.where(m < head_blocks, m * blk, tail_start + (m - head_blocks) * blk), 32), 0
        return pl.BlockSpec((pl.Element(1), pl.Element(blk), pl.Element(d)), index)

    return pl.pallas_call(
        _proj_weight_body,
        grid=(depth, (head_blocks + tail_blocks) // per_step),
        in_specs=[window(k) for k in range(per_step)],
        out_specs=pl.BlockSpec((1, d, per_step * blk), lambda l, j: (l, 0, j)),
        out_shape=jax.ShapeDtypeStruct((depth, d, (head_blocks + tail_blocks) * blk), BF16),
        compiler_params=_params(2),
        name="proj_weight",
    )(*([w_t] * per_step))


def _prep_mla_weights(mla_w_uq, mla_w_ukv):
    depth = mla_w_uq.shape[0]
    tail = LANES - MLA_NOPE - MLA_ROPE
    uq = mla_w_uq.reshape(depth, MLA_Q_RANK, MLA_HEADS, MLA_NOPE + MLA_ROPE)
    uq = jnp.concatenate([uq[..., MLA_NOPE:], uq[..., :MLA_NOPE], jnp.zeros(uq.shape[:3] + (tail,), uq.dtype)], axis=-1)
    uq = uq.reshape(depth, MLA_Q_RANK, MLA_QK_W).astype(BF16)
    ukv = mla_w_ukv.reshape(depth, MLA_KV_RANK, MLA_HEADS, MLA_NOPE + MLA_VD)
    kn = jnp.pad(ukv[..., :MLA_NOPE], ((0, 0), (0, 0), (0, 0), (MLA_ROPE, tail)))
    kn = kn.reshape(depth, MLA_KV_RANK, MLA_QK_W)
    vv = ukv[..., MLA_NOPE:].reshape(depth, MLA_KV_RANK, MLA_V_W)
    return uq, jnp.concatenate([kn, vv], axis=2).astype(BF16)


def kernel(x, c, ctx, c_ctx, ada_w, ada_b, norm_g, ffn_w_in, ffn_w_out, w_in, diff_lambda,
           diff_subln_g, mla_q_norm_g, mla_w_uq, mla_kv_norm_g, mla_w_ukv, w_branch_diff,
           w_branch_mla, w_branch_fourier, w_out, final_norm_g):
    b, n_lat, d = x.shape
    n_ctx = ctx.shape[1]
    depth = ada_w.shape[0]
    assert b * n_ctx == n_lat, "context tokens of all batches must fill exactly one latent-length segment"

    cond_rows = -(-(b + 1) // 8) * 8
    cond = jnp.concatenate([c, c_ctx[None, :], jnp.zeros((cond_rows - b - 1, d), F32)], axis=0)
    mods = _ada_call(cond, ada_w, ada_b).reshape(depth, cond_rows, N_MOD, d)

    tab = _rope_tables(n_lat, PROJ_TILE)
    dft_x = _pos_dft_tables(n_lat, FOURIER_GROUP_DIM)
    dft_c = _pos_dft_tables(n_ctx, FOURIER_GROUP_DIM)
    cs128 = _chan_dft_table(FOURIER_GROUP_DIM)

    w_proj = _proj_weight_call(jnp.swapaxes(w_in, 1, 2), PROJ_WA_COLS, PROJ_KR_COL + MLA_ROPE)
    uq, ukv = _prep_mla_weights(mla_w_uq, mla_w_ukv)
    wd, wm, wf, wo = w_branch_diff, w_branch_mla, w_branch_fourier, w_out
    qn, kvn, sub_g = (a[:, None, :] for a in (mla_q_norm_g, mla_kv_norm_g, diff_subln_g))
    final_g = final_norm_g[None, :]
    flat = lambda a: a.reshape(1, b * n_ctx, a.shape[-1])

    h, h_ctx = x, ctx.reshape(1, b * n_ctx, d)
    for l in range(depth):
        last = l == depth - 1
        lam_init = 0.8 - 0.6 * math.exp(-0.3 * l)
        diff = functools.partial(_diff_body, lam_init=lam_init)
        diff_extra = ([_layer(diff_lambda, l), _layer(sub_g, l)], [diff_lambda, sub_g])
        sizes = (b, n_lat, n_ctx)

        h = _ffn_call(h, mods, l, norm_g, 0, ffn_w_in, ffn_w_out, 0, 0, ctx=h_ctx)
        h_ctx = None
        qd, kd, vd, qm, km, vm, zc, zs, gt = _proj_call(h, mods, l, norm_g, w_proj, uq, ukv, qn, kvn, cs128, tab, b)

        y_lat = (_attn_call(diff, "diff_attn", qd, kd, vd, *diff_extra, DIFF_HEADS, MXU_W, DIFF_W, *sizes),
                 _attn_call(_mla_body, "mla_attn", qm, km, vm, [], [], MLA_HEADS, LANES, MLA_V_W, *sizes),
                 _dft_call(dft_x, zc, zs, b, True))
        y_ctx = None
        if not last:
            yd_c, ym_c = _ctx_attn_call(lam_init, *diff_extra, qd, kd, vd, qm, km, vm, b, n_ctx)
            y_ctx = tuple(flat(a) for a in (yd_c, ym_c, _dft_call(dft_c, zc, zs, b, False)))
        h = _merge_call(h, mods, l, gt, y_lat, y_ctx, wd, wm, wf, wo)
        h = _ffn_call(h, mods, l, norm_g, 2, ffn_w_in, ffn_w_out, 1, 6, final_g=final_g if last else None)
    return h
```

```python
import functools
import math

import numpy as np
import jax
import jax.numpy as jnp
from jax import lax
from jax.experimental import pallas as pl
from jax.experimental.pallas import tpu as pltpu

F32 = jnp.float32
BF16 = jnp.bfloat16

GRID_W = 64
DIFF_HEADS = 4
DIFF_HD = 64
DIFF_VD = 2 * DIFF_HD
MLA_HEADS = 8
MLA_NOPE = 64
MLA_ROPE = 32
MLA_VD = 64
MLA_Q_RANK = 384
MLA_KV_RANK = 256
FOURIER_GROUPS = 4
FOURIER_GROUP_DIM = 128
N_BRANCHES = 3
ROPE_BASE = 10000.0
RMS_EPS = 1e-6
N_MOD = 9

DIFF_W = DIFF_HEADS * 2 * DIFF_HD
MLA_QK_W = MLA_HEADS * 128
MLA_V_W = MLA_HEADS * MLA_VD
FOURIER_W = FOURIER_GROUPS * FOURIER_GROUP_DIM

LANES = 128
MXU_W = 256
FFN_CHUNK = MXU_W
VMEM_LIMIT = 56 * 1024 * 1024
FFN_VMEM_LIMIT = 60 * 1024 * 1024
LOG2E = math.log2(math.e)
PROJ_TILE = 512
PROJ_KR_COL = 3 * DIFF_W + MLA_Q_RANK + MLA_KV_RANK
PROJ_WA_COLS = PROJ_KR_COL + 3 * LANES


def _tile(n, pref):
    t = min(n, pref)
    while n % t:
        t //= 2
    return t


def _fixed(block, idx):
    return pl.BlockSpec(block, lambda *_: idx, pipeline_mode=pl.Buffered(1))


def _whole(a):
    return _fixed(a.shape, (0,) * a.ndim)


def _layer(a, l):
    return _fixed((1,) + a.shape[1:], (l,) + (0,) * (a.ndim - 1))


def _params(n_axes, vmem_limit=VMEM_LIMIT):
    return pltpu.CompilerParams(dimension_semantics=("arbitrary",) * n_axes,
                                vmem_limit_bytes=vmem_limit)


def _norm_mod(x, g, shift, scale):
    y = x * lax.rsqrt(jnp.mean(x * x, axis=-1, keepdims=True) + RMS_EPS)
    return (y * (g * (1.0 + scale)) + shift).astype(BF16)


def _rms(x, g):
    return x * lax.rsqrt(jnp.mean(x * x, axis=-1, keepdims=True) + RMS_EPS) * g


def _dot(a, b):
    return jnp.dot(a, b, preferred_element_type=F32)


def _ada_body(s_ref, w_ref, b_ref, o_ref):
    s = s_ref[...]
    a = (s * jax.nn.sigmoid(s)).astype(BF16)
    o_ref[0] = _dot(a, w_ref[0].astype(BF16)) + b_ref[0]


def _ada_call(cond, ada_w, ada_b):
    depth, d, n = ada_w.shape
    rows = cond.shape[0]
    tn = _tile(n, 1024)
    return pl.pallas_call(
        _ada_body,
        grid=(depth, n // tn),
        in_specs=[pl.BlockSpec((rows, d), lambda l, j: (0, 0)),
                  pl.BlockSpec((1, d, tn), lambda l, j: (l, 0, j)),
                  pl.BlockSpec((1, 1, tn), lambda l, j: (l, 0, j))],
        out_specs=pl.BlockSpec((1, rows, tn), lambda l, j: (l, 0, j)),
        out_shape=jax.ShapeDtypeStruct((depth, rows, n), F32),
        compiler_params=_params(2),
        name="ada_mod",
    )(cond, ada_w, ada_b.reshape(depth, 1, n))


def _mods_block(mods, l):
    d = mods.shape[-1]
    return pl.BlockSpec((1, 1, N_MOD, d), lambda i, j: (l, i, 0, 0))


def _lat_or_ctx_specs(rows, width, n_lat_seg, n_tiles):
    lat = pl.BlockSpec((1, rows, width), lambda i, j: (jnp.minimum(i, n_lat_seg - 1),
                                                       jnp.where(i < n_lat_seg, j, n_tiles - 1), 0))
    ctx = pl.BlockSpec((1, rows, width), lambda i, j: (0, jnp.where(i < n_lat_seg, 0, j), 0))
    return lat, ctx


def _pick(n_lat_seg, lat_ref, ctx_ref):
    return jnp.where(pl.program_id(0) < n_lat_seg, lat_ref[0], ctx_ref[0])


def _ffn_body(x_ref, *rest, mod0, gi, final, n_lat_seg):
    if n_lat_seg is not None:
        x = _pick(n_lat_seg, x_ref, rest[0])
        rest = rest[1:]
    else:
        x = x_ref[0]
    mod_ref, g_ref, win_ref, wout_ref = rest[:4]
    if final:
        fg_ref, o_ref = rest[4:]
    else:
        (o_ref,) = rest[4:]
    d_ff = wout_ref.shape[2]
    xm = _norm_mod(x, g_ref[0, gi:gi + 1, :], mod_ref[0, 0, mod0:mod0 + 1, :], mod_ref[0, 0, mod0 + 1:mod0 + 2, :])
    acc = jnp.zeros(x.shape, F32)
    for c in range(d_ff // FFN_CHUNK):
        lo = c * FFN_CHUNK
        gate = _dot(xm, win_ref[0, 0, :, lo:lo + FFN_CHUNK])
        up = _dot(xm, win_ref[0, 0, :, d_ff + lo:d_ff + lo + FFN_CHUNK])
        act = (gate * jax.nn.sigmoid(gate) * up).astype(BF16)
        acc = acc + _dot(act, wout_ref[0, 0, lo:lo + FFN_CHUNK, :])
    y = x + (0.5 * mod_ref[0, 0, mod0 + 2:mod0 + 3, :]) * acc
    if final:
        y = _rms(y, fg_ref[...])
    o_ref[0] = y


def _ffn_call(x, mods, l, norm_g, gi, win, wout, which, mod0, ctx=None, final_g=None, tm=1024):
    s, t, d = x.shape
    tm = _tile(t, tm if ctx is None else tm // 2)
    final = final_g is not None
    n_seg = s + 1 if ctx is not None else s
    if ctx is not None:
        in_specs = list(_lat_or_ctx_specs(tm, d, s, t // tm))
        args = [x, ctx]
    else:
        in_specs = [pl.BlockSpec((1, tm, d), lambda i, j: (i, j, 0))]
        args = [x]
    in_specs += [_mods_block(mods, l), _layer(norm_g, l),
                 _fixed((1, 1) + win.shape[2:], (l, which, 0, 0)),
                 _fixed((1, 1) + wout.shape[2:], (l, which, 0, 0))]
    args += [mods, norm_g, win, wout]
    if final:
        in_specs.append(_whole(final_g))
        args.append(final_g)
    return pl.pallas_call(
        functools.partial(_ffn_body, mod0=mod0, gi=gi, final=final, n_lat_seg=s if ctx is not None else None),
        grid=(n_seg, t // tm),
        in_specs=in_specs,
        out_specs=pl.BlockSpec((1, tm, d), lambda i, j: (i, j, 0)),
        out_shape=jax.ShapeDtypeStruct((n_seg, t, d), F32),
        compiler_params=_params(2, FFN_VMEM_LIMIT),
        name="ffn",
    )(*args)


def _rope(x, cos, sin_signed, chunk):
    lane = lax.broadcasted_iota(jnp.int32, x.shape, 1)
    even = (lane % (2 * chunk)) < chunk
    partner = jnp.where(even, pltpu.roll(x, LANES - chunk, 1), pltpu.roll(x, chunk, 1))
    return x * cos + partner * sin_signed


def _proj_body(x_ref, mod_ref, g_ref, w_ref, wuq_ref, wukv_ref, qn_ref, kvn_ref, cs_ref, tab_ref,
               qd_ref, kd_ref, vd_ref, qm_ref, km_ref, vm_ref, zc_ref, zs_ref, gt_ref, z_scr,
               *, qscale_d, qscale_m):
    x = x_ref[0]
    xm = _norm_mod(x, g_ref[0, 1:2, :], mod_ref[0, 0, 3:4, :], mod_ref[0, 0, 4:5, :])
    cos_d, sin_d, cos_m, sin_m = tab_ref[0], tab_ref[1], tab_ref[2], tab_ref[3]

    def mm(part, c0, width):
        c0 += PROJ_WA_COLS * part
        return _dot(xm, w_ref[0, :, c0:c0 + width])

    HEAD, TAIL = 0, 1

    o_cq = 3 * DIFF_W
    o_ckv = o_cq + MLA_Q_RANK
    o_kr = o_ckv + MLA_KV_RANK
    cq = _rms(mm(HEAD, o_cq, MLA_Q_RANK), qn_ref[0]).astype(BF16)
    ckv = _rms(mm(HEAD, o_ckv, MLA_KV_RANK), kvn_ref[0]).astype(BF16)
    kr = mm(HEAD, o_kr, LANES)
    kr = jnp.where(lax.broadcasted_iota(jnp.int32, kr.shape, 1) < MLA_ROPE, kr, 0.0)
    kr = _rope(kr, cos_m, sin_m, MLA_ROPE // 4)
    qm = _dot(cq, wuq_ref[0])
    for h in range(MLA_HEADS):
        sl = slice(h * LANES, (h + 1) * LANES)
        qm_ref[0, :, sl] = (_rope(qm[:, sl], cos_m, sin_m, MLA_ROPE // 4) * qscale_m).astype(BF16)
    kv = _dot(ckv, wukv_ref[0])
    for h in range(MLA_HEADS):
        sl = slice(h * LANES, (h + 1) * LANES)
        km_ref[0, :, sl] = (kv[:, sl] + kr).astype(BF16)
    vm_ref[0] = kv[:, MLA_QK_W:].astype(BF16)

    q = mm(HEAD, 0, DIFF_W)
    for h in range(DIFF_HEADS):
        sl = slice(h * LANES, (h + 1) * LANES)
        qd_ref[0, :, sl] = (_rope(q[:, sl], cos_d, sin_d, DIFF_HD // 4) * qscale_d).astype(BF16)
    k = mm(HEAD, DIFF_W, DIFF_W)
    for h in range(DIFF_HEADS):
        sl = slice(h * LANES, (h + 1) * LANES)
        kd_ref[0, :, sl] = _rope(k[:, sl], cos_d, sin_d, DIFF_HD // 4).astype(BF16)

    d = x.shape[-1]
    for bi in range(N_BRANCHES):
        gt_ref[0, :, bi * d:(bi + 1) * d] = (0.5 * jnp.tanh(0.5 * mm(TAIL, FOURIER_W + bi * d, d)) + 0.5).astype(BF16)

    f = mm(TAIL, 0, FOURIER_W).astype(BF16)
    for gi in range(FOURIER_GROUPS):
        sl = slice(gi * LANES, (gi + 1) * LANES)
        z = _dot(f[:, sl], cs_ref[...])
        z_scr[gi] = z[:, :LANES]
        z_scr[FOURIER_GROUPS + gi] = z[:, LANES:]
    half = z_scr.shape[1] // 2
    for parity in range(2):
        for gi in range(FOURIER_GROUPS):
            sl = slice(parity * FOURIER_W + gi * LANES, parity * FOURIER_W + (gi + 1) * LANES)
            zc_ref[0, :, sl] = z_scr[gi, pl.ds(parity, half, stride=2), :].astype(BF16)
            zs_ref[0, :, sl] = z_scr[FOURIER_GROUPS + gi, pl.ds(parity, half, stride=2), :].astype(BF16)
    vd_ref[0] = mm(HEAD, 2 * DIFF_W, DIFF_W).astype(BF16)


def _proj_call(x, mods, l, norm_g, w, wuq, wukv, qn, kvn, cs128, tab, n_lat_seg, tm=PROJ_TILE):
    s, t, d = x.shape
    assert t % tm == 0 and tab.shape[1] == t + tm
    tok = lambda wd, rows=tm: pl.BlockSpec((1, rows, wd), lambda i, j: (i, j, 0))
    tab_map = lambda i, j: (0, jnp.where(i < n_lat_seg, j, t // tm), 0)
    full = [DIFF_W, DIFF_W, DIFF_W, MLA_QK_W, MLA_QK_W, MLA_V_W]
    out_specs = ([tok(wd) for wd in full] + [tok(2 * FOURIER_W, tm // 2)] * 2 + [tok(N_BRANCHES * d)])
    out_shape = ([jax.ShapeDtypeStruct((s, t, wd), BF16) for wd in full]
                 + [jax.ShapeDtypeStruct((s, t // 2, 2 * FOURIER_W), BF16)] * 2
                 + [jax.ShapeDtypeStruct((s, t, N_BRANCHES * d), BF16)])
    return pl.pallas_call(
        functools.partial(_proj_body,
                          qscale_d=DIFF_HD ** -0.5 * LOG2E,
                          qscale_m=(MLA_NOPE + MLA_ROPE) ** -0.5 * LOG2E),
        grid=(s, t // tm),
        in_specs=[tok(d), _mods_block(mods, l), _layer(norm_g, l),
                  _layer(w, l), _layer(wuq, l), _layer(wukv, l), _layer(qn, l), _layer(kvn, l),
                  _whole(cs128), pl.BlockSpec((4, tm, LANES), tab_map)],
        out_specs=out_specs,
        out_shape=out_shape,
        scratch_shapes=[pltpu.VMEM((2 * FOURIER_GROUPS, tm, LANES), F32)],
        compiler_params=_params(2),
        name="in_proj",
    )(x, mods, norm_g, w, wuq, wukv, qn, kvn, cs128, tab)


def _fill_kv(srcs_k, srcs_v, k_scr, v_scr, heads, vd):
    vw = v_scr.shape[1] // heads
    r0 = 0
    for k_ref, v_ref in zip(srcs_k, srcs_v):
        n = k_ref.shape[1]
        k_scr[r0:r0 + n, :] = k_ref[0]
        lane = lax.broadcasted_iota(jnp.int32, (n, vw - vd), 1)
        ones_col = jnp.where(lane == 0, 1.0, 0.0).astype(BF16)
        for h in range(heads):
            v_scr[r0:r0 + n, h * vw:h * vw + vd] = v_ref[0, :, h * vd:(h + 1) * vd]
            v_scr[r0:r0 + n, h * vw + vd:(h + 1) * vw] = ones_col
        r0 += n


def _softmax_pv(s, v_aug, vd):
    m = jnp.max(s, axis=-1, keepdims=True)
    e = jnp.exp2(s - m).astype(BF16)
    r = _dot(e, v_aug)
    return r[:, :vd], r[:, vd:vd + 1]


SMALL_UNIT_ROWS = 128


def _head_row_units(heads, rows):
    if rows <= 2 * SMALL_UNIT_ROWS:
        return [(h, 0, rows) for h in range(heads)]
    units = [(0, 0, SMALL_UNIT_ROWS), (0, SMALL_UNIT_ROWS, rows)]
    units += [(h, 0, rows) for h in range(1, heads - 1)]
    units += [(heads - 1, 0, rows - SMALL_UNIT_ROWS), (heads - 1, rows - SMALL_UNIT_ROWS, rows)]
    return units


def _qk(q, k):
    return lax.dot_general(q, k, (((1,), (1,)), ((), ())), preferred_element_type=F32)


def _diff_body(*refs, n_src, lam_init):
    q_ref, lam_ref, sg_ref = refs[0], refs[1], refs[2]
    k_srcs = refs[3:3 + n_src]
    v_srcs = refs[3 + n_src:3 + 2 * n_src]
    o_ref, k_scr, v_scr = refs[3 + 2 * n_src:]

    @pl.when(pl.program_id(1) == 0)
    def _():
        _fill_kv(k_srcs, v_srcs, k_scr, v_scr, DIFF_HEADS, DIFF_VD)

    lp = lam_ref[0]
    lam = (jnp.exp(jnp.sum(lp[0:1] * lp[1:2], axis=-1, keepdims=True))
           - jnp.exp(jnp.sum(lp[2:3] * lp[3:4], axis=-1, keepdims=True)) + lam_init)
    zero = jnp.zeros((), BF16)
    vw = v_scr.shape[1] // DIFF_HEADS
    for h, r0, r1 in _head_row_units(DIFF_HEADS, q_ref.shape[1]):
        sl = slice(h * LANES, (h + 1) * LANES)
        qh, kh, vh = q_ref[0, r0:r1, sl], k_scr[:, sl], v_scr[:, h * vw:(h + 1) * vw]
        first = lax.broadcasted_iota(jnp.int32, qh.shape, 1) < DIFF_HD
        o1, l1 = _softmax_pv(_qk(jnp.where(first, qh, zero), kh), vh, DIFF_VD)
        o2, l2 = _softmax_pv(_qk(jnp.where(first, zero, qh), kh), vh, DIFF_VD)
        o = o1 * (1.0 / l1) - o2 * (lam / l2)
        o_ref[0, r0:r1, sl] = (_rms(o, sg_ref[0]) * (1.0 - lam_init)).astype(BF16)


def _mla_body(*refs, n_src):
    q_ref = refs[0]
    k_srcs = refs[1:1 + n_src]
    v_srcs = refs[1 + n_src:1 + 2 * n_src]
    o_ref, k_scr, v_scr = refs[1 + 2 * n_src:]

    @pl.when(pl.program_id(1) == 0)
    def _():
        _fill_kv(k_srcs, v_srcs, k_scr, v_scr, MLA_HEADS, MLA_VD)

    vw = v_scr.shape[1] // MLA_HEADS
    for h, r0, r1 in _head_row_units(MLA_HEADS, q_ref.shape[1]):
        sl = slice(h * LANES, (h + 1) * LANES)
        o, l = _softmax_pv(_qk(q_ref[0, r0:r1, sl], k_scr[:, sl]), v_scr[:, h * vw:(h + 1) * vw], MLA_VD)
        o_ref[0, r0:r1, h * MLA_VD:(h + 1) * MLA_VD] = (o * (1.0 / l)).astype(BF16)


def _ctx_rows_spec(a, b, n_ctx):
    return pl.BlockSpec((1, n_ctx, a.shape[2]), lambda i, j: (b, i, 0))


def _attn_call(body, name, q, k, v, extra_specs, extra, heads, v_aug_w, out_w, b, n_lat, n_ctx, tq=1024):
    lat_rows = lambda a: pl.BlockSpec((1, n_lat, a.shape[2]), lambda i, j: (i, 0, 0))
    tq = _tile(n_lat, tq)
    return pl.pallas_call(
        functools.partial(body, n_src=2),
        grid=(b, n_lat // tq),
        in_specs=([pl.BlockSpec((1, tq, q.shape[2]), lambda i, j: (i, j, 0))] + extra_specs
                  + [_ctx_rows_spec(k, b, n_ctx), lat_rows(k), _ctx_rows_spec(v, b, n_ctx), lat_rows(v)]),
        out_specs=pl.BlockSpec((1, tq, out_w), lambda i, j: (i, j, 0)),
        out_shape=jax.ShapeDtypeStruct((b, n_lat, out_w), BF16),
        scratch_shapes=[pltpu.VMEM((n_ctx + n_lat, k.shape[2]), BF16),
                        pltpu.VMEM((n_ctx + n_lat, heads * v_aug_w), BF16)],
        compiler_params=_params(2),
        name=name,
    )(q, *extra, k, k, v, v)


def _ctx_attn_body(qd_ref, lam_ref, sg_ref, kd_ref, vd_ref, qm_ref, km_ref, vm_ref, od_ref, om_ref,
                   kd_scr, vd_scr, km_scr, vm_scr, *, lam_init):
    _diff_body(qd_ref, lam_ref, sg_ref, kd_ref, vd_ref, od_ref, kd_scr, vd_scr, n_src=1, lam_init=lam_init)
    _mla_body(qm_ref, km_ref, vm_ref, om_ref, km_scr, vm_scr, n_src=1)


def _ctx_attn_call(lam_init, diff_extra_specs, diff_extra, qd, kd, vd, qm, km, vm, b, n_ctx):
    rows = lambda a: _ctx_rows_spec(a, b, n_ctx)
    out = lambda w: pl.BlockSpec((1, n_ctx, w), lambda i, j: (i, 0, 0))
    return pl.pallas_call(
        functools.partial(_ctx_attn_body, lam_init=lam_init),
        grid=(b, 1),
        in_specs=[rows(qd)] + diff_extra_specs + [rows(kd), rows(vd), rows(qm), rows(km), rows(vm)],
        out_specs=[out(DIFF_W), out(MLA_V_W)],
        out_shape=[jax.ShapeDtypeStruct((b, n_ctx, DIFF_W), BF16),
                   jax.ShapeDtypeStruct((b, n_ctx, MLA_V_W), BF16)],
        scratch_shapes=[pltpu.VMEM((n_ctx, kd.shape[2]), BF16), pltpu.VMEM((n_ctx, DIFF_HEADS * MXU_W), BF16),
                        pltpu.VMEM((n_ctx, km.shape[2]), BF16), pltpu.VMEM((n_ctx, MLA_HEADS * LANES), BF16)],
        compiler_params=_params(2),
        name="ctx_attn",
    )(qd, *diff_extra, kd, vd, qm, km, vm)


def _dft_body(t_ref, zc_ref, zs_ref, o_ref):
    w = o_ref.shape[-1]
    even = _dot(t_ref[0], zc_ref[0, :, :w]) + _dot(t_ref[1], zs_ref[0, :, :w])
    odd = _dot(t_ref[2], zc_ref[0, :, w:]) + _dot(t_ref[3], zs_ref[0, :, w:])
    o_ref[0, 0] = (even + odd).astype(BF16)
    o_ref[0, 1] = (even - odd).astype(BF16)


def _dft_call(tables, zc, zs, b, latent, tm=1024):
    half, w = tables.shape[1], zc.shape[2] // 2
    t = 2 * half
    tm = _tile(half, tm)
    z_spec = pl.BlockSpec((1, half, 2 * w), (lambda i, j: (j, 0, 0)) if latent else (lambda i, j: (b, j, 0)))
    out = pl.pallas_call(
        _dft_body,
        grid=(half // tm, b),
        in_specs=[pl.BlockSpec((4, tm, half), lambda i, j: (0, i, 0)), z_spec, z_spec],
        out_specs=pl.BlockSpec((1, 2, tm, w), lambda i, j: (j, 0, i, 0)),
        out_shape=jax.ShapeDtypeStruct((b, 2, half, w), BF16),
        compiler_params=_params(2),
        name="pos_dft",
    )(tables, zc, zs)
    return out.reshape(b, t, w)


def _merge_body(x_ref, mod_ref, gt_ref, *rest, n_lat_seg):
    if n_lat_seg is None:
        (yd_ref, ym_ref, yf_ref), rest = rest[:3], rest[3:]
        yd, ym, yf = yd_ref[0], ym_ref[0], yf_ref[0]
    else:
        ys, rest = rest[:6], rest[6:]
        yd, ym, yf = (_pick(n_lat_seg, ys[2 * i], ys[2 * i + 1]) for i in range(N_BRANCHES))
    wd_ref, wm_ref, wf_ref, wo_ref, o_ref = rest
    x = x_ref[0]
    d = x.shape[-1]
    merged = (gt_ref[0, :, 0:d].astype(F32) * _dot(yd, wd_ref[0])
              + gt_ref[0, :, d:2 * d].astype(F32) * _dot(ym, wm_ref[0])
              + gt_ref[0, :, 2 * d:3 * d].astype(F32) * _dot(yf, wf_ref[0]))
    o_ref[0] = x + mod_ref[0, 0, 5:6, :] * _dot(merged.astype(BF16), wo_ref[0])


def _merge_call(x, mods, l, gt, y_lat, y_ctx, wd, wm, wf, wo, tm=512):
    b, t = y_lat[0].shape[:2]
    d = x.shape[2]
    tm = _tile(t, tm)
    tok = lambda wd_: pl.BlockSpec((1, tm, wd_), lambda i, j: (i, j, 0))
    if y_ctx is None:
        n_seg, y_specs, y_args = b, [tok(a.shape[2]) for a in y_lat], list(y_lat)
    else:
        n_seg, y_specs, y_args = b + 1, [], []
        for yl, yc in zip(y_lat, y_ctx):
            y_specs += list(_lat_or_ctx_specs(tm, yl.shape[2], b, t // tm))
            y_args += [yl, yc]
    return pl.pallas_call(
        functools.partial(_merge_body, n_lat_seg=None if y_ctx is None else b),
        grid=(n_seg, t // tm),
        in_specs=[tok(d), _mods_block(mods, l), tok(gt.shape[2])] + y_specs
                 + [_layer(wd, l), _layer(wm, l), _layer(wf, l), _layer(wo, l)],
        out_specs=tok(d),
        out_shape=jax.ShapeDtypeStruct((n_seg, t, d), F32),
        compiler_params=_params(2),
        name="merge_out",
    )(x, mods, gt, *y_args, wd, wm, wf, wo)


def _rope_tables(n_lat, n_ident):
    rows = np.arange(n_lat) // GRID_W
    cols = np.arange(n_lat) % GRID_W

    def cos_sin(dim):
        nf = dim // 4
        freqs = np.power(ROPE_BASE, -np.arange(nf, dtype=np.float64) / nf)
        ar = rows.astype(np.float64)[:, None] * freqs[None, :]
        ac = cols.astype(np.float64)[:, None] * freqs[None, :]
        ang = np.concatenate([ar, ar, ac, ac], axis=-1)
        sign = np.concatenate([-np.ones(nf), np.ones(nf), -np.ones(nf), np.ones(nf)])
        return np.cos(ang), np.sin(ang) * sign[None, :]

    cd, sd = cos_sin(DIFF_HD)
    cos_d = np.tile(cd, (1, LANES // DIFF_HD))
    sin_d = np.tile(sd, (1, LANES // DIFF_HD))
    cm, sm = cos_sin(MLA_ROPE)
    cos_m = np.ones((n_lat, LANES))
    sin_m = np.zeros((n_lat, LANES))
    cos_m[:, :MLA_ROPE] = cm
    sin_m[:, :MLA_ROPE] = sm
    lat = np.stack([cos_d, sin_d, cos_m, sin_m])
    ident = np.stack([np.ones((n_ident, LANES)), np.zeros((n_ident, LANES))] * 2)
    return jnp.asarray(np.concatenate([lat, ident], axis=1), F32)


def _angles(rows, cols, n):
    return 2.0 * np.pi * ((rows[:, None] * cols[None, :]) % n) / n


def _pos_dft_tables(n, group):
    j, m = np.arange(n // 2), np.arange(n // 2)
    scale = 1.0 / math.sqrt(n * group)
    ae, ao = _angles(j, 2 * m, n), _angles(j, 2 * m + 1, n)
    return jnp.asarray(np.stack([np.cos(ae), -np.sin(ae), np.cos(ao), -np.sin(ao)]) * scale, F32).astype(BF16)


def _chan_dft_table(group):
    a = _angles(np.arange(group), np.arange(group), group)
    return jnp.asarray(np.concatenate([np.cos(a), np.sin(a)], axis=1), F32).astype(BF16)


def _proj_weight_body(*refs):
    *w_refs, o_ref = refs
    blk = w_refs[0].shape[1]
    for k, w_ref in enumerate(w_refs):
        o_ref[0, :, k * blk:(k + 1) * blk] = w_ref[0].T.astype(BF16)


def _proj_weight_call(w_t, n_head, tail_start, blk=MXU_W, per_step=4):
    depth, cols, d = w_t.shape
    head_blocks, tail_blocks = n_head // blk, (cols - tail_start) // blk
    assert n_head % blk == 0 and (cols - tail_start) % blk == 0 and tail_start % 32 == 0
    assert (head_blocks + tail_blocks) % per_step == 0

    def window(k):
        def index(l, j):
            m = j * per_step + k
            return l, pl.multiple_of(jnp.where(m < head_blocks, m * blk, tail_start + (m - head_blocks) * blk), 32), 0
        return pl.BlockSpec((pl.Element(1), pl.Element(blk), pl.Element(d)), index)

    return pl.pallas_call(
        _proj_weight_body,
        grid=(depth, (head_blocks + tail_blocks) // per_step),
        in_specs=[window(k) for k in range(per_step)],
        out_specs=pl.BlockSpec((1, d, per_step * blk), lambda l, j: (l, 0, j)),
        out_shape=jax.ShapeDtypeStruct((depth, d, (head_blocks + tail_blocks) * blk), BF16),
        compiler_params=_params(2),
        name="proj_weight",
    )(*([w_t] * per_step))


def _prep_mla_weights(mla_w_uq, mla_w_ukv):
    depth = mla_w_uq.shape[0]
    tail = LANES - MLA_NOPE - MLA_ROPE
    uq = mla_w_uq.reshape(depth, MLA_Q_RANK, MLA_HEADS, MLA_NOPE + MLA_ROPE)
    uq = jnp.concatenate([uq[..., MLA_NOPE:], uq[..., :MLA_NOPE], jnp.zeros(uq.shape[:3] + (tail,), uq.dtype)], axis=-1)
    uq = uq.reshape(depth, MLA_Q_RANK, MLA_QK_W).astype(BF16)
    ukv = mla_w_ukv.reshape(depth, MLA_KV_RANK, MLA_HEADS, MLA_NOPE + MLA_VD)
    kn = jnp.pad(ukv[..., :MLA_NOPE], ((0, 0), (0, 0), (0, 0), (MLA_ROPE, tail)))
    kn = kn.reshape(depth, MLA_KV_RANK, MLA_QK_W)
    vv = ukv[..., MLA_NOPE:].reshape(depth, MLA_KV_RANK, MLA_V_W)
    return uq, jnp.concatenate([kn, vv], axis=2).astype(BF16)


def kernel(x, c, ctx, c_ctx, ada_w, ada_b, norm_g, ffn_w_in, ffn_w_out, w_in, diff_lambda,
           diff_subln_g, mla_q_norm_g, mla_w_uq, mla_kv_norm_g, mla_w_ukv, w_branch_diff,
           w_branch_mla, w_branch_fourier, w_out, final_norm_g):
    b, n_lat, d = x.shape
    n_ctx = ctx.shape[1]
    depth = ada_w.shape[0]
    assert b * n_ctx == n_lat, "context tokens of all batches must fill exactly one latent-length segment"

    cond_rows = -(-(b + 1) // 8) * 8
    cond = jnp.concatenate([c, c_ctx[None, :], jnp.zeros((cond_rows - b - 1, d), F32)], axis=0)
    mods = _ada_call(cond, ada_w, ada_b).reshape(depth, cond_rows, N_MOD, d)

    tab = _rope_tables(n_lat, PROJ_TILE)
    dft_x = _pos_dft_tables(n_lat, FOURIER_GROUP_DIM)
    dft_c = _pos_dft_tables(n_ctx, FOURIER_GROUP_DIM)
    cs128 = _chan_dft_table(FOURIER_GROUP_DIM)

    w_proj = _proj_weight_call(jnp.swapaxes(w_in, 1, 2), PROJ_WA_COLS, PROJ_KR_COL + MLA_ROPE)
    uq, ukv = _prep_mla_weights(mla_w_uq, mla_w_ukv)
    wd, wm, wf, wo = w_branch_diff, w_branch_mla, w_branch_fourier, w_out
    qn, kvn, sub_g = (a[:, None, :] for a in (mla_q_norm_g, mla_kv_norm_g, diff_subln_g))
    final_g = final_norm_g[None, :]
    flat = lambda a: a.reshape(1, b * n_ctx, a.shape[-1])

    h, h_ctx = x, ctx.reshape(1, b * n_ctx, d)
    for l in range(depth):
        last = l == depth - 1
        lam_init = 0.8 - 0.6 * math.exp(-0.3 * l)
        diff = functools.partial(_diff_body, lam_init=lam_init)
        diff_extra = ([_layer(diff_lambda, l), _layer(sub_g, l)], [diff_lambda, sub_g])
        sizes = (b, n_lat, n_ctx)

        h = _ffn_call(h, mods, l, norm_g, 0, ffn_w_in, ffn_w_out, 0, 0, ctx=h_ctx)
        h_ctx = None
        qd, kd, vd, qm, km, vm, zc, zs, gt = _proj_call(h, mods, l, norm_g, w_proj, uq, ukv, qn, kvn, cs128, tab, b)

        y_lat = (_attn_call(diff, "diff_attn", qd, kd, vd, *diff_extra, DIFF_HEADS, MXU_W, DIFF_W, *sizes),
                 _attn_call(_mla_body, "mla_attn", qm, km, vm, [], [], MLA_HEADS, LANES, MLA_V_W, *sizes),
                 _dft_call(dft_x, zc, zs, b, True))
        y_ctx = None
        if not last:
            yd_c, ym_c = _ctx_attn_call(lam_init, *diff_extra, qd, kd, vd, qm, km, vm, b, n_ctx)
            y_ctx = tuple(flat(a) for a in (yd_c, ym_c, _dft_call(dft_c, zc, zs, b, False)))
        h = _merge_call(h, mods, l, gt, y_lat, y_ctx, wd, wm, wf, wo)
        h = _ffn_call(h, mods, l, norm_g, 2, ffn_w_in, ffn_w_out, 1, 6, final_g=final_g if last else None)
    return h
```

```python
import functools
import math

import numpy as np
import jax
import jax.numpy as jnp
from jax import lax
from jax.experimental import pallas as pl
from jax.experimental.pallas import tpu as pltpu

F32 = jnp.float32
BF16 = jnp.bfloat16

GRID_W = 64
DIFF_HEADS = 4
DIFF_HD = 64
DIFF_VD = 2 * DIFF_HD
MLA_HEADS = 8
MLA_NOPE = 64
MLA_ROPE = 32
MLA_VD = 64
MLA_Q_RANK = 384
MLA_KV_RANK = 256
FOURIER_GROUPS = 4
FOURIER_GROUP_DIM = 128
N_BRANCHES = 3
ROPE_BASE = 10000.0
RMS_EPS = 1e-6
N_MOD = 9

DIFF_W = DIFF_HEADS * 2 * DIFF_HD
MLA_QK_W = MLA_HEADS * 128
MLA_V_W = MLA_HEADS * MLA_VD
FOURIER_W = FOURIER_GROUPS * FOURIER_GROUP_DIM

LANES = 128
MXU_W = 256
FFN_CHUNK = MXU_W
VMEM_LIMIT = 56 * 1024 * 1024
FFN_VMEM_LIMIT = 60 * 1024 * 1024
LOG2E = math.log2(math.e)
PROJ_TILE = 512
PROJ_KR_COL = 3 * DIFF_W + MLA_Q_RANK + MLA_KV_RANK
PROJ_WA_COLS = PROJ_KR_COL + 3 * LANES


def _tile(n, pref):
    t = min(n, pref)
    while n % t:
        t //= 2
    return t


def _fixed(block, idx):
    return pl.BlockSpec(block, lambda *_: idx, pipeline_mode=pl.Buffered(1))


def _whole(a):
    return _fixed(a.shape, (0,) * a.ndim)


def _layer(a, l):
    return _fixed((1,) + a.shape[1:], (l,) + (0,) * (a.ndim - 1))


def _params(n_axes, vmem_limit=VMEM_LIMIT):
    return pltpu.CompilerParams(dimension_semantics=("arbitrary",) * n_axes,
                                vmem_limit_bytes=vmem_limit)


def _norm_mod(x, g, shift, scale):
    y = x * lax.rsqrt(jnp.mean(x * x, axis=-1, keepdims=True) + RMS_EPS)
    return (y * (g * (1.0 + scale)) + shift).astype(BF16)


def _rms(x, g):
    return x * lax.rsqrt(jnp.mean(x * x, axis=-1, keepdims=True) + RMS_EPS) * g


def _dot(a, b):
    return jnp.dot(a, b, preferred_element_type=F32)


def _ada_body(s_ref, w_ref, b_ref, o_ref):
    s = s_ref[...]
    a = (s * jax.nn.sigmoid(s)).astype(BF16)
    o_ref[0] = _dot(a, w_ref[0].astype(BF16)) + b_ref[0]


def _ada_call(cond, ada_w, ada_b):
    depth, d, n = ada_w.shape
    rows = cond.shape[0]
    tn = _tile(n, 2304)
    return pl.pallas_call(
        _ada_body,
        grid=(depth, n // tn),
        in_specs=[pl.BlockSpec((rows, d), lambda l, j: (0, 0)),
                  pl.BlockSpec((1, d, tn), lambda l, j: (l, 0, j)),
                  pl.BlockSpec((1, 1, tn), lambda l, j: (l, 0, j))],
        out_specs=pl.BlockSpec((1, rows, tn), lambda l, j: (l, 0, j)),
        out_shape=jax.ShapeDtypeStruct((depth, rows, n), F32),
        compiler_params=_params(2),
        name="ada_mod",
    )(cond, ada_w, ada_b.reshape(depth, 1, n))


def _mods_block(mods, l):
    d = mods.shape[-1]
    return pl.BlockSpec((1, 1, N_MOD, d), lambda i, j: (l, i, 0, 0))


def _lat_or_ctx_specs(rows, width, n_lat_seg, n_tiles):
    lat = pl.BlockSpec((1, rows, width), lambda i, j: (jnp.minimum(i, n_lat_seg - 1),
                                                       jnp.where(i < n_lat_seg, j, n_tiles - 1), 0))
    ctx = pl.BlockSpec((1, rows, width), lambda i, j: (0, jnp.where(i < n_lat_seg, 0, j), 0))
    return lat, ctx


def _pick(n_lat_seg, lat_ref, ctx_ref):
    return jnp.where(pl.program_id(0) < n_lat_seg, lat_ref[0], ctx_ref[0])


def _ffn_body(x_ref, *rest, mod0, gi, final, n_lat_seg):
    if n_lat_seg is not None:
        x = _pick(n_lat_seg, x_ref, rest[0])
        rest = rest[1:]
    else:
        x = x_ref[0]
    mod_ref, g_ref, win_ref, wout_ref = rest[:4]
    if final:
        fg_ref, o_ref = rest[4:]
    else:
        (o_ref,) = rest[4:]
    d_ff = wout_ref.shape[2]
    xm = _norm_mod(x, g_ref[0, gi:gi + 1, :], mod_ref[0, 0, mod0:mod0 + 1, :], mod_ref[0, 0, mod0 + 1:mod0 + 2, :])
    acc = jnp.zeros(x.shape, F32)
    for c in range(d_ff // FFN_CHUNK):
        lo = c * FFN_CHUNK
        gate = _dot(xm, win_ref[0, 0, :, lo:lo + FFN_CHUNK])
        up = _dot(xm, win_ref[0, 0, :, d_ff + lo:d_ff + lo + FFN_CHUNK])
        act = (gate * jax.nn.sigmoid(gate) * up).astype(BF16)
        acc = acc + _dot(act, wout_ref[0, 0, lo:lo + FFN_CHUNK, :])
    y = x + (0.5 * mod_ref[0, 0, mod0 + 2:mod0 + 3, :]) * acc
    if final:
        y = _rms(y, fg_ref[...])
    o_ref[0] = y


def _ffn_call(x, mods, l, norm_g, gi, win, wout, which, mod0, ctx=None, final_g=None, tm=1024):
    s, t, d = x.shape
    tm = _tile(t, tm if ctx is None else tm // 2)
    final = final_g is not None
    n_seg = s + 1 if ctx is not None else s
    if ctx is not None:
        in_specs = list(_lat_or_ctx_specs(tm, d, s, t // tm))
        args = [x, ctx]
    else:
        in_specs = [pl.BlockSpec((1, tm, d), lambda i, j: (i, j, 0))]
        args = [x]
    in_specs += [_mods_block(mods, l), _layer(norm_g, l),
                 _fixed((1, 1) + win.shape[2:], (l, which, 0, 0)),
                 _fixed((1, 1) + wout.shape[2:], (l, which, 0, 0))]
    args += [mods, norm_g, win, wout]
    if final:
        in_specs.append(_whole(final_g))
        args.append(final_g)
    return pl.pallas_call(
        functools.partial(_ffn_body, mod0=mod0, gi=gi, final=final, n_lat_seg=s if ctx is not None else None),
        grid=(n_seg, t // tm),
        in_specs=in_specs,
        out_specs=pl.BlockSpec((1, tm, d), lambda i, j: (i, j, 0)),
        out_shape=jax.ShapeDtypeStruct((n_seg, t, d), F32),
        compiler_params=_params(2, FFN_VMEM_LIMIT),
        name="ffn",
    )(*args)


def _rope(x, cos, sin_signed, chunk):
    lane = lax.broadcasted_iota(jnp.int32, x.shape, 1)
    even = (lane % (2 * chunk)) < chunk
    partner = jnp.where(even, pltpu.roll(x, LANES - chunk, 1), pltpu.roll(x, chunk, 1))
    return x * cos + partner * sin_signed


def _proj_body(x_ref, mod_ref, g_ref, w_ref, wuq_ref, wukv_ref, qn_ref, kvn_ref, cs_ref, tab_ref,
               qd_ref, kd_ref, vd_ref, qm_ref, km_ref, vm_ref, zc_ref, zs_ref, gt_ref, z_scr,
               *, qscale_d, qscale_m):
    x = x_ref[0]
    xm = _norm_mod(x, g_ref[0, 1:2, :], mod_ref[0, 0, 3:4, :], mod_ref[0, 0, 4:5, :])
    cos_d, sin_d, cos_m, sin_m = tab_ref[0], tab_ref[1], tab_ref[2], tab_ref[3]

    def mm(part, c0, width):
        c0 += PROJ_WA_COLS * part
        return _dot(xm, w_ref[0, :, c0:c0 + width])

    HEAD, TAIL = 0, 1

    o_cq = 3 * DIFF_W
    o_ckv = o_cq + MLA_Q_RANK
    o_kr = o_ckv + MLA_KV_RANK
    cq = _rms(mm(HEAD, o_cq, MLA_Q_RANK), qn_ref[0]).astype(BF16)
    ckv = _rms(mm(HEAD, o_ckv, MLA_KV_RANK), kvn_ref[0]).astype(BF16)
    kr = mm(HEAD, o_kr, LANES)
    kr = jnp.where(lax.broadcasted_iota(jnp.int32, kr.shape, 1) < MLA_ROPE, kr, 0.0)
    kr = _rope(kr, cos_m, sin_m, MLA_ROPE // 4)
    qm = _dot(cq, wuq_ref[0])
    for h in range(MLA_HEADS):
        sl = slice(h * LANES, (h + 1) * LANES)
        qm_ref[0, :, sl] = (_rope(qm[:, sl], cos_m, sin_m, MLA_ROPE // 4) * qscale_m).astype(BF16)
    kv = _dot(ckv, wukv_ref[0])
    for h in range(MLA_HEADS):
        sl = slice(h * LANES, (h + 1) * LANES)
        km_ref[0, :, sl] = (kv[:, sl] + kr).astype(BF16)
    vm_ref[0] = kv[:, MLA_QK_W:].astype(BF16)

    q = mm(HEAD, 0, DIFF_W)
    for h in range(DIFF_HEADS):
        sl = slice(h * LANES, (h + 1) * LANES)
        qd_ref[0, :, sl] = (_rope(q[:, sl], cos_d, sin_d, DIFF_HD // 4) * qscale_d).astype(BF16)
    k = mm(HEAD, DIFF_W, DIFF_W)
    for h in range(DIFF_HEADS):
        sl = slice(h * LANES, (h + 1) * LANES)
        kd_ref[0, :, sl] = _rope(k[:, sl], cos_d, sin_d, DIFF_HD // 4).astype(BF16)

    d = x.shape[-1]
    for bi in range(N_BRANCHES):
        gt_ref[0, :, bi * d:(bi + 1) * d] = (0.5 * jnp.tanh(0.5 * mm(TAIL, FOURIER_W + bi * d, d)) + 0.5).astype(BF16)

    f = mm(TAIL, 0, FOURIER_W).astype(BF16)
    for gi in range(FOURIER_GROUPS):
        sl = slice(gi * LANES, (gi + 1) * LANES)
        z = _dot(f[:, sl], cs_ref[...])
        z_scr[gi] = z[:, :LANES]
        z_scr[FOURIER_GROUPS + gi] = z[:, LANES:]
    half = z_scr.shape[1] // 2
    for parity in range(2):
        for gi in range(FOURIER_GROUPS):
            sl = slice(parity * FOURIER_W + gi * LANES, parity * FOURIER_W + (gi + 1) * LANES)
            zc_ref[0, :, sl] = z_scr[gi, pl.ds(parity, half, stride=2), :].astype(BF16)
            zs_ref[0, :, sl] = z_scr[FOURIER_GROUPS + gi, pl.ds(parity, half, stride=2), :].astype(BF16)
    vd_ref[0] = mm(HEAD, 2 * DIFF_W, DIFF_W).astype(BF16)


def _proj_call(x, mods, l, norm_g, w, wuq, wukv, qn, kvn, cs128, tab, n_lat_seg, tm=PROJ_TILE):
    s, t, d = x.shape
    assert t % tm == 0 and tab.shape[1] == t + tm
    tok = lambda wd, rows=tm: pl.BlockSpec((1, rows, wd), lambda i, j: (i, j, 0))
    tab_map = lambda i, j: (0, jnp.where(i < n_lat_seg, j, t // tm), 0)
    full = [DIFF_W, DIFF_W, DIFF_W, MLA_QK_W, MLA_QK_W, MLA_V_W]
    out_specs = ([tok(wd) for wd in full] + [tok(2 * FOURIER_W, tm // 2)] * 2 + [tok(N_BRANCHES * d)])
    out_shape = ([jax.ShapeDtypeStruct((s, t, wd), BF16) for wd in full]
                 + [jax.ShapeDtypeStruct((s, t // 2, 2 * FOURIER_W), BF16)] * 2
                 + [jax.ShapeDtypeStruct((s, t, N_BRANCHES * d), BF16)])
    return pl.pallas_call(
        functools.partial(_proj_body,
                          qscale_d=DIFF_HD ** -0.5 * LOG2E,
                          qscale_m=(MLA_NOPE + MLA_ROPE) ** -0.5 * LOG2E),
        grid=(s, t // tm),
        in_specs=[tok(d), _mods_block(mods, l), _layer(norm_g, l),
                  _layer(w, l), _layer(wuq, l), _layer(wukv, l), _layer(qn, l), _layer(kvn, l),
                  _whole(cs128), pl.BlockSpec((4, tm, LANES), tab_map)],
        out_specs=out_specs,
        out_shape=out_shape,
        scratch_shapes=[pltpu.VMEM((2 * FOURIER_GROUPS, tm, LANES), F32)],
        compiler_params=_params(2),
        name="in_proj",
    )(x, mods, norm_g, w, wuq, wukv, qn, kvn, cs128, tab)


def _fill_kv(srcs_k, srcs_v, k_scr, v_scr, heads, vd):
    vw = v_scr.shape[1] // heads
    r0 = 0
    for k_ref, v_ref in zip(srcs_k, srcs_v):
        n = k_ref.shape[1]
        k_scr[r0:r0 + n, :] = k_ref[0]
        lane = lax.broadcasted_iota(jnp.int32, (n, vw - vd), 1)
        ones_col = jnp.where(lane == 0, 1.0, 0.0).astype(BF16)
        for h in range(heads):
            v_scr[r0:r0 + n, h * vw:h * vw + vd] = v_ref[0, :, h * vd:(h + 1) * vd]
            v_scr[r0:r0 + n, h * vw + vd:(h + 1) * vw] = ones_col
        r0 += n


def _softmax_pv(s, v_aug, vd):
    m = jnp.max(s, axis=-1, keepdims=True)
    e = jnp.exp2(s - m).astype(BF16)
    r = _dot(e, v_aug)
    return r[:, :vd], r[:, vd:vd + 1]


SMALL_UNIT_ROWS = 128


def _head_row_units(heads, rows):
    if rows <= 2 * SMALL_UNIT_ROWS:
        return [(h, 0, rows) for h in range(heads)]
    units = [(0, 0, SMALL_UNIT_ROWS), (0, SMALL_UNIT_ROWS, rows)]
    units += [(h, 0, rows) for h in range(1, heads - 1)]
    units += [(heads - 1, 0, rows - SMALL_UNIT_ROWS), (heads - 1, rows - SMALL_UNIT_ROWS, rows)]
    return units


def _qk(q, k):
    return lax.dot_general(q, k, (((1,), (1,)), ((), ())), preferred_element_type=F32)


def _diff_body(*refs, n_src, lam_init):
    q_ref, lam_ref, sg_ref = refs[0], refs[1], refs[2]
    k_srcs = refs[3:3 + n_src]
    v_srcs = refs[3 + n_src:3 + 2 * n_src]
    o_ref, k_scr, v_scr = refs[3 + 2 * n_src:]

    @pl.when(pl.program_id(1) == 0)
    def _():
        _fill_kv(k_srcs, v_srcs, k_scr, v_scr, DIFF_HEADS, DIFF_VD)

    lp = lam_ref[0]
    lam = (jnp.exp(jnp.sum(lp[0:1] * lp[1:2], axis=-1, keepdims=True))
           - jnp.exp(jnp.sum(lp[2:3] * lp[3:4], axis=-1, keepdims=True)) + lam_init)
    zero = jnp.zeros((), BF16)
    vw = v_scr.shape[1] // DIFF_HEADS
    for h, r0, r1 in _head_row_units(DIFF_HEADS, q_ref.shape[1]):
        sl = slice(h * LANES, (h + 1) * LANES)
        qh, kh, vh = q_ref[0, r0:r1, sl], k_scr[:, sl], v_scr[:, h * vw:(h + 1) * vw]
        first = lax.broadcasted_iota(jnp.int32, qh.shape, 1) < DIFF_HD
        o1, l1 = _softmax_pv(_qk(jnp.where(first, qh, zero), kh), vh, DIFF_VD)
        o2, l2 = _softmax_pv(_qk(jnp.where(first, zero, qh), kh), vh, DIFF_VD)
        o = o1 * (1.0 / l1) - o2 * (lam / l2)
        o_ref[0, r0:r1, sl] = (_rms(o, sg_ref[0]) * (1.0 - lam_init)).astype(BF16)


def _mla_body(*refs, n_src):
    q_ref = refs[0]
    k_srcs = refs[1:1 + n_src]
    v_srcs = refs[1 + n_src:1 + 2 * n_src]
    o_ref, k_scr, v_scr = refs[1 + 2 * n_src:]

    @pl.when(pl.program_id(1) == 0)
    def _():
        _fill_kv(k_srcs, v_srcs, k_scr, v_scr, MLA_HEADS, MLA_VD)

    vw = v_scr.shape[1] // MLA_HEADS
    for h, r0, r1 in _head_row_units(MLA_HEADS, q_ref.shape[1]):
        sl = slice(h * LANES, (h + 1) * LANES)
        o, l = _softmax_pv(_qk(q_ref[0, r0:r1, sl], k_scr[:, sl]), v_scr[:, h * vw:(h + 1) * vw], MLA_VD)
        o_ref[0, r0:r1, h * MLA_VD:(h + 1) * MLA_VD] = (o * (1.0 / l)).astype(BF16)


def _ctx_rows_spec(a, b, n_ctx):
    return pl.BlockSpec((1, n_ctx, a.shape[2]), lambda i, j: (b, i, 0))


def _attn_call(body, name, q, k, v, extra_specs, extra, heads, v_aug_w, out_w, b, n_lat, n_ctx, tq=1024):
    lat_rows = lambda a: pl.BlockSpec((1, n_lat, a.shape[2]), lambda i, j: (i, 0, 0))
    tq = _tile(n_lat, tq)
    return pl.pallas_call(
        functools.partial(body, n_src=2),
        grid=(b, n_lat // tq),
        in_specs=([pl.BlockSpec((1, tq, q.shape[2]), lambda i, j: (i, j, 0))] + extra_specs
                  + [_ctx_rows_spec(k, b, n_ctx), lat_rows(k), _ctx_rows_spec(v, b, n_ctx), lat_rows(v)]),
        out_specs=pl.BlockSpec((1, tq, out_w), lambda i, j: (i, j, 0)),
        out_shape=jax.ShapeDtypeStruct((b, n_lat, out_w), BF16),
        scratch_shapes=[pltpu.VMEM((n_ctx + n_lat, k.shape[2]), BF16),
                        pltpu.VMEM((n_ctx + n_lat, heads * v_aug_w), BF16)],
        compiler_params=_params(2),
        name=name,
    )(q, *extra, k, k, v, v)


def _ctx_attn_body(qd_ref, lam_ref, sg_ref, kd_ref, vd_ref, qm_ref, km_ref, vm_ref, od_ref, om_ref,
                   kd_scr, vd_scr, km_scr, vm_scr, *, lam_init, n_ctx):
    for i in range(qd_ref.shape[1] // n_ctx):
        rows = (slice(None), slice(i * n_ctx, (i + 1) * n_ctx), slice(None))
        _diff_body(qd_ref.at[rows], lam_ref, sg_ref, kd_ref.at[rows], vd_ref.at[rows], od_ref.at[rows],
                   kd_scr.at[i], vd_scr.at[i], n_src=1, lam_init=lam_init)
        _mla_body(qm_ref.at[rows], km_ref.at[rows], vm_ref.at[rows], om_ref.at[rows],
                  km_scr.at[i], vm_scr.at[i], n_src=1)


def _ctx_attn_call(lam_init, diff_extra_specs, diff_extra, qd, kd, vd, qm, km, vm, b, n_ctx):
    seg = lambda a: pl.BlockSpec((1, b * n_ctx, a.shape[2]), lambda i, j: (b, 0, 0))
    out = lambda w: pl.BlockSpec((1, b * n_ctx, w), lambda i, j: (0, 0, 0))
    return pl.pallas_call(
        functools.partial(_ctx_attn_body, lam_init=lam_init, n_ctx=n_ctx),
        grid=(1, 1),
        in_specs=[seg(qd)] + diff_extra_specs + [seg(kd), seg(vd), seg(qm), seg(km), seg(vm)],
        out_specs=[out(DIFF_W), out(MLA_V_W)],
        out_shape=[jax.ShapeDtypeStruct((1, b * n_ctx, DIFF_W), BF16),
                   jax.ShapeDtypeStruct((1, b * n_ctx, MLA_V_W), BF16)],
        scratch_shapes=[pltpu.VMEM((b, n_ctx, kd.shape[2]), BF16), pltpu.VMEM((b, n_ctx, DIFF_HEADS * MXU_W), BF16),
                        pltpu.VMEM((b, n_ctx, km.shape[2]), BF16), pltpu.VMEM((b, n_ctx, MLA_HEADS * LANES), BF16)],
        compiler_params=_params(2),
        name="ctx_attn",
    )(qd, *diff_extra, kd, vd, qm, km, vm)


def _dft_body(t_ref, zc_ref, zs_ref, o_ref):
    w = o_ref.shape[-1]
    even = _dot(t_ref[0], zc_ref[0, :, :w]) + _dot(t_ref[1], zs_ref[0, :, :w])
    odd = _dot(t_ref[2], zc_ref[0, :, w:]) + _dot(t_ref[3], zs_ref[0, :, w:])
    o_ref[0, 0] = (even + odd).astype(BF16)
    o_ref[0, 1] = (even - odd).astype(BF16)


def _dft_call(tables, zc, zs, b, latent, tm=1024):
    half, w = tables.shape[1], zc.shape[2] // 2
    t = 2 * half
    tm = _tile(half, tm)
    z_spec = pl.BlockSpec((1, half, 2 * w), (lambda i, j: (j, 0, 0)) if latent else (lambda i, j: (b, j, 0)))
    out = pl.pallas_call(
        _dft_body,
        grid=(half // tm, b),
        in_specs=[pl.BlockSpec((4, tm, half), lambda i, j: (0, i, 0)), z_spec, z_spec],
        out_specs=pl.BlockSpec((1, 2, tm, w), lambda i, j: (j, 0, i, 0)),
        out_shape=jax.ShapeDtypeStruct((b, 2, half, w), BF16),
        compiler_params=_params(2),
        name="pos_dft",
    )(tables, zc, zs)
    return out.reshape(b, t, w)


def _merge_body(x_ref, mod_ref, gt_ref, *rest, n_lat_seg):
    if n_lat_seg is None:
        (yd_ref, ym_ref, yf_ref), rest = rest[:3], rest[3:]
        yd, ym, yf = yd_ref[0], ym_ref[0], yf_ref[0]
    else:
        ys, rest = rest[:6], rest[6:]
        yd, ym, yf = (_pick(n_lat_seg, ys[2 * i], ys[2 * i + 1]) for i in range(N_BRANCHES))
    wd_ref, wm_ref, wf_ref, wo_ref, o_ref = rest
    x = x_ref[0]
    d = x.shape[-1]
    merged = (gt_ref[0, :, 0:d].astype(F32) * _dot(yd, wd_ref[0])
              + gt_ref[0, :, d:2 * d].astype(F32) * _dot(ym, wm_ref[0])
              + gt_ref[0, :, 2 * d:3 * d].astype(F32) * _dot(yf, wf_ref[0]))
    o_ref[0] = x + mod_ref[0, 0, 5:6, :] * _dot(merged.astype(BF16), wo_ref[0])


def _merge_call(x, mods, l, gt, y_lat, y_ctx, wd, wm, wf, wo, tm=512):
    b, t = y_lat[0].shape[:2]
    d = x.shape[2]
    tm = _tile(t, tm)
    tok = lambda wd_: pl.BlockSpec((1, tm, wd_), lambda i, j: (i, j, 0))
    if y_ctx is None:
        n_seg, y_specs, y_args = b, [tok(a.shape[2]) for a in y_lat], list(y_lat)
    else:
        n_seg, y_specs, y_args = b + 1, [], []
        for yl, yc in zip(y_lat, y_ctx):
            y_specs += list(_lat_or_ctx_specs(tm, yl.shape[2], b, t // tm))
            y_args += [yl, yc]
    return pl.pallas_call(
        functools.partial(_merge_body, n_lat_seg=None if y_ctx is None else b),
        grid=(n_seg, t // tm),
        in_specs=[tok(d), _mods_block(mods, l), tok(gt.shape[2])] + y_specs
                 + [_layer(wd, l), _layer(wm, l), _layer(wf, l), _layer(wo, l)],
        out_specs=tok(d),
        out_shape=jax.ShapeDtypeStruct((n_seg, t, d), F32),
        compiler_params=_params(2),
        name="merge_out",
    )(x, mods, gt, *y_args, wd, wm, wf, wo)


def _rope_tables(n_lat, n_ident):
    rows = np.arange(n_lat) // GRID_W
    cols = np.arange(n_lat) % GRID_W

    def cos_sin(dim):
        nf = dim // 4
        freqs = np.power(ROPE_BASE, -np.arange(nf, dtype=np.float64) / nf)
        ar = rows.astype(np.float64)[:, None] * freqs[None, :]
        ac = cols.astype(np.float64)[:, None] * freqs[None, :]
        ang = np.concatenate([ar, ar, ac, ac], axis=-1)
        sign = np.concatenate([-np.ones(nf), np.ones(nf), -np.ones(nf), np.ones(nf)])
        return np.cos(ang), np.sin(ang) * sign[None, :]

    cd, sd = cos_sin(DIFF_HD)
    cos_d = np.tile(cd, (1, LANES // DIFF_HD))
    sin_d = np.tile(sd, (1, LANES // DIFF_HD))
    cm, sm = cos_sin(MLA_ROPE)
    cos_m = np.ones((n_lat, LANES))
    sin_m = np.zeros((n_lat, LANES))
    cos_m[:, :MLA_ROPE] = cm
    sin_m[:, :MLA_ROPE] = sm
    lat = np.stack([cos_d, sin_d, cos_m, sin_m])
    ident = np.stack([np.ones((n_ident, LANES)), np.zeros((n_ident, LANES))] * 2)
    return jnp.asarray(np.concatenate([lat, ident], axis=1), F32)


def _angles(rows, cols, n):
    return 2.0 * np.pi * ((rows[:, None] * cols[None, :]) % n) / n


def _pos_dft_tables(n, group):
    j, m = np.arange(n // 2), np.arange(n // 2)
    scale = 1.0 / math.sqrt(n * group)
    ae, ao = _angles(j, 2 * m, n), _angles(j, 2 * m + 1, n)
    return jnp.asarray(np.stack([np.cos(ae), -np.sin(ae), np.cos(ao), -np.sin(ao)]) * scale, F32).astype(BF16)


def _chan_dft_table(group):
    a = _angles(np.arange(group), np.arange(group), group)
    return jnp.asarray(np.concatenate([np.cos(a), np.sin(a)], axis=1), F32).astype(BF16)


def _proj_weight_body(*refs):
    *w_refs, o_ref = refs
    blk = w_refs[0].shape[1]
    for k, w_ref in enumerate(w_refs):
        o_ref[0, :, k * blk:(k + 1) * blk] = w_ref[0].T.astype(BF16)


def _proj_weight_call(w_t, n_head, tail_start, blk=MXU_W, per_step=4):
    depth, cols, d = w_t.shape
    head_blocks, tail_blocks = n_head // blk, (cols - tail_start) // blk
    assert n_head % blk == 0 and (cols - tail_start) % blk == 0 and tail_start % 32 == 0
    assert (head_blocks + tail_blocks) % per_step == 0

    def window(k):
        def index(l, j):
            m = j * per_step + k
            return l, pl.multiple_of(jnp.where(m < head_blocks, m * blk, tail_start + (m - head_blocks) * blk), 32), 0
        return pl.BlockSpec((pl.Element(1), pl.Element(blk), pl.Element(d)), index)

    return pl.pallas_call(
        _proj_weight_body,
        grid=(depth, (head_blocks + tail_blocks) // per_step),
        in_specs=[window(k) for k in range(per_step)],
        out_specs=pl.BlockSpec((1, d, per_step * blk), lambda l, j: (l, 0, j)),
        out_shape=jax.ShapeDtypeStruct((depth, d, (head_blocks + tail_blocks) * blk), BF16),
        compiler_params=_params(2),
        name="proj_weight",
    )(*([w_t] * per_step))


def _prep_mla_weights(mla_w_uq, mla_w_ukv):
    depth = mla_w_uq.shape[0]
    tail = LANES - MLA_NOPE - MLA_ROPE
    uq = mla_w_uq.reshape(depth, MLA_Q_RANK, MLA_HEADS, MLA_NOPE + MLA_ROPE)
    uq = jnp.concatenate([uq[..., MLA_NOPE:], uq[..., :MLA_NOPE], jnp.zeros(uq.shape[:3] + (tail,), uq.dtype)], axis=-1)
    uq = uq.reshape(depth, MLA_Q_RANK, MLA_QK_W).astype(BF16)
    ukv = mla_w_ukv.reshape(depth, MLA_KV_RANK, MLA_HEADS, MLA_NOPE + MLA_VD)
    kn = jnp.pad(ukv[..., :MLA_NOPE], ((0, 0), (0, 0), (0, 0), (MLA_ROPE, tail)))
    kn = kn.reshape(depth, MLA_KV_RANK, MLA_QK_W)
    vv = ukv[..., MLA_NOPE:].reshape(depth, MLA_KV_RANK, MLA_V_W)
    return uq, jnp.concatenate([kn, vv], axis=2).astype(BF16)


def kernel(x, c, ctx, c_ctx, ada_w, ada_b, norm_g, ffn_w_in, ffn_w_out, w_in, diff_lambda,
           diff_subln_g, mla_q_norm_g, mla_w_uq, mla_kv_norm_g, mla_w_ukv, w_branch_diff,
           w_branch_mla, w_branch_fourier, w_out, final_norm_g):
    b, n_lat, d = x.shape
    n_ctx = ctx.shape[1]
    depth = ada_w.shape[0]
    assert b * n_ctx == n_lat, "context tokens of all batches must fill exactly one latent-length segment"

    cond_rows = -(-(b + 1) // 8) * 8
    cond = jnp.concatenate([c, c_ctx[None, :], jnp.zeros((cond_rows - b - 1, d), F32)], axis=0)
    mods = _ada_call(cond, ada_w, ada_b).reshape(depth, cond_rows, N_MOD, d)

    tab = _rope_tables(n_lat, PROJ_TILE)
    dft_x = _pos_dft_tables(n_lat, FOURIER_GROUP_DIM)
    dft_c = _pos_dft_tables(n_ctx, FOURIER_GROUP_DIM)
    cs128 = _chan_dft_table(FOURIER_GROUP_DIM)

    w_proj = _proj_weight_call(jnp.swapaxes(w_in, 1, 2), PROJ_WA_COLS, PROJ_KR_COL + MLA_ROPE)
    uq, ukv = _prep_mla_weights(mla_w_uq, mla_w_ukv)
    wd, wm, wf, wo = w_branch_diff, w_branch_mla, w_branch_fourier, w_out
    qn, kvn, sub_g = (a[:, None, :] for a in (mla_q_norm_g, mla_kv_norm_g, diff_subln_g))
    final_g = final_norm_g[None, :]
    flat = lambda a: a.reshape(1, b * n_ctx, a.shape[-1])

    h, h_ctx = x, ctx.reshape(1, b * n_ctx, d)
    for l in range(depth):
        last = l == depth - 1
        lam_init = 0.8 - 0.6 * math.exp(-0.3 * l)
        diff = functools.partial(_diff_body, lam_init=lam_init)
        diff_extra = ([_layer(diff_lambda, l), _layer(sub_g, l)], [diff_lambda, sub_g])
        sizes = (b, n_lat, n_ctx)

        h = _ffn_call(h, mods, l, norm_g, 0, ffn_w_in, ffn_w_out, 0, 0, ctx=h_ctx)
        h_ctx = None
        qd, kd, vd, qm, km, vm, zc, zs, gt = _proj_call(h, mods, l, norm_g, w_proj, uq, ukv, qn, kvn, cs128, tab, b)

        y_lat = (_attn_call(diff, "diff_attn", qd, kd, vd, *diff_extra, DIFF_HEADS, MXU_W, DIFF_W, *sizes),
                 _attn_call(_mla_body, "mla_attn", qm, km, vm, [], [], MLA_HEADS, LANES, MLA_V_W, *sizes),
                 _dft_call(dft_x, zc, zs, b, True))
        y_ctx = None
        if not last:
            yd_c, ym_c = _ctx_attn_call(lam_init, *diff_extra, qd, kd, vd, qm, km, vm, b, n_ctx)
            y_ctx = tuple(flat(a) for a in (yd_c, ym_c, _dft_call(dft_c, zc, zs, b, False)))
        h = _merge_call(h, mods, l, gt, y_lat, y_ctx, wd, wm, wf, wo)
        h = _ffn_call(h, mods, l, norm_g, 2, ffn_w_in, ffn_w_out, 1, 6, final_g=final_g if last else None)
    return h
```

```python
import functools
import math

import numpy as np
import jax
import jax.numpy as jnp
from jax import lax
from jax.experimental import pallas as pl
from jax.experimental.pallas import tpu as pltpu

F32 = jnp.float32
BF16 = jnp.bfloat16

GRID_W = 64
DIFF_HEADS = 4
DIFF_HD = 64
DIFF_VD = 2 * DIFF_HD
MLA_HEADS = 8
MLA_NOPE = 64
MLA_ROPE = 32
MLA_VD = 64
MLA_Q_RANK = 384
MLA_KV_RANK = 256
FOURIER_GROUPS = 4
FOURIER_GROUP_DIM = 128
N_BRANCHES = 3
ROPE_BASE = 10000.0
RMS_EPS = 1e-6
N_MOD = 9

DIFF_W = DIFF_HEADS * 2 * DIFF_HD
MLA_QK_W = MLA_HEADS * 128
MLA_V_W = MLA_HEADS * MLA_VD
FOURIER_W = FOURIER_GROUPS * FOURIER_GROUP_DIM

LANES = 128
MXU_W = 256
FFN_CHUNK = MXU_W
VMEM_LIMIT = 56 * 1024 * 1024
FFN_VMEM_LIMIT = 60 * 1024 * 1024
LOG2E = math.log2(math.e)
PROJ_TILE = 512
PROJ_KR_COL = 3 * DIFF_W + MLA_Q_RANK + MLA_KV_RANK
PROJ_WA_COLS = PROJ_KR_COL + 3 * LANES


def _tile(n, pref):
    t = min(n, pref)
    while n % t:
        t //= 2
    return t


def _fixed(block, idx):
    return pl.BlockSpec(block, lambda *_: idx, pipeline_mode=pl.Buffered(1))


def _whole(a):
    return _fixed(a.shape, (0,) * a.ndim)


def _layer(a, l):
    return _fixed((1,) + a.shape[1:], (l,) + (0,) * (a.ndim - 1))


def _params(n_axes, vmem_limit=VMEM_LIMIT):
    return pltpu.CompilerParams(dimension_semantics=("arbitrary",) * n_axes,
                                vmem_limit_bytes=vmem_limit)


def _norm_mod(x, g, shift, scale):
    y = x * lax.rsqrt(jnp.mean(x * x, axis=-1, keepdims=True) + RMS_EPS)
    return (y * (g * (1.0 + scale)) + shift).astype(BF16)


def _rms(x, g):
    return x * lax.rsqrt(jnp.mean(x * x, axis=-1, keepdims=True) + RMS_EPS) * g


def _dot(a, b):
    return jnp.dot(a, b, preferred_element_type=F32)


def _ada_body(s_ref, w_ref, b_ref, o_ref):
    s = s_ref[...]
    a = (s * jax.nn.sigmoid(s)).astype(BF16)
    o_ref[0] = _dot(a, w_ref[0].astype(BF16)) + b_ref[0]


def _ada_call(cond, ada_w, ada_b):
    depth, d, n = ada_w.shape
    rows = cond.shape[0]
    tn = _tile(n, 2304)
    return pl.pallas_call(
        _ada_body,
        grid=(depth, n // tn),
        in_specs=[pl.BlockSpec((rows, d), lambda l, j: (0, 0)),
                  pl.BlockSpec((1, d, tn), lambda l, j: (l, 0, j)),
                  pl.BlockSpec((1, 1, tn), lambda l, j: (l, 0, j))],
        out_specs=pl.BlockSpec((1, rows, tn), lambda l, j: (l, 0, j)),
        out_shape=jax.ShapeDtypeStruct((depth, rows, n), F32),
        compiler_params=_params(2),
        name="ada_mod",
    )(cond, ada_w, ada_b.reshape(depth, 1, n))


def _mods_block(mods, l):
    d = mods.shape[-1]
    return pl.BlockSpec((1, 1, N_MOD, d), lambda i, j: (l, i, 0, 0))


def _lat_or_ctx_specs(rows, width, n_lat_seg, n_tiles):
    lat = pl.BlockSpec((1, rows, width), lambda i, j: (jnp.minimum(i, n_lat_seg - 1),
                                                       jnp.where(i < n_lat_seg, j, n_tiles - 1), 0))
    ctx = pl.BlockSpec((1, rows, width), lambda i, j: (0, jnp.where(i < n_lat_seg, 0, j), 0))
    return lat, ctx


def _pick(n_lat_seg, lat_ref, ctx_ref):
    return jnp.where(pl.program_id(0) < n_lat_seg, lat_ref[0], ctx_ref[0])


def _ffn_body(x_ref, *rest, mod0, gi, final, n_lat_seg):
    if n_lat_seg is not None:
        x = _pick(n_lat_seg, x_ref, rest[0])
        rest = rest[1:]
    else:
        x = x_ref[0]
    mod_ref, g_ref, win_ref, wout_ref = rest[:4]
    if final:
        fg_ref, o_ref = rest[4:]
    else:
        (o_ref,) = rest[4:]
    d_ff = wout_ref.shape[2]
    xm = _norm_mod(x, g_ref[0, gi:gi + 1, :], mod_ref[0, 0, mod0:mod0 + 1, :], mod_ref[0, 0, mod0 + 1:mod0 + 2, :])
    acc = jnp.zeros(x.shape, F32)
    for c in range(d_ff // FFN_CHUNK):
        lo = c * FFN_CHUNK
        gate = _dot(xm, win_ref[0, 0, :, lo:lo + FFN_CHUNK])
        up = _dot(xm, win_ref[0, 0, :, d_ff + lo:d_ff + lo + FFN_CHUNK])
        act = (gate * jax.nn.sigmoid(gate) * up).astype(BF16)
        acc = acc + _dot(act, wout_ref[0, 0, lo:lo + FFN_CHUNK, :])
    y = x + (0.5 * mod_ref[0, 0, mod0 + 2:mod0 + 3, :]) * acc
    if final:
        y = _rms(y, fg_ref[...])
    o_ref[0] = y


def _ffn_call(x, mods, l, norm_g, gi, win, wout, which, mod0, ctx=None, final_g=None, tm=1024):
    s, t, d = x.shape
    tm = _tile(t, tm if ctx is None else tm // 2)
    final = final_g is not None
    n_seg = s + 1 if ctx is not None else s
    if ctx is not None:
        in_specs = list(_lat_or_ctx_specs(tm, d, s, t // tm))
        args = [x, ctx]
    else:
        in_specs = [pl.BlockSpec((1, tm, d), lambda i, j: (i, j, 0))]
        args = [x]
    in_specs += [_mods_block(mods, l), _layer(norm_g, l),
                 _fixed((1, 1) + win.shape[2:], (l, which, 0, 0)),
                 _fixed((1, 1) + wout.shape[2:], (l, which, 0, 0))]
    args += [mods, norm_g, win, wout]
    if final:
        in_specs.append(_whole(final_g))
        args.append(final_g)
    return pl.pallas_call(
        functools.partial(_ffn_body, mod0=mod0, gi=gi, final=final, n_lat_seg=s if ctx is not None else None),
        grid=(n_seg, t // tm),
        in_specs=in_specs,
        out_specs=pl.BlockSpec((1, tm, d), lambda i, j: (i, j, 0)),
        out_shape=jax.ShapeDtypeStruct((n_seg, t, d), F32),
        compiler_params=_params(2, FFN_VMEM_LIMIT),
        name="ffn",
    )(*args)


def _rope(x, cos, sin_signed, chunk):
    lane = lax.broadcasted_iota(jnp.int32, x.shape, 1)
    even = (lane % (2 * chunk)) < chunk
    partner = jnp.where(even, pltpu.roll(x, LANES - chunk, 1), pltpu.roll(x, chunk, 1))
    return x * cos + partner * sin_signed


def _proj_body(x_ref, mod_ref, g_ref, w_ref, wuq_ref, wukv_ref, qn_ref, kvn_ref, cs_ref, tab_ref,
               qd_ref, kd_ref, vd_ref, qm_ref, km_ref, vm_ref, zc_ref, zs_ref, gt_ref, z_scr,
               *, qscale_d, qscale_m):
    x = x_ref[0]
    xm = _norm_mod(x, g_ref[0, 1:2, :], mod_ref[0, 0, 3:4, :], mod_ref[0, 0, 4:5, :])
    cos_d, sin_d, cos_m, sin_m = tab_ref[0], tab_ref[1], tab_ref[2], tab_ref[3]

    def mm(part, c0, width):
        c0 += PROJ_WA_COLS * part
        return _dot(xm, w_ref[0, :, c0:c0 + width])

    HEAD, TAIL = 0, 1

    o_cq = 3 * DIFF_W
    o_ckv = o_cq + MLA_Q_RANK
    o_kr = o_ckv + MLA_KV_RANK
    cq = _rms(mm(HEAD, o_cq, MLA_Q_RANK), qn_ref[0]).astype(BF16)
    ckv = _rms(mm(HEAD, o_ckv, MLA_KV_RANK), kvn_ref[0]).astype(BF16)
    kr = mm(HEAD, o_kr, LANES)
    kr = jnp.where(lax.broadcasted_iota(jnp.int32, kr.shape, 1) < MLA_ROPE, kr, 0.0)
    kr = _rope(kr, cos_m, sin_m, MLA_ROPE // 4)
    qm = _dot(cq, wuq_ref[0])
    for h in range(MLA_HEADS):
        sl = slice(h * LANES, (h + 1) * LANES)
        qm_ref[0, :, sl] = (_rope(qm[:, sl], cos_m, sin_m, MLA_ROPE // 4) * qscale_m).astype(BF16)
    kv = _dot(ckv, wukv_ref[0])
    for h in range(MLA_HEADS):
        sl = slice(h * LANES, (h + 1) * LANES)
        km_ref[0, :, sl] = (kv[:, sl] + kr).astype(BF16)
    vm_ref[0] = kv[:, MLA_QK_W:].astype(BF16)

    q = mm(HEAD, 0, DIFF_W)
    for h in range(DIFF_HEADS):
        sl = slice(h * LANES, (h + 1) * LANES)
        qd_ref[0, :, sl] = (_rope(q[:, sl], cos_d, sin_d, DIFF_HD // 4) * qscale_d).astype(BF16)
    k = mm(HEAD, DIFF_W, DIFF_W)
    for h in range(DIFF_HEADS):
        sl = slice(h * LANES, (h + 1) * LANES)
        kd_ref[0, :, sl] = _rope(k[:, sl], cos_d, sin_d, DIFF_HD // 4).astype(BF16)

    d = x.shape[-1]
    for bi in range(N_BRANCHES):
        gt_ref[0, :, bi * d:(bi + 1) * d] = (0.5 * jnp.tanh(0.5 * mm(TAIL, FOURIER_W + bi * d, d)) + 0.5).astype(BF16)

    f = mm(TAIL, 0, FOURIER_W).astype(BF16)
    for gi in range(FOURIER_GROUPS):
        sl = slice(gi * LANES, (gi + 1) * LANES)
        z = _dot(f[:, sl], cs_ref[...])
        z_scr[gi] = z[:, :LANES]
        z_scr[FOURIER_GROUPS + gi] = z[:, LANES:]
    half = z_scr.shape[1] // 2
    for parity in range(2):
        for gi in range(FOURIER_GROUPS):
            sl = slice(parity * FOURIER_W + gi * LANES, parity * FOURIER_W + (gi + 1) * LANES)
            zc_ref[0, :, sl] = z_scr[gi, pl.ds(parity, half, stride=2), :].astype(BF16)
            zs_ref[0, :, sl] = z_scr[FOURIER_GROUPS + gi, pl.ds(parity, half, stride=2), :].astype(BF16)
    vd_ref[0] = mm(HEAD, 2 * DIFF_W, DIFF_W).astype(BF16)


def _proj_call(x, mods, l, norm_g, w, wuq, wukv, qn, kvn, cs128, tab, n_lat_seg, tm=PROJ_TILE):
    s, t, d = x.shape
    assert t % tm == 0 and tab.shape[1] == t + tm
    tok = lambda wd, rows=tm: pl.BlockSpec((1, rows, wd), lambda i, j: (i, j, 0))
    tab_map = lambda i, j: (0, jnp.where(i < n_lat_seg, j, t // tm), 0)
    full = [DIFF_W, DIFF_W, DIFF_W, MLA_QK_W, MLA_QK_W, MLA_V_W]
    out_specs = ([tok(wd) for wd in full] + [tok(2 * FOURIER_W, tm // 2)] * 2 + [tok(N_BRANCHES * d)])
    out_shape = ([jax.ShapeDtypeStruct((s, t, wd), BF16) for wd in full]
                 + [jax.ShapeDtypeStruct((s, t // 2, 2 * FOURIER_W), BF16)] * 2
                 + [jax.ShapeDtypeStruct((s, t, N_BRANCHES * d), BF16)])
    return pl.pallas_call(
        functools.partial(_proj_body,
                          qscale_d=DIFF_HD ** -0.5 * LOG2E,
                          qscale_m=(MLA_NOPE + MLA_ROPE) ** -0.5 * LOG2E),
        grid=(s, t // tm),
        in_specs=[tok(d), _mods_block(mods, l), _layer(norm_g, l),
                  _layer(w, l), _layer(wuq, l), _layer(wukv, l), _layer(qn, l), _layer(kvn, l),
                  _whole(cs128), pl.BlockSpec((4, tm, LANES), tab_map)],
        out_specs=out_specs,
        out_shape=out_shape,
        scratch_shapes=[pltpu.VMEM((2 * FOURIER_GROUPS, tm, LANES), F32)],
        compiler_params=_params(2),
        name="in_proj",
    )(x, mods, norm_g, w, wuq, wukv, qn, kvn, cs128, tab)


def _fill_kv(srcs_k, srcs_v, k_scr, v_scr, heads, vd):
    vw = v_scr.shape[1] // heads
    r0 = 0
    for k_ref, v_ref in zip(srcs_k, srcs_v):
        n = k_ref.shape[1]
        k_scr[r0:r0 + n, :] = k_ref[0]
        lane = lax.broadcasted_iota(jnp.int32, (n, vw - vd), 1)
        ones_col = jnp.where(lane == 0, 1.0, 0.0).astype(BF16)
        for h in range(heads):
            v_scr[r0:r0 + n, h * vw:h * vw + vd] = v_ref[0, :, h * vd:(h + 1) * vd]
            v_scr[r0:r0 + n, h * vw + vd:(h + 1) * vw] = ones_col
        r0 += n


def _softmax_pv(s, v_aug, vd):
    m = jnp.max(s, axis=-1, keepdims=True)
    e = jnp.exp2(s - m).astype(BF16)
    r = _dot(e, v_aug)
    return r[:, :vd], r[:, vd:vd + 1]


SMALL_UNIT_ROWS = 128


def _head_row_units(heads, rows):
    if rows <= 2 * SMALL_UNIT_ROWS:
        return [(h, 0, rows) for h in range(heads)]
    units = [(0, 0, SMALL_UNIT_ROWS), (0, SMALL_UNIT_ROWS, rows)]
    units += [(h, 0, rows) for h in range(1, heads - 1)]
    units += [(heads - 1, 0, rows - SMALL_UNIT_ROWS), (heads - 1, rows - SMALL_UNIT_ROWS, rows)]
    return units


def _scores_one_ahead(items, scores, consume):
    ahead = scores(items[0])
    for i, item in enumerate(items):
        s = ahead
        if i + 1 < len(items):
            ahead = scores(items[i + 1])
        consume(item, s)


def _qk(q, k):
    return lax.dot_general(q, k, (((1,), (1,)), ((), ())), preferred_element_type=F32)


def _diff_body(*refs, n_src, lam_init):
    q_ref, lam_ref, sg_ref = refs[0], refs[1], refs[2]
    k_srcs = refs[3:3 + n_src]
    v_srcs = refs[3 + n_src:3 + 2 * n_src]
    o_ref, k_scr, v_scr = refs[3 + 2 * n_src:]

    @pl.when(pl.program_id(1) == 0)
    def _():
        _fill_kv(k_srcs, v_srcs, k_scr, v_scr, DIFF_HEADS, DIFF_VD)

    lp = lam_ref[0]
    lam = (jnp.exp(jnp.sum(lp[0:1] * lp[1:2], axis=-1, keepdims=True))
           - jnp.exp(jnp.sum(lp[2:3] * lp[3:4], axis=-1, keepdims=True)) + lam_init)
    zero = jnp.zeros((), BF16)
    vw = v_scr.shape[1] // DIFF_HEADS
    maps = [(h, r0, r1, which) for h, r0, r1 in _head_row_units(DIFF_HEADS, q_ref.shape[1]) for which in (0, 1)]

    def scores(item):
        h, r0, r1, which = item
        sl = slice(h * LANES, (h + 1) * LANES)
        qh = q_ref[0, r0:r1, sl]
        first = lax.broadcasted_iota(jnp.int32, qh.shape, 1) < DIFF_HD
        return _qk(jnp.where(first, qh, zero) if which == 0 else jnp.where(first, zero, qh), k_scr[:, sl])

    partial_out = {}

    def consume(item, s):
        h, r0, r1, which = item
        partial_out[which] = _softmax_pv(s, v_scr[:, h * vw:(h + 1) * vw], DIFF_VD)
        if which == 1:
            (o1, l1), (o2, l2) = partial_out[0], partial_out[1]
            o = o1 * (1.0 / l1) - o2 * (lam / l2)
            o_ref[0, r0:r1, h * LANES:(h + 1) * LANES] = (_rms(o, sg_ref[0]) * (1.0 - lam_init)).astype(BF16)

    _scores_one_ahead(maps, scores, consume)


def _mla_body(*refs, n_src):
    q_ref = refs[0]
    k_srcs = refs[1:1 + n_src]
    v_srcs = refs[1 + n_src:1 + 2 * n_src]
    o_ref, k_scr, v_scr = refs[1 + 2 * n_src:]

    @pl.when(pl.program_id(1) == 0)
    def _():
        _fill_kv(k_srcs, v_srcs, k_scr, v_scr, MLA_HEADS, MLA_VD)

    vw = v_scr.shape[1] // MLA_HEADS

    def scores(item):
        h, r0, r1 = item
        sl = slice(h * LANES, (h + 1) * LANES)
        return _qk(q_ref[0, r0:r1, sl], k_scr[:, sl])

    def consume(item, s):
        h, r0, r1 = item
        o, l = _softmax_pv(s, v_scr[:, h * vw:(h + 1) * vw], MLA_VD)
        o_ref[0, r0:r1, h * MLA_VD:(h + 1) * MLA_VD] = (o * (1.0 / l)).astype(BF16)

    _scores_one_ahead(_head_row_units(MLA_HEADS, q_ref.shape[1]), scores, consume)


def _ctx_rows_spec(a, b, n_ctx):
    return pl.BlockSpec((1, n_ctx, a.shape[2]), lambda i, j: (b, i, 0))


def _attn_call(body, name, q, k, v, extra_specs, extra, heads, v_aug_w, out_w, b, n_lat, n_ctx, tq=1024):
    lat_rows = lambda a: pl.BlockSpec((1, n_lat, a.shape[2]), lambda i, j: (i, 0, 0))
    tq = _tile(n_lat, tq)
    return pl.pallas_call(
        functools.partial(body, n_src=2),
        grid=(b, n_lat // tq),
        in_specs=([pl.BlockSpec((1, tq, q.shape[2]), lambda i, j: (i, j, 0))] + extra_specs
                  + [_ctx_rows_spec(k, b, n_ctx), lat_rows(k), _ctx_rows_spec(v, b, n_ctx), lat_rows(v)]),
        out_specs=pl.BlockSpec((1, tq, out_w), lambda i, j: (i, j, 0)),
        out_shape=jax.ShapeDtypeStruct((b, n_lat, out_w), BF16),
        scratch_shapes=[pltpu.VMEM((n_ctx + n_lat, k.shape[2]), BF16),
                        pltpu.VMEM((n_ctx + n_lat, heads * v_aug_w), BF16)],
        compiler_params=_params(2),
        name=name,
    )(q, *extra, k, k, v, v)


def _ctx_attn_body(qd_ref, lam_ref, sg_ref, kd_ref, vd_ref, qm_ref, km_ref, vm_ref, od_ref, om_ref,
                   kd_scr, vd_scr, km_scr, vm_scr, *, lam_init, n_ctx):
    for i in range(qd_ref.shape[1] // n_ctx):
        rows = (slice(None), slice(i * n_ctx, (i + 1) * n_ctx), slice(None))
        _diff_body(qd_ref.at[rows], lam_ref, sg_ref, kd_ref.at[rows], vd_ref.at[rows], od_ref.at[rows],
                   kd_scr.at[i], vd_scr.at[i], n_src=1, lam_init=lam_init)
        _mla_body(qm_ref.at[rows], km_ref.at[rows], vm_ref.at[rows], om_ref.at[rows],
                  km_scr.at[i], vm_scr.at[i], n_src=1)


def _ctx_attn_call(lam_init, diff_extra_specs, diff_extra, qd, kd, vd, qm, km, vm, b, n_ctx):
    seg = lambda a: pl.BlockSpec((1, b * n_ctx, a.shape[2]), lambda i, j: (b, 0, 0))
    out = lambda w: pl.BlockSpec((1, b * n_ctx, w), lambda i, j: (0, 0, 0))
    return pl.pallas_call(
        functools.partial(_ctx_attn_body, lam_init=lam_init, n_ctx=n_ctx),
        grid=(1, 1),
        in_specs=[seg(qd)] + diff_extra_specs + [seg(kd), seg(vd), seg(qm), seg(km), seg(vm)],
        out_specs=[out(DIFF_W), out(MLA_V_W)],
        out_shape=[jax.ShapeDtypeStruct((1, b * n_ctx, DIFF_W), BF16),
                   jax.ShapeDtypeStruct((1, b * n_ctx, MLA_V_W), BF16)],
        scratch_shapes=[pltpu.VMEM((b, n_ctx, kd.shape[2]), BF16), pltpu.VMEM((b, n_ctx, DIFF_HEADS * MXU_W), BF16),
                        pltpu.VMEM((b, n_ctx, km.shape[2]), BF16), pltpu.VMEM((b, n_ctx, MLA_HEADS * LANES), BF16)],
        compiler_params=_params(2),
        name="ctx_attn",
    )(qd, *diff_extra, kd, vd, qm, km, vm)


def _dft_body(t_ref, zc_ref, zs_ref, o_ref):
    w = o_ref.shape[-1]
    even = _dot(t_ref[0], zc_ref[0, :, :w]) + _dot(t_ref[1], zs_ref[0, :, :w])
    odd = _dot(t_ref[2], zc_ref[0, :, w:]) + _dot(t_ref[3], zs_ref[0, :, w:])
    o_ref[0, 0] = (even + odd).astype(BF16)
    o_ref[0, 1] = (even - odd).astype(BF16)


def _dft_call(tables, zc, zs, b, latent, tm=1024):
    half, w = tables.shape[1], zc.shape[2] // 2
    t = 2 * half
    tm = _tile(half, tm)
    z_spec = pl.BlockSpec((1, half, 2 * w), (lambda i, j: (j, 0, 0)) if latent else (lambda i, j: (b, j, 0)))
    out = pl.pallas_call(
        _dft_body,
        grid=(half // tm, b),
        in_specs=[pl.BlockSpec((4, tm, half), lambda i, j: (0, i, 0)), z_spec, z_spec],
        out_specs=pl.BlockSpec((1, 2, tm, w), lambda i, j: (j, 0, i, 0)),
        out_shape=jax.ShapeDtypeStruct((b, 2, half, w), BF16),
        compiler_params=_params(2),
        name="pos_dft",
    )(tables, zc, zs)
    return out.reshape(b, t, w)


def _merge_body(x_ref, mod_ref, gt_ref, *rest, n_lat_seg):
    if n_lat_seg is None:
        (yd_ref, ym_ref, yf_ref), rest = rest[:3], rest[3:]
        yd, ym, yf = yd_ref[0], ym_ref[0], yf_ref[0]
    else:
        ys, rest = rest[:6], rest[6:]
        yd, ym, yf = (_pick(n_lat_seg, ys[2 * i], ys[2 * i + 1]) for i in range(N_BRANCHES))
    wd_ref, wm_ref, wf_ref, wo_ref, o_ref = rest
    x = x_ref[0]
    d = x.shape[-1]
    merged = (gt_ref[0, :, 0:d].astype(F32) * _dot(yd, wd_ref[0])
              + gt_ref[0, :, d:2 * d].astype(F32) * _dot(ym, wm_ref[0])
              + gt_ref[0, :, 2 * d:3 * d].astype(F32) * _dot(yf, wf_ref[0]))
    o_ref[0] = x + mod_ref[0, 0, 5:6, :] * _dot(merged.astype(BF16), wo_ref[0])


def _merge_call(x, mods, l, gt, y_lat, y_ctx, wd, wm, wf, wo, tm=512):
    b, t = y_lat[0].shape[:2]
    d = x.shape[2]
    tm = _tile(t, tm)
    tok = lambda wd_: pl.BlockSpec((1, tm, wd_), lambda i, j: (i, j, 0))
    if y_ctx is None:
        n_seg, y_specs, y_args = b, [tok(a.shape[2]) for a in y_lat], list(y_lat)
    else:
        n_seg, y_specs, y_args = b + 1, [], []
        for yl, yc in zip(y_lat, y_ctx):
            y_specs += list(_lat_or_ctx_specs(tm, yl.shape[2], b, t // tm))
            y_args += [yl, yc]
    return pl.pallas_call(
        functools.partial(_merge_body, n_lat_seg=None if y_ctx is None else b),
        grid=(n_seg, t // tm),
        in_specs=[tok(d), _mods_block(mods, l), tok(gt.shape[2])] + y_specs
                 + [_layer(wd, l), _layer(wm, l), _layer(wf, l), _layer(wo, l)],
        out_specs=tok(d),
        out_shape=jax.ShapeDtypeStruct((n_seg, t, d), F32),
        compiler_params=_params(2),
        name="merge_out",
    )(x, mods, gt, *y_args, wd, wm, wf, wo)


def _rope_tables(n_lat, n_ident):
    rows = np.arange(n_lat) // GRID_W
    cols = np.arange(n_lat) % GRID_W

    def cos_sin(dim):
        nf = dim // 4
        freqs = np.power(ROPE_BASE, -np.arange(nf, dtype=np.float64) / nf)
        ar = rows.astype(np.float64)[:, None] * freqs[None, :]
        ac = cols.astype(np.float64)[:, None] * freqs[None, :]
        ang = np.concatenate([ar, ar, ac, ac], axis=-1)
        sign = np.concatenate([-np.ones(nf), np.ones(nf), -np.ones(nf), np.ones(nf)])
        return np.cos(ang), np.sin(ang) * sign[None, :]

    cd, sd = cos_sin(DIFF_HD)
    cos_d = np.tile(cd, (1, LANES // DIFF_HD))
    sin_d = np.tile(sd, (1, LANES // DIFF_HD))
    cm, sm = cos_sin(MLA_ROPE)
    cos_m = np.ones((n_lat, LANES))
    sin_m = np.zeros((n_lat, LANES))
    cos_m[:, :MLA_ROPE] = cm
    sin_m[:, :MLA_ROPE] = sm
    lat = np.stack([cos_d, sin_d, cos_m, sin_m])
    ident = np.stack([np.ones((n_ident, LANES)), np.zeros((n_ident, LANES))] * 2)
    return jnp.asarray(np.concatenate([lat, ident], axis=1), F32)


def _angles(rows, cols, n):
    return 2.0 * np.pi * ((rows[:, None] * cols[None, :]) % n) / n


def _pos_dft_tables(n, group):
    j, m = np.arange(n // 2), np.arange(n // 2)
    scale = 1.0 / math.sqrt(n * group)
    ae, ao = _angles(j, 2 * m, n), _angles(j, 2 * m + 1, n)
    return jnp.asarray(np.stack([np.cos(ae), -np.sin(ae), np.cos(ao), -np.sin(ao)]) * scale, F32).astype(BF16)


def _chan_dft_table(group):
    a = _angles(np.arange(group), np.arange(group), group)
    return jnp.asarray(np.concatenate([np.cos(a), np.sin(a)], axis=1), F32).astype(BF16)


def _proj_weight_body(*refs):
    *w_refs, o_ref = refs
    blk = w_refs[0].shape[1]
    for k, w_ref in enumerate(w_refs):
        o_ref[0, :, k * blk:(k + 1) * blk] = w_ref[0].T.astype(BF16)


def _proj_weight_call(w_t, n_head, tail_start, blk=MXU_W, per_step=4):
    depth, cols, d = w_t.shape
    head_blocks, tail_blocks = n_head // blk, (cols - tail_start) // blk
    assert n_head % blk == 0 and (cols - tail_start) % blk == 0 and tail_start % 32 == 0
    assert (head_blocks + tail_blocks) % per_step == 0

    def window(k):
        def index(l, j):
            m = j * per_step + k
            return l, pl.multiple_of(jnp.where(m < head_blocks, m * blk, tail_start + (m - head_blocks) * blk), 32), 0
        return pl.BlockSpec((pl.Element(1), pl.Element(blk), pl.Element(d)), index)

    return pl.pallas_call(
        _proj_weight_body,
        grid=(depth, (head_blocks + tail_blocks) // per_step),
        in_specs=[window(k) for k in range(per_step)],
        out_specs=pl.BlockSpec((1, d, per_step * blk), lambda l, j: (l, 0, j)),
        out_shape=jax.ShapeDtypeStruct((depth, d, (head_blocks + tail_blocks) * blk), BF16),
        compiler_params=_params(2),
        name="proj_weight",
    )(*([w_t] * per_step))


def _prep_mla_weights(mla_w_uq, mla_w_ukv):
    depth = mla_w_uq.shape[0]
    tail = LANES - MLA_NOPE - MLA_ROPE
    uq = mla_w_uq.reshape(depth, MLA_Q_RANK, MLA_HEADS, MLA_NOPE + MLA_ROPE)
    uq = jnp.concatenate([uq[..., MLA_NOPE:], uq[..., :MLA_NOPE], jnp.zeros(uq.shape[:3] + (tail,), uq.dtype)], axis=-1)
    uq = uq.reshape(depth, MLA_Q_RANK, MLA_QK_W).astype(BF16)
    ukv = mla_w_ukv.reshape(depth, MLA_KV_RANK, MLA_HEADS, MLA_NOPE + MLA_VD)
    kn = jnp.pad(ukv[..., :MLA_NOPE], ((0, 0), (0, 0), (0, 0), (MLA_ROPE, tail)))
    kn = kn.reshape(depth, MLA_KV_RANK, MLA_QK_W)
    vv = ukv[..., MLA_NOPE:].reshape(depth, MLA_KV_RANK, MLA_V_W)
    return uq, jnp.concatenate([kn, vv], axis=2).astype(BF16)


def kernel(x, c, ctx, c_ctx, ada_w, ada_b, norm_g, ffn_w_in, ffn_w_out, w_in, diff_lambda,
           diff_subln_g, mla_q_norm_g, mla_w_uq, mla_kv_norm_g, mla_w_ukv, w_branch_diff,
           w_branch_mla, w_branch_fourier, w_out, final_norm_g):
    b, n_lat, d = x.shape
    n_ctx = ctx.shape[1]
    depth = ada_w.shape[0]
    assert b * n_ctx == n_lat, "context tokens of all batches must fill exactly one latent-length segment"

    cond_rows = -(-(b + 1) // 8) * 8
    cond = jnp.concatenate([c, c_ctx[None, :], jnp.zeros((cond_rows - b - 1, d), F32)], axis=0)
    mods = _ada_call(cond, ada_w, ada_b).reshape(depth, cond_rows, N_MOD, d)

    tab = _rope_tables(n_lat, PROJ_TILE)
    dft_x = _pos_dft_tables(n_lat, FOURIER_GROUP_DIM)
    dft_c = _pos_dft_tables(n_ctx, FOURIER_GROUP_DIM)
    cs128 = _chan_dft_table(FOURIER_GROUP_DIM)

    w_proj = _proj_weight_call(jnp.swapaxes(w_in, 1, 2), PROJ_WA_COLS, PROJ_KR_COL + MLA_ROPE)
    uq, ukv = _prep_mla_weights(mla_w_uq, mla_w_ukv)
    wd, wm, wf, wo = w_branch_diff, w_branch_mla, w_branch_fourier, w_out
    qn, kvn, sub_g = (a[:, None, :] for a in (mla_q_norm_g, mla_kv_norm_g, diff_subln_g))
    final_g = final_norm_g[None, :]
    flat = lambda a: a.reshape(1, b * n_ctx, a.shape[-1])

    h, h_ctx = x, ctx.reshape(1, b * n_ctx, d)
    for l in range(depth):
        last = l == depth - 1
        lam_init = 0.8 - 0.6 * math.exp(-0.3 * l)
        diff = functools.partial(_diff_body, lam_init=lam_init)
        diff_extra = ([_layer(diff_lambda, l), _layer(sub_g, l)], [diff_lambda, sub_g])
        sizes = (b, n_lat, n_ctx)

        h = _ffn_call(h, mods, l, norm_g, 0, ffn_w_in, ffn_w_out, 0, 0, ctx=h_ctx)
        h_ctx = None
        qd, kd, vd, qm, km, vm, zc, zs, gt = _proj_call(h, mods, l, norm_g, w_proj, uq, ukv, qn, kvn, cs128, tab, b)

        y_lat = (_attn_call(diff, "diff_attn", qd, kd, vd, *diff_extra, DIFF_HEADS, MXU_W, DIFF_W, *sizes),
                 _attn_call(_mla_body, "mla_attn", qm, km, vm, [], [], MLA_HEADS, LANES, MLA_V_W, *sizes),
                 _dft_call(dft_x, zc, zs, b, True))
        y_ctx = None
        if not last:
            yd_c, ym_c = _ctx_attn_call(lam_init, *diff_extra, qd, kd, vd, qm, km, vm, b, n_ctx)
            y_ctx = tuple(flat(a) for a in (yd_c, ym_c, _dft_call(dft_c, zc, zs, b, False)))
        h = _merge_call(h, mods, l, gt, y_lat, y_ctx, wd, wm, wf, wo)
        h = _ffn_call(h, mods, l, norm_g, 2, ffn_w_in, ffn_w_out, 1, 6, final_g=final_g if last else None)
    return h
```

```python
import functools
import math

import numpy as np
import jax
import jax.numpy as jnp
from jax import lax
from jax.experimental import pallas as pl
from jax.experimental.pallas import tpu as pltpu

F32 = jnp.float32
BF16 = jnp.bfloat16

GRID_W = 64
DIFF_HEADS = 4
DIFF_HD = 64
DIFF_VD = 2 * DIFF_HD
MLA_HEADS = 8
MLA_NOPE = 64
MLA_ROPE = 32
MLA_VD = 64
MLA_Q_RANK = 384
MLA_KV_RANK = 256
FOURIER_GROUPS = 4
FOURIER_GROUP_DIM = 128
N_BRANCHES = 3
ROPE_BASE = 10000.0
RMS_EPS = 1e-6
N_MOD = 9

DIFF_W = DIFF_HEADS * 2 * DIFF_HD
MLA_QK_W = MLA_HEADS * 128
MLA_V_W = MLA_HEADS * MLA_VD
FOURIER_W = FOURIER_GROUPS * FOURIER_GROUP_DIM

LANES = 128
MXU_W = 256
FFN_CHUNK = MXU_W
VMEM_LIMIT = 56 * 1024 * 1024
FFN_VMEM_LIMIT = 60 * 1024 * 1024
LOG2E = math.log2(math.e)
PROJ_TILE = 512
PROJ_KR_COL = 3 * DIFF_W + MLA_Q_RANK + MLA_KV_RANK
PROJ_WA_COLS = PROJ_KR_COL + 3 * LANES


def _tile(n, pref):
    t = min(n, pref)
    while n % t:
        t //= 2
    return t


def _fixed(block, idx):
    return pl.BlockSpec(block, lambda *_: idx, pipeline_mode=pl.Buffered(1))


def _whole(a):
    return _fixed(a.shape, (0,) * a.ndim)


def _layer(a, l):
    return _fixed((1,) + a.shape[1:], (l,) + (0,) * (a.ndim - 1))


def _params(n_axes, vmem_limit=VMEM_LIMIT):
    return pltpu.CompilerParams(dimension_semantics=("arbitrary",) * n_axes,
                                vmem_limit_bytes=vmem_limit)


def _norm_mod(x, g, shift, scale):
    y = x * lax.rsqrt(jnp.mean(x * x, axis=-1, keepdims=True) + RMS_EPS)
    return (y * (g * (1.0 + scale)) + shift).astype(BF16)


def _rms(x, g):
    return x * lax.rsqrt(jnp.mean(x * x, axis=-1, keepdims=True) + RMS_EPS) * g


def _dot(a, b):
    return jnp.dot(a, b, preferred_element_type=F32)


def _ada_body(s_ref, w_ref, b_ref, o_ref):
    s = s_ref[...]
    a = (s * jax.nn.sigmoid(s)).astype(BF16)
    o_ref[0] = _dot(a, w_ref[0].astype(BF16)) + b_ref[0]


def _ada_call(cond, ada_w, ada_b):
    depth, d, n = ada_w.shape
    rows = cond.shape[0]
    tn = _tile(n, 2304)
    return pl.pallas_call(
        _ada_body,
        grid=(depth, n // tn),
        in_specs=[pl.BlockSpec((rows, d), lambda l, j: (0, 0)),
                  pl.BlockSpec((1, d, tn), lambda l, j: (l, 0, j)),
                  pl.BlockSpec((1, 1, tn), lambda l, j: (l, 0, j))],
        out_specs=pl.BlockSpec((1, rows, tn), lambda l, j: (l, 0, j)),
        out_shape=jax.ShapeDtypeStruct((depth, rows, n), F32),
        compiler_params=_params(2),
        name="ada_mod",
    )(cond, ada_w, ada_b.reshape(depth, 1, n))


def _mods_block(mods, l):
    d = mods.shape[-1]
    return pl.BlockSpec((1, 1, N_MOD, d), lambda i, j: (l, i, 0, 0))


def _lat_or_ctx_specs(rows, width, n_lat_seg, n_tiles):
    lat = pl.BlockSpec((1, rows, width), lambda i, j: (jnp.minimum(i, n_lat_seg - 1),
                                                       jnp.where(i < n_lat_seg, j, n_tiles - 1), 0))
    ctx = pl.BlockSpec((1, rows, width), lambda i, j: (0, jnp.where(i < n_lat_seg, 0, j), 0))
    return lat, ctx


def _pick(n_lat_seg, lat_ref, ctx_ref):
    return jnp.where(pl.program_id(0) < n_lat_seg, lat_ref[0], ctx_ref[0])


def _ffn_body(x_ref, *rest, mod0, gi, final, n_lat_seg):
    if n_lat_seg is not None:
        x = _pick(n_lat_seg, x_ref, rest[0])
        rest = rest[1:]
    else:
        x = x_ref[0]
    mod_ref, g_ref, win_ref, wout_ref = rest[:4]
    if final:
        fg_ref, o_ref = rest[4:]
    else:
        (o_ref,) = rest[4:]
    d_ff = wout_ref.shape[2]
    xm = _norm_mod(x, g_ref[0, gi:gi + 1, :], mod_ref[0, 0, mod0:mod0 + 1, :], mod_ref[0, 0, mod0 + 1:mod0 + 2, :])
    acc = jnp.zeros(x.shape, F32)
    for c in range(d_ff // FFN_CHUNK):
        lo = c * FFN_CHUNK
        gate = _dot(xm, win_ref[0, 0, :, lo:lo + FFN_CHUNK])
        up = _dot(xm, win_ref[0, 0, :, d_ff + lo:d_ff + lo + FFN_CHUNK])
        act = (gate * jax.nn.sigmoid(gate) * up).astype(BF16)
        acc = acc + _dot(act, wout_ref[0, 0, lo:lo + FFN_CHUNK, :])
    y = x + (0.5 * mod_ref[0, 0, mod0 + 2:mod0 + 3, :]) * acc
    if final:
        y = _rms(y, fg_ref[...])
    o_ref[0] = y


def _ffn_call(x, mods, l, norm_g, gi, win, wout, which, mod0, ctx=None, final_g=None, tm=1024):
    s, t, d = x.shape
    tm = _tile(t, tm if ctx is None else tm // 2)
    final = final_g is not None
    n_seg = s + 1 if ctx is not None else s
    if ctx is not None:
        in_specs = list(_lat_or_ctx_specs(tm, d, s, t // tm))
        args = [x, ctx]
    else:
        in_specs = [pl.BlockSpec((1, tm, d), lambda i, j: (i, j, 0))]
        args = [x]
    in_specs += [_mods_block(mods, l), _layer(norm_g, l),
                 _fixed((1, 1) + win.shape[2:], (l, which, 0, 0)),
                 _fixed((1, 1) + wout.shape[2:], (l, which, 0, 0))]
    args += [mods, norm_g, win, wout]
    if final:
        in_specs.append(_whole(final_g))
        args.append(final_g)
    return pl.pallas_call(
        functools.partial(_ffn_body, mod0=mod0, gi=gi, final=final, n_lat_seg=s if ctx is not None else None),
        grid=(n_seg, t // tm),
        in_specs=in_specs,
        out_specs=pl.BlockSpec((1, tm, d), lambda i, j: (i, j, 0)),
        out_shape=jax.ShapeDtypeStruct((n_seg, t, d), F32),
        compiler_params=_params(2, FFN_VMEM_LIMIT),
        name="ffn",
    )(*args)


def _rope(x, cos, sin_signed, chunk):
    lane = lax.broadcasted_iota(jnp.int32, x.shape, 1)
    even = (lane % (2 * chunk)) < chunk
    partner = jnp.where(even, pltpu.roll(x, LANES - chunk, 1), pltpu.roll(x, chunk, 1))
    return x * cos + partner * sin_signed


def _proj_body(x_ref, mod_ref, g_ref, w_ref, wuq_ref, wukv_ref, qn_ref, kvn_ref, cs_ref, tab_ref,
               qd_ref, kd_ref, vd_ref, qm_ref, km_ref, vm_ref, zc_ref, zs_ref, gt_ref, z_scr,
               *, qscale_d, qscale_m):
    x = x_ref[0]
    xm = _norm_mod(x, g_ref[0, 1:2, :], mod_ref[0, 0, 3:4, :], mod_ref[0, 0, 4:5, :])
    cos_d, sin_d, cos_m, sin_m = tab_ref[0], tab_ref[1], tab_ref[2], tab_ref[3]

    def mm(part, c0, width):
        c0 += PROJ_WA_COLS * part
        return _dot(xm, w_ref[0, :, c0:c0 + width])

    HEAD, TAIL = 0, 1

    o_cq = 3 * DIFF_W
    o_ckv = o_cq + MLA_Q_RANK
    o_kr = o_ckv + MLA_KV_RANK
    cq = _rms(mm(HEAD, o_cq, MLA_Q_RANK), qn_ref[0]).astype(BF16)
    ckv = _rms(mm(HEAD, o_ckv, MLA_KV_RANK), kvn_ref[0]).astype(BF16)
    kr = mm(HEAD, o_kr, LANES)
    kr = jnp.where(lax.broadcasted_iota(jnp.int32, kr.shape, 1) < MLA_ROPE, kr, 0.0)
    kr = _rope(kr, cos_m, sin_m, MLA_ROPE // 4)
    qm = _dot(cq, wuq_ref[0])
    for h in range(MLA_HEADS):
        sl = slice(h * LANES, (h + 1) * LANES)
        qm_ref[0, :, sl] = (_rope(qm[:, sl], cos_m, sin_m, MLA_ROPE // 4) * qscale_m).astype(BF16)
    kv = _dot(ckv, wukv_ref[0])
    for h in range(MLA_HEADS):
        sl = slice(h * LANES, (h + 1) * LANES)
        km_ref[0, :, sl] = (kv[:, sl] + kr).astype(BF16)
    vm_ref[0] = kv[:, MLA_QK_W:].astype(BF16)

    q = mm(HEAD, 0, DIFF_W)
    for h in range(DIFF_HEADS):
        sl = slice(h * LANES, (h + 1) * LANES)
        qd_ref[0, :, sl] = (_rope(q[:, sl], cos_d, sin_d, DIFF_HD // 4) * qscale_d).astype(BF16)
    k = mm(HEAD, DIFF_W, DIFF_W)
    for h in range(DIFF_HEADS):
        sl = slice(h * LANES, (h + 1) * LANES)
        kd_ref[0, :, sl] = _rope(k[:, sl], cos_d, sin_d, DIFF_HD // 4).astype(BF16)

    d = x.shape[-1]
    for bi in range(N_BRANCHES):
        gt_ref[0, :, bi * d:(bi + 1) * d] = (0.5 * jnp.tanh(0.5 * mm(TAIL, FOURIER_W + bi * d, d)) + 0.5).astype(BF16)

    f = mm(TAIL, 0, FOURIER_W).astype(BF16)
    for gi in range(FOURIER_GROUPS):
        sl = slice(gi * LANES, (gi + 1) * LANES)
        z = _dot(f[:, sl], cs_ref[...])
        z_scr[gi] = z[:, :LANES]
        z_scr[FOURIER_GROUPS + gi] = z[:, LANES:]
    half = z_scr.shape[1] // 2
    for parity in range(2):
        for gi in range(FOURIER_GROUPS):
            sl = slice(parity * FOURIER_W + gi * LANES, parity * FOURIER_W + (gi + 1) * LANES)
            zc_ref[0, :, sl] = z_scr[gi, pl.ds(parity, half, stride=2), :].astype(BF16)
            zs_ref[0, :, sl] = z_scr[FOURIER_GROUPS + gi, pl.ds(parity, half, stride=2), :].astype(BF16)
    vd_ref[0] = mm(HEAD, 2 * DIFF_W, DIFF_W).astype(BF16)


def _proj_call(x, mods, l, norm_g, w, wuq, wukv, qn, kvn, cs128, tab, n_lat_seg, tm=PROJ_TILE):
    s, t, d = x.shape
    assert t % tm == 0 and tab.shape[1] == t + tm
    tok = lambda wd, rows=tm: pl.BlockSpec((1, rows, wd), lambda i, j: (i, j, 0))
    tab_map = lambda i, j: (0, jnp.where(i < n_lat_seg, j, t // tm), 0)
    full = [DIFF_W, DIFF_W, DIFF_W, MLA_QK_W, MLA_QK_W, MLA_V_W]
    out_specs = ([tok(wd) for wd in full] + [tok(2 * FOURIER_W, tm // 2)] * 2 + [tok(N_BRANCHES * d)])
    out_shape = ([jax.ShapeDtypeStruct((s, t, wd), BF16) for wd in full]
                 + [jax.ShapeDtypeStruct((s, t // 2, 2 * FOURIER_W), BF16)] * 2
                 + [jax.ShapeDtypeStruct((s, t, N_BRANCHES * d), BF16)])
    return pl.pallas_call(
        functools.partial(_proj_body,
                          qscale_d=DIFF_HD ** -0.5 * LOG2E,
                          qscale_m=(MLA_NOPE + MLA_ROPE) ** -0.5 * LOG2E),
        grid=(s, t // tm),
        in_specs=[tok(d), _mods_block(mods, l), _layer(norm_g, l),
                  _layer(w, l), _layer(wuq, l), _layer(wukv, l), _layer(qn, l), _layer(kvn, l),
                  _whole(cs128), pl.BlockSpec((4, tm, LANES), tab_map)],
        out_specs=out_specs,
        out_shape=out_shape,
        scratch_shapes=[pltpu.VMEM((2 * FOURIER_GROUPS, tm, LANES), F32)],
        compiler_params=_params(2),
        name="in_proj",
    )(x, mods, norm_g, w, wuq, wukv, qn, kvn, cs128, tab)


def _fill_kv(srcs_k, srcs_v, k_scr, v_scr, heads, vd):
    vw = v_scr.shape[1] // heads
    r0 = 0
    for k_ref, v_ref in zip(srcs_k, srcs_v):
        n = k_ref.shape[1]
        k_scr[r0:r0 + n, :] = k_ref[0]
        lane = lax.broadcasted_iota(jnp.int32, (n, vw - vd), 1)
        ones_col = jnp.where(lane == 0, 1.0, 0.0).astype(BF16)
        for h in range(heads):
            v_scr[r0:r0 + n, h * vw:h * vw + vd] = v_ref[0, :, h * vd:(h + 1) * vd]
            v_scr[r0:r0 + n, h * vw + vd:(h + 1) * vw] = ones_col
        r0 += n


def _softmax_pv(s, v_aug, vd):
    m = jnp.max(s, axis=-1, keepdims=True)
    e = jnp.exp2(s - m).astype(BF16)
    r = _dot(e, v_aug)
    return r[:, :vd], r[:, vd:vd + 1]


SMALL_UNIT_ROWS = 128


def _head_row_units(heads, rows):
    if rows <= 2 * SMALL_UNIT_ROWS:
        return [(h, 0, rows) for h in range(heads)]
    units = [(0, 0, SMALL_UNIT_ROWS), (0, SMALL_UNIT_ROWS, rows)]
    units += [(h, 0, rows) for h in range(1, heads - 1)]
    units += [(heads - 1, 0, rows - SMALL_UNIT_ROWS), (heads - 1, rows - SMALL_UNIT_ROWS, rows)]
    return units


def _scores_one_ahead(items, scores, consume):
    ahead = scores(items[0])
    for i, item in enumerate(items):
        s = ahead
        if i + 1 < len(items):
            ahead = scores(items[i + 1])
        consume(item, s)


def _qk(q, k):
    return lax.dot_general(q, k, (((1,), (1,)), ((), ())), preferred_element_type=F32)


def _diff_body(*refs, n_src, lam_init):
    q_ref, lam_ref, sg_ref = refs[0], refs[1], refs[2]
    k_srcs = refs[3:3 + n_src]
    v_srcs = refs[3 + n_src:3 + 2 * n_src]
    o_ref, k_scr, v_scr = refs[3 + 2 * n_src:]

    @pl.when(pl.program_id(1) == 0)
    def _():
        _fill_kv(k_srcs, v_srcs, k_scr, v_scr, DIFF_HEADS, DIFF_VD)

    lp = lam_ref[0]
    lam = (jnp.exp(jnp.sum(lp[0:1] * lp[1:2], axis=-1, keepdims=True))
           - jnp.exp(jnp.sum(lp[2:3] * lp[3:4], axis=-1, keepdims=True)) + lam_init)
    zero = jnp.zeros((), BF16)
    vw = v_scr.shape[1] // DIFF_HEADS
    def scores(item):
        h, r0, r1 = item
        sl = slice(h * LANES, (h + 1) * LANES)
        qh = q_ref[0, r0:r1, sl]
        first = lax.broadcasted_iota(jnp.int32, qh.shape, 1) < DIFF_HD
        return (_qk(jnp.where(first, qh, zero), k_scr[:, sl]), _qk(jnp.where(first, zero, qh), k_scr[:, sl]))

    def consume(item, s):
        h, r0, r1 = item
        e1 = jnp.exp2(s[0] - jnp.max(s[0], axis=-1, keepdims=True))
        e2 = jnp.exp2(s[1] - jnp.max(s[1], axis=-1, keepdims=True))
        w = e1 * (1.0 / jnp.sum(e1, axis=-1, keepdims=True)) - e2 * (lam / jnp.sum(e2, axis=-1, keepdims=True))
        o = _dot(w.astype(BF16), v_scr[:, h * vw:h * vw + DIFF_VD])
        o_ref[0, r0:r1, h * LANES:(h + 1) * LANES] = (_rms(o, sg_ref[0]) * (1.0 - lam_init)).astype(BF16)

    _scores_one_ahead(_head_row_units(DIFF_HEADS, q_ref.shape[1]), scores, consume)


def _mla_body(*refs, n_src):
    q_ref = refs[0]
    k_srcs = refs[1:1 + n_src]
    v_srcs = refs[1 + n_src:1 + 2 * n_src]
    o_ref, k_scr, v_scr = refs[1 + 2 * n_src:]

    @pl.when(pl.program_id(1) == 0)
    def _():
        _fill_kv(k_srcs, v_srcs, k_scr, v_scr, MLA_HEADS, MLA_VD)

    vw = v_scr.shape[1] // MLA_HEADS

    def scores(item):
        h, r0, r1 = item
        sl = slice(h * LANES, (h + 1) * LANES)
        return _qk(q_ref[0, r0:r1, sl], k_scr[:, sl])

    def consume(item, s):
        h, r0, r1 = item
        o, l = _softmax_pv(s, v_scr[:, h * vw:(h + 1) * vw], MLA_VD)
        o_ref[0, r0:r1, h * MLA_VD:(h + 1) * MLA_VD] = (o * (1.0 / l)).astype(BF16)

    _scores_one_ahead(_head_row_units(MLA_HEADS, q_ref.shape[1]), scores, consume)


def _ctx_rows_spec(a, b, n_ctx):
    return pl.BlockSpec((1, n_ctx, a.shape[2]), lambda i, j: (b, i, 0))


def _attn_call(body, name, q, k, v, extra_specs, extra, heads, v_aug_w, out_w, b, n_lat, n_ctx, tq=1024):
    lat_rows = lambda a: pl.BlockSpec((1, n_lat, a.shape[2]), lambda i, j: (i, 0, 0))
    tq = _tile(n_lat, tq)
    return pl.pallas_call(
        functools.partial(body, n_src=2),
        grid=(b, n_lat // tq),
        in_specs=([pl.BlockSpec((1, tq, q.shape[2]), lambda i, j: (i, j, 0))] + extra_specs
                  + [_ctx_rows_spec(k, b, n_ctx), lat_rows(k), _ctx_rows_spec(v, b, n_ctx), lat_rows(v)]),
        out_specs=pl.BlockSpec((1, tq, out_w), lambda i, j: (i, j, 0)),
        out_shape=jax.ShapeDtypeStruct((b, n_lat, out_w), BF16),
        scratch_shapes=[pltpu.VMEM((n_ctx + n_lat, k.shape[2]), BF16),
                        pltpu.VMEM((n_ctx + n_lat, heads * v_aug_w), BF16)],
        compiler_params=_params(2),
        name=name,
    )(q, *extra, k, k, v, v)


def _ctx_attn_body(qd_ref, lam_ref, sg_ref, kd_ref, vd_ref, qm_ref, km_ref, vm_ref, od_ref, om_ref,
                   kd_scr, vd_scr, km_scr, vm_scr, *, lam_init, n_ctx):
    for i in range(qd_ref.shape[1] // n_ctx):
        rows = (slice(None), slice(i * n_ctx, (i + 1) * n_ctx), slice(None))
        _diff_body(qd_ref.at[rows], lam_ref, sg_ref, kd_ref.at[rows], vd_ref.at[rows], od_ref.at[rows],
                   kd_scr.at[i], vd_scr.at[i], n_src=1, lam_init=lam_init)
        _mla_body(qm_ref.at[rows], km_ref.at[rows], vm_ref.at[rows], om_ref.at[rows],
                  km_scr.at[i], vm_scr.at[i], n_src=1)


def _ctx_attn_call(lam_init, diff_extra_specs, diff_extra, qd, kd, vd, qm, km, vm, b, n_ctx):
    seg = lambda a: pl.BlockSpec((1, b * n_ctx, a.shape[2]), lambda i, j: (b, 0, 0))
    out = lambda w: pl.BlockSpec((1, b * n_ctx, w), lambda i, j: (0, 0, 0))
    return pl.pallas_call(
        functools.partial(_ctx_attn_body, lam_init=lam_init, n_ctx=n_ctx),
        grid=(1, 1),
        in_specs=[seg(qd)] + diff_extra_specs + [seg(kd), seg(vd), seg(qm), seg(km), seg(vm)],
        out_specs=[out(DIFF_W), out(MLA_V_W)],
        out_shape=[jax.ShapeDtypeStruct((1, b * n_ctx, DIFF_W), BF16),
                   jax.ShapeDtypeStruct((1, b * n_ctx, MLA_V_W), BF16)],
        scratch_shapes=[pltpu.VMEM((b, n_ctx, kd.shape[2]), BF16), pltpu.VMEM((b, n_ctx, DIFF_HEADS * MXU_W), BF16),
                        pltpu.VMEM((b, n_ctx, km.shape[2]), BF16), pltpu.VMEM((b, n_ctx, MLA_HEADS * LANES), BF16)],
        compiler_params=_params(2),
        name="ctx_attn",
    )(qd, *diff_extra, kd, vd, qm, km, vm)


def _dft_body(t_ref, zc_ref, zs_ref, o_ref):
    w = o_ref.shape[-1]
    even = _dot(t_ref[0], zc_ref[0, :, :w]) + _dot(t_ref[1], zs_ref[0, :, :w])
    odd = _dot(t_ref[2], zc_ref[0, :, w:]) + _dot(t_ref[3], zs_ref[0, :, w:])
    o_ref[0, 0] = (even + odd).astype(BF16)
    o_ref[0, 1] = (even - odd).astype(BF16)


def _dft_call(tables, zc, zs, b, latent, tm=1024):
    half, w = tables.shape[1], zc.shape[2] // 2
    t = 2 * half
    tm = _tile(half, tm)
    z_spec = pl.BlockSpec((1, half, 2 * w), (lambda i, j: (j, 0, 0)) if latent else (lambda i, j: (b, j, 0)))
    out = pl.pallas_call(
        _dft_body,
        grid=(half // tm, b),
        in_specs=[pl.BlockSpec((4, tm, half), lambda i, j: (0, i, 0)), z_spec, z_spec],
        out_specs=pl.BlockSpec((1, 2, tm, w), lambda i, j: (j, 0, i, 0)),
        out_shape=jax.ShapeDtypeStruct((b, 2, half, w), BF16),
        compiler_params=_params(2),
        name="pos_dft",
    )(tables, zc, zs)
    return out.reshape(b, t, w)


def _merge_body(x_ref, mod_ref, gt_ref, *rest, n_lat_seg):
    if n_lat_seg is None:
        (yd_ref, ym_ref, yf_ref), rest = rest[:3], rest[3:]
        yd, ym, yf = yd_ref[0], ym_ref[0], yf_ref[0]
    else:
        ys, rest = rest[:6], rest[6:]
        yd, ym, yf = (_pick(n_lat_seg, ys[2 * i], ys[2 * i + 1]) for i in range(N_BRANCHES))
    wd_ref, wm_ref, wf_ref, wo_ref, o_ref = rest
    x = x_ref[0]
    d = x.shape[-1]
    merged = (gt_ref[0, :, 0:d].astype(F32) * _dot(yd, wd_ref[0])
              + gt_ref[0, :, d:2 * d].astype(F32) * _dot(ym, wm_ref[0])
              + gt_ref[0, :, 2 * d:3 * d].astype(F32) * _dot(yf, wf_ref[0]))
    o_ref[0] = x + mod_ref[0, 0, 5:6, :] * _dot(merged.astype(BF16), wo_ref[0])


def _merge_call(x, mods, l, gt, y_lat, y_ctx, wd, wm, wf, wo, tm=512):
    b, t = y_lat[0].shape[:2]
    d = x.shape[2]
    tm = _tile(t, tm)
    tok = lambda wd_: pl.BlockSpec((1, tm, wd_), lambda i, j: (i, j, 0))
    if y_ctx is None:
        n_seg, y_specs, y_args = b, [tok(a.shape[2]) for a in y_lat], list(y_lat)
    else:
        n_seg, y_specs, y_args = b + 1, [], []
        for yl, yc in zip(y_lat, y_ctx):
            y_specs += list(_lat_or_ctx_specs(tm, yl.shape[2], b, t // tm))
            y_args += [yl, yc]
    return pl.pallas_call(
        functools.partial(_merge_body, n_lat_seg=None if y_ctx is None else b),
        grid=(n_seg, t // tm),
        in_specs=[tok(d), _mods_block(mods, l), tok(gt.shape[2])] + y_specs
                 + [_layer(wd, l), _layer(wm, l), _layer(wf, l), _layer(wo, l)],
        out_specs=tok(d),
        out_shape=jax.ShapeDtypeStruct((n_seg, t, d), F32),
        compiler_params=_params(2),
        name="merge_out",
    )(x, mods, gt, *y_args, wd, wm, wf, wo)


def _rope_tables(n_lat, n_ident):
    rows = np.arange(n_lat) // GRID_W
    cols = np.arange(n_lat) % GRID_W

    def cos_sin(dim):
        nf = dim // 4
        freqs = np.power(ROPE_BASE, -np.arange(nf, dtype=np.float64) / nf)
        ar = rows.astype(np.float64)[:, None] * freqs[None, :]
        ac = cols.astype(np.float64)[:, None] * freqs[None, :]
        ang = np.concatenate([ar, ar, ac, ac], axis=-1)
        sign = np.concatenate([-np.ones(nf), np.ones(nf), -np.ones(nf), np.ones(nf)])
        return np.cos(ang), np.sin(ang) * sign[None, :]

    cd, sd = cos_sin(DIFF_HD)
    cos_d = np.tile(cd, (1, LANES // DIFF_HD))
    sin_d = np.tile(sd, (1, LANES // DIFF_HD))
    cm, sm = cos_sin(MLA_ROPE)
    cos_m = np.ones((n_lat, LANES))
    sin_m = np.zeros((n_lat, LANES))
    cos_m[:, :MLA_ROPE] = cm
    sin_m[:, :MLA_ROPE] = sm
    lat = np.stack([cos_d, sin_d, cos_m, sin_m])
    ident = np.stack([np.ones((n_ident, LANES)), np.zeros((n_ident, LANES))] * 2)
    return jnp.asarray(np.concatenate([lat, ident], axis=1), F32)


def _angles(rows, cols, n):
    return 2.0 * np.pi * ((rows[:, None] * cols[None, :]) % n) / n


def _pos_dft_tables(n, group):
    j, m = np.arange(n // 2), np.arange(n // 2)
    scale = 1.0 / math.sqrt(n * group)
    ae, ao = _angles(j, 2 * m, n), _angles(j, 2 * m + 1, n)
    return jnp.asarray(np.stack([np.cos(ae), -np.sin(ae), np.cos(ao), -np.sin(ao)]) * scale, F32).astype(BF16)


def _chan_dft_table(group):
    a = _angles(np.arange(group), np.arange(group), group)
    return jnp.asarray(np.concatenate([np.cos(a), np.sin(a)], axis=1), F32).astype(BF16)


def _proj_weight_body(*refs):
    *w_refs, o_ref = refs
    blk = w_refs[0].shape[1]
    for k, w_ref in enumerate(w_refs):
        o_ref[0, :, k * blk:(k + 1) * blk] = w_ref[0].T.astype(BF16)


def _proj_weight_call(w_t, n_head, tail_start, blk=MXU_W, per_step=4):
    depth, cols, d = w_t.shape
    head_blocks, tail_blocks = n_head // blk, (cols - tail_start) // blk
    assert n_head % blk == 0 and (cols - tail_start) % blk == 0 and tail_start % 32 == 0
    assert (head_blocks + tail_blocks) % per_step == 0

    def window(k):
        def index(l, j):
            m = j * per_step + k
            return l, pl.multiple_of(jnp.where(m < head_blocks, m * blk, tail_start + (m - head_blocks) * blk), 32), 0
        return pl.BlockSpec((pl.Element(1), pl.Element(blk), pl.Element(d)), index)

    return pl.pallas_call(
        _proj_weight_body,
        grid=(depth, (head_blocks + tail_blocks) // per_step),
        in_specs=[window(k) for k in range(per_step)],
        out_specs=pl.BlockSpec((1, d, per_step * blk), lambda l, j: (l, 0, j)),
        out_shape=jax.ShapeDtypeStruct((depth, d, (head_blocks + tail_blocks) * blk), BF16),
        compiler_params=_params(2),
        name="proj_weight",
    )(*([w_t] * per_step))


def _prep_mla_weights(mla_w_uq, mla_w_ukv):
    depth = mla_w_uq.shape[0]
    tail = LANES - MLA_NOPE - MLA_ROPE
    uq = mla_w_uq.reshape(depth, MLA_Q_RANK, MLA_HEADS, MLA_NOPE + MLA_ROPE)
    uq = jnp.concatenate([uq[..., MLA_NOPE:], uq[..., :MLA_NOPE], jnp.zeros(uq.shape[:3] + (tail,), uq.dtype)], axis=-1)
    uq = uq.reshape(depth, MLA_Q_RANK, MLA_QK_W).astype(BF16)
    ukv = mla_w_ukv.reshape(depth, MLA_KV_RANK, MLA_HEADS, MLA_NOPE + MLA_VD)
    kn = jnp.pad(ukv[..., :MLA_NOPE], ((0, 0), (0, 0), (0, 0), (MLA_ROPE, tail)))
    kn = kn.reshape(depth, MLA_KV_RANK, MLA_QK_W)
    vv = ukv[..., MLA_NOPE:].reshape(depth, MLA_KV_RANK, MLA_V_W)
    return uq, jnp.concatenate([kn, vv], axis=2).astype(BF16)


def kernel(x, c, ctx, c_ctx, ada_w, ada_b, norm_g, ffn_w_in, ffn_w_out, w_in, diff_lambda,
           diff_subln_g, mla_q_norm_g, mla_w_uq, mla_kv_norm_g, mla_w_ukv, w_branch_diff,
           w_branch_mla, w_branch_fourier, w_out, final_norm_g):
    b, n_lat, d = x.shape
    n_ctx = ctx.shape[1]
    depth = ada_w.shape[0]
    assert b * n_ctx == n_lat, "context tokens of all batches must fill exactly one latent-length segment"

    cond_rows = -(-(b + 1) // 8) * 8
    cond = jnp.concatenate([c, c_ctx[None, :], jnp.zeros((cond_rows - b - 1, d), F32)], axis=0)
    mods = _ada_call(cond, ada_w, ada_b).reshape(depth, cond_rows, N_MOD, d)

    tab = _rope_tables(n_lat, PROJ_TILE)
    dft_x = _pos_dft_tables(n_lat, FOURIER_GROUP_DIM)
    dft_c = _pos_dft_tables(n_ctx, FOURIER_GROUP_DIM)
    cs128 = _chan_dft_table(FOURIER_GROUP_DIM)

    w_proj = _proj_weight_call(jnp.swapaxes(w_in, 1, 2), PROJ_WA_COLS, PROJ_KR_COL + MLA_ROPE)
    uq, ukv = _prep_mla_weights(mla_w_uq, mla_w_ukv)
    wd, wm, wf, wo = w_branch_diff, w_branch_mla, w_branch_fourier, w_out
    qn, kvn, sub_g = (a[:, None, :] for a in (mla_q_norm_g, mla_kv_norm_g, diff_subln_g))
    final_g = final_norm_g[None, :]
    flat = lambda a: a.reshape(1, b * n_ctx, a.shape[-1])

    h, h_ctx = x, ctx.reshape(1, b * n_ctx, d)
    for l in range(depth):
        last = l == depth - 1
        lam_init = 0.8 - 0.6 * math.exp(-0.3 * l)
        diff = functools.partial(_diff_body, lam_init=lam_init)
        diff_extra = ([_layer(diff_lambda, l), _layer(sub_g, l)], [diff_lambda, sub_g])
        sizes = (b, n_lat, n_ctx)

        h = _ffn_call(h, mods, l, norm_g, 0, ffn_w_in, ffn_w_out, 0, 0, ctx=h_ctx)
        h_ctx = None
        qd, kd, vd, qm, km, vm, zc, zs, gt = _proj_call(h, mods, l, norm_g, w_proj, uq, ukv, qn, kvn, cs128, tab, b)

        y_lat = (_attn_call(diff, "diff_attn", qd, kd, vd, *diff_extra, DIFF_HEADS, MXU_W, DIFF_W, *sizes, tq=512),
                 _attn_call(_mla_body, "mla_attn", qm, km, vm, [], [], MLA_HEADS, LANES, MLA_V_W, *sizes),
                 _dft_call(dft_x, zc, zs, b, True))
        y_ctx = None
        if not last:
            yd_c, ym_c = _ctx_attn_call(lam_init, *diff_extra, qd, kd, vd, qm, km, vm, b, n_ctx)
            y_ctx = tuple(flat(a) for a in (yd_c, ym_c, _dft_call(dft_c, zc, zs, b, False)))
        h = _merge_call(h, mods, l, gt, y_lat, y_ctx, wd, wm, wf, wo)
        h = _ffn_call(h, mods, l, norm_g, 2, ffn_w_in, ffn_w_out, 1, 6, final_g=final_g if last else None)
    return h
```

```python
import functools
import math

import numpy as np
import jax
import jax.numpy as jnp
from jax import lax
from jax.experimental import pallas as pl
from jax.experimental.pallas import tpu as pltpu

F32 = jnp.float32
BF16 = jnp.bfloat16

GRID_W = 64
DIFF_HEADS = 4
DIFF_HD = 64
DIFF_VD = 2 * DIFF_HD
MLA_HEADS = 8
MLA_NOPE = 64
MLA_ROPE = 32
MLA_VD = 64
MLA_Q_RANK = 384
MLA_KV_RANK = 256
FOURIER_GROUPS = 4
FOURIER_GROUP_DIM = 128
N_BRANCHES = 3
ROPE_BASE = 10000.0
RMS_EPS = 1e-6
N_MOD = 9

DIFF_W = DIFF_HEADS * 2 * DIFF_HD
MLA_QK_W = MLA_HEADS * 128
MLA_V_W = MLA_HEADS * MLA_VD
FOURIER_W = FOURIER_GROUPS * FOURIER_GROUP_DIM

LANES = 128
MXU_W = 256
FFN_CHUNK = MXU_W
VMEM_LIMIT = 56 * 1024 * 1024
FFN_VMEM_LIMIT = 60 * 1024 * 1024
LOG2E = math.log2(math.e)
PROJ_TILE = 512
PROJ_KR_COL = 3 * DIFF_W + MLA_Q_RANK + MLA_KV_RANK
PROJ_WA_COLS = PROJ_KR_COL + 3 * LANES


def _tile(n, pref):
    t = min(n, pref)
    while n % t:
        t //= 2
    return t


def _fixed(block, idx):
    return pl.BlockSpec(block, lambda *_: idx, pipeline_mode=pl.Buffered(1))


def _whole(a):
    return _fixed(a.shape, (0,) * a.ndim)


def _layer(a, l):
    return _fixed((1,) + a.shape[1:], (l,) + (0,) * (a.ndim - 1))


def _params(n_axes, vmem_limit=VMEM_LIMIT):
    return pltpu.CompilerParams(dimension_semantics=("arbitrary",) * n_axes,
                                vmem_limit_bytes=vmem_limit)


def _norm_mod(x, g, shift, scale):
    y = x * lax.rsqrt(jnp.mean(x * x, axis=-1, keepdims=True) + RMS_EPS)
    return (y * (g * (1.0 + scale)) + shift).astype(BF16)


def _rms(x, g):
    return x * lax.rsqrt(jnp.mean(x * x, axis=-1, keepdims=True) + RMS_EPS) * g


def _dot(a, b):
    return jnp.dot(a, b, preferred_element_type=F32)


def _ada_body(s_ref, w_ref, b_ref, o_ref):
    s = s_ref[...]
    a = (s * jax.nn.sigmoid(s)).astype(BF16)
    o_ref[0] = _dot(a, w_ref[0].astype(BF16)) + b_ref[0]


def _ada_call(cond, ada_w, ada_b):
    depth, d, n = ada_w.shape
    rows = cond.shape[0]
    tn = _tile(n, 2304)
    return pl.pallas_call(
        _ada_body,
        grid=(depth, n // tn),
        in_specs=[pl.BlockSpec((rows, d), lambda l, j: (0, 0)),
                  pl.BlockSpec((1, d, tn), lambda l, j: (l, 0, j)),
                  pl.BlockSpec((1, 1, tn), lambda l, j: (l, 0, j))],
        out_specs=pl.BlockSpec((1, rows, tn), lambda l, j: (l, 0, j)),
        out_shape=jax.ShapeDtypeStruct((depth, rows, n), F32),
        compiler_params=_params(2),
        name="ada_mod",
    )(cond, ada_w, ada_b.reshape(depth, 1, n))


def _mods_block(mods, l):
    d = mods.shape[-1]
    return pl.BlockSpec((1, 1, N_MOD, d), lambda i, j: (l, i, 0, 0))


def _lat_or_ctx_specs(rows, width, n_lat_seg, n_tiles):
    lat = pl.BlockSpec((1, rows, width), lambda i, j: (jnp.minimum(i, n_lat_seg - 1),
                                                       jnp.where(i < n_lat_seg, j, n_tiles - 1), 0))
    ctx = pl.BlockSpec((1, rows, width), lambda i, j: (0, jnp.where(i < n_lat_seg, 0, j), 0))
    return lat, ctx


def _pick(n_lat_seg, lat_ref, ctx_ref):
    return jnp.where(pl.program_id(0) < n_lat_seg, lat_ref[0], ctx_ref[0])


def _ffn_body(x_ref, *rest, mod0, gi, final, n_lat_seg):
    if n_lat_seg is not None:
        x = _pick(n_lat_seg, x_ref, rest[0])
        rest = rest[1:]
    else:
        x = x_ref[0]
    mod_ref, g_ref, win_ref, wout_ref = rest[:4]
    if final:
        fg_ref, o_ref = rest[4:]
    else:
        (o_ref,) = rest[4:]
    d_ff = wout_ref.shape[2]
    xm = _norm_mod(x, g_ref[0, gi:gi + 1, :], mod_ref[0, 0, mod0:mod0 + 1, :], mod_ref[0, 0, mod0 + 1:mod0 + 2, :])
    acc = jnp.zeros(x.shape, F32)
    for c in range(d_ff // FFN_CHUNK):
        lo = c * FFN_CHUNK
        gate = _dot(xm, win_ref[0, 0, :, lo:lo + FFN_CHUNK])
        up = _dot(xm, win_ref[0, 0, :, d_ff + lo:d_ff + lo + FFN_CHUNK])
        act = (gate * jax.nn.sigmoid(gate) * up).astype(BF16)
        acc = acc + _dot(act, wout_ref[0, 0, lo:lo + FFN_CHUNK, :])
    y = x + (0.5 * mod_ref[0, 0, mod0 + 2:mod0 + 3, :]) * acc
    if final:
        y = _rms(y, fg_ref[...])
    o_ref[0] = y


def _ffn_call(x, mods, l, norm_g, gi, win, wout, which, mod0, ctx=None, final_g=None, tm=1024):
    s, t, d = x.shape
    tm = _tile(t, tm if ctx is None else tm // 2)
    final = final_g is not None
    n_seg = s + 1 if ctx is not None else s
    if ctx is not None:
        in_specs = list(_lat_or_ctx_specs(tm, d, s, t // tm))
        args = [x, ctx]
    else:
        in_specs = [pl.BlockSpec((1, tm, d), lambda i, j: (i, j, 0))]
        args = [x]
    in_specs += [_mods_block(mods, l), _layer(norm_g, l),
                 _fixed((1, 1) + win.shape[2:], (l, which, 0, 0)),
                 _fixed((1, 1) + wout.shape[2:], (l, which, 0, 0))]
    args += [mods, norm_g, win, wout]
    if final:
        in_specs.append(_whole(final_g))
        args.append(final_g)
    return pl.pallas_call(
        functools.partial(_ffn_body, mod0=mod0, gi=gi, final=final, n_lat_seg=s if ctx is not None else None),
        grid=(n_seg, t // tm),
        in_specs=in_specs,
        out_specs=pl.BlockSpec((1, tm, d), lambda i, j: (i, j, 0)),
        out_shape=jax.ShapeDtypeStruct((n_seg, t, d), F32),
        compiler_params=_params(2, FFN_VMEM_LIMIT),
        name="ffn",
    )(*args)


def _rope(x, cos, sin_signed, chunk):
    lane = lax.broadcasted_iota(jnp.int32, x.shape, 1)
    even = (lane % (2 * chunk)) < chunk
    partner = jnp.where(even, pltpu.roll(x, LANES - chunk, 1), pltpu.roll(x, chunk, 1))
    return x * cos + partner * sin_signed


def _proj_body(x_ref, mod_ref, g_ref, w_ref, wuq_ref, wukv_ref, qn_ref, kvn_ref, cs_ref, tab_ref,
               qd_ref, kd_ref, vd_ref, qm_ref, km_ref, vm_ref, zc_ref, zs_ref, gt_ref, z_scr,
               *, qscale_d, qscale_m):
    x = x_ref[0]
    xm = _norm_mod(x, g_ref[0, 1:2, :], mod_ref[0, 0, 3:4, :], mod_ref[0, 0, 4:5, :])
    cos_d, sin_d, cos_m, sin_m = tab_ref[0], tab_ref[1], tab_ref[2], tab_ref[3]

    def mm(part, c0, width):
        c0 += PROJ_WA_COLS * part
        return _dot(xm, w_ref[0, :, c0:c0 + width])

    HEAD, TAIL = 0, 1

    o_cq = 3 * DIFF_W
    o_ckv = o_cq + MLA_Q_RANK
    o_kr = o_ckv + MLA_KV_RANK
    cq = _rms(mm(HEAD, o_cq, MLA_Q_RANK), qn_ref[0]).astype(BF16)
    ckv = _rms(mm(HEAD, o_ckv, MLA_KV_RANK), kvn_ref[0]).astype(BF16)
    kr = mm(HEAD, o_kr, LANES)
    kr = jnp.where(lax.broadcasted_iota(jnp.int32, kr.shape, 1) < MLA_ROPE, kr, 0.0)
    kr = _rope(kr, cos_m, sin_m, MLA_ROPE // 4)
    qm = _dot(cq, wuq_ref[0])
    for h in range(MLA_HEADS):
        sl = slice(h * LANES, (h + 1) * LANES)
        qm_ref[0, :, sl] = (_rope(qm[:, sl], cos_m, sin_m, MLA_ROPE // 4) * qscale_m).astype(BF16)
    kv = _dot(ckv, wukv_ref[0])
    for h in range(MLA_HEADS):
        sl = slice(h * LANES, (h + 1) * LANES)
        km_ref[0, :, sl] = (kv[:, sl] + kr).astype(BF16)
    vm_ref[0] = kv[:, MLA_QK_W:].astype(BF16)

    q = mm(HEAD, 0, DIFF_W)
    for h in range(DIFF_HEADS):
        sl = slice(h * LANES, (h + 1) * LANES)
        qd_ref[0, :, sl] = (_rope(q[:, sl], cos_d, sin_d, DIFF_HD // 4) * qscale_d).astype(BF16)
    k = mm(HEAD, DIFF_W, DIFF_W)
    for h in range(DIFF_HEADS):
        sl = slice(h * LANES, (h + 1) * LANES)
        kd_ref[0, :, sl] = _rope(k[:, sl], cos_d, sin_d, DIFF_HD // 4).astype(BF16)

    d = x.shape[-1]
    for bi in range(N_BRANCHES):
        gt_ref[0, :, bi * d:(bi + 1) * d] = (0.5 * jnp.tanh(0.5 * mm(TAIL, FOURIER_W + bi * d, d)) + 0.5).astype(BF16)

    f = mm(TAIL, 0, FOURIER_W).astype(BF16)
    for gi in range(FOURIER_GROUPS):
        sl = slice(gi * LANES, (gi + 1) * LANES)
        z = _dot(f[:, sl], cs_ref[...])
        z_scr[gi] = z[:, :LANES]
        z_scr[FOURIER_GROUPS + gi] = z[:, LANES:]
    half = z_scr.shape[1] // 2
    for parity in range(2):
        for gi in range(FOURIER_GROUPS):
            sl = slice(parity * FOURIER_W + gi * LANES, parity * FOURIER_W + (gi + 1) * LANES)
            zc_ref[0, :, sl] = z_scr[gi, pl.ds(parity, half, stride=2), :].astype(BF16)
            zs_ref[0, :, sl] = z_scr[FOURIER_GROUPS + gi, pl.ds(parity, half, stride=2), :].astype(BF16)
    vd_ref[0] = mm(HEAD, 2 * DIFF_W, DIFF_W).astype(BF16)


def _proj_call(x, mods, l, norm_g, w, wuq, wukv, qn, kvn, cs128, tab, n_lat_seg, tm=PROJ_TILE):
    s, t, d = x.shape
    assert t % tm == 0 and tab.shape[1] == t + tm
    tok = lambda wd, rows=tm: pl.BlockSpec((1, rows, wd), lambda i, j: (i, j, 0))
    tab_map = lambda i, j: (0, jnp.where(i < n_lat_seg, j, t // tm), 0)
    full = [DIFF_W, DIFF_W, DIFF_W, MLA_QK_W, MLA_QK_W, MLA_V_W]
    out_specs = ([tok(wd) for wd in full] + [tok(2 * FOURIER_W, tm // 2)] * 2 + [tok(N_BRANCHES * d)])
    out_shape = ([jax.ShapeDtypeStruct((s, t, wd), BF16) for wd in full]
                 + [jax.ShapeDtypeStruct((s, t // 2, 2 * FOURIER_W), BF16)] * 2
                 + [jax.ShapeDtypeStruct((s, t, N_BRANCHES * d), BF16)])
    return pl.pallas_call(
        functools.partial(_proj_body,
                          qscale_d=DIFF_HD ** -0.5 * LOG2E,
                          qscale_m=(MLA_NOPE + MLA_ROPE) ** -0.5 * LOG2E),
        grid=(s, t // tm),
        in_specs=[tok(d), _mods_block(mods, l), _layer(norm_g, l),
                  _layer(w, l), _layer(wuq, l), _layer(wukv, l), _layer(qn, l), _layer(kvn, l),
                  _whole(cs128), pl.BlockSpec((4, tm, LANES), tab_map)],
        out_specs=out_specs,
        out_shape=out_shape,
        scratch_shapes=[pltpu.VMEM((2 * FOURIER_GROUPS, tm, LANES), F32)],
        compiler_params=_params(2),
        name="in_proj",
    )(x, mods, norm_g, w, wuq, wukv, qn, kvn, cs128, tab)


def _fill_kv(srcs_k, srcs_v, k_scr, v_scr, heads, vd):
    vw = v_scr.shape[1] // heads
    r0 = 0
    for k_ref, v_ref in zip(srcs_k, srcs_v):
        n = k_ref.shape[1]
        k_scr[r0:r0 + n, :] = k_ref[0]
        lane = lax.broadcasted_iota(jnp.int32, (n, vw - vd), 1)
        ones_col = jnp.where(lane == 0, 1.0, 0.0).astype(BF16)
        for h in range(heads):
            v_scr[r0:r0 + n, h * vw:h * vw + vd] = v_ref[0, :, h * vd:(h + 1) * vd]
            v_scr[r0:r0 + n, h * vw + vd:(h + 1) * vw] = ones_col
        r0 += n


def _softmax_pv(s, v_aug, vd):
    m = jnp.max(s, axis=-1, keepdims=True)
    e = jnp.exp2(s - m).astype(BF16)
    r = _dot(e, v_aug)
    return r[:, :vd], r[:, vd:vd + 1]


SMALL_UNIT_ROWS = 128


def _head_row_units(heads, rows):
    if rows <= 2 * SMALL_UNIT_ROWS:
        return [(h, 0, rows) for h in range(heads)]
    units = [(0, 0, SMALL_UNIT_ROWS), (0, SMALL_UNIT_ROWS, rows)]
    units += [(h, 0, rows) for h in range(1, heads - 1)]
    units += [(heads - 1, 0, rows - SMALL_UNIT_ROWS), (heads - 1, rows - SMALL_UNIT_ROWS, rows)]
    return units


def _scores_one_ahead(items, scores, consume):
    ahead = scores(items[0])
    for i, item in enumerate(items):
        s = ahead
        if i + 1 < len(items):
            ahead = scores(items[i + 1])
        consume(item, s)


def _qk(q, k):
    return lax.dot_general(q, k, (((1,), (1,)), ((), ())), preferred_element_type=F32)


def _diff_body(*refs, n_src, lam_init):
    q_ref, lam_ref, sg_ref = refs[0], refs[1], refs[2]
    k_srcs = refs[3:3 + n_src]
    v_srcs = refs[3 + n_src:3 + 2 * n_src]
    o_ref, k_scr, v_scr = refs[3 + 2 * n_src:]

    @pl.when(pl.program_id(1) == 0)
    def _():
        _fill_kv(k_srcs, v_srcs, k_scr, v_scr, DIFF_HEADS, DIFF_VD)

    lp = lam_ref[0]
    lam = (jnp.exp(jnp.sum(lp[0:1] * lp[1:2], axis=-1, keepdims=True))
           - jnp.exp(jnp.sum(lp[2:3] * lp[3:4], axis=-1, keepdims=True)) + lam_init)
    zero = jnp.zeros((), BF16)
    vw = v_scr.shape[1] // DIFF_HEADS
    def scores(item):
        h, r0, r1 = item
        sl = slice(h * LANES, (h + 1) * LANES)
        qh = q_ref[0, r0:r1, sl]
        first = lax.broadcasted_iota(jnp.int32, qh.shape, 1) < DIFF_HD
        return (_qk(jnp.where(first, qh, zero), k_scr[:, sl]), _qk(jnp.where(first, zero, qh), k_scr[:, sl]))

    def consume(item, s):
        h, r0, r1 = item
        e1 = jnp.exp2(s[0] - jnp.max(s[0], axis=-1, keepdims=True))
        e2 = jnp.exp2(s[1] - jnp.max(s[1], axis=-1, keepdims=True))
        a = (1.0 / jnp.sum(e1, axis=-1, keepdims=True)).astype(BF16)
        c = (lam / jnp.sum(e2, axis=-1, keepdims=True)).astype(BF16)
        w = e1.astype(BF16) * a - e2.astype(BF16) * c
        o = _dot(w, v_scr[:, h * vw:h * vw + DIFF_VD])
        o_ref[0, r0:r1, h * LANES:(h + 1) * LANES] = (_rms(o, sg_ref[0]) * (1.0 - lam_init)).astype(BF16)

    _scores_one_ahead(_head_row_units(DIFF_HEADS, q_ref.shape[1]), scores, consume)


def _mla_body(*refs, n_src):
    q_ref = refs[0]
    k_srcs = refs[1:1 + n_src]
    v_srcs = refs[1 + n_src:1 + 2 * n_src]
    o_ref, k_scr, v_scr = refs[1 + 2 * n_src:]

    @pl.when(pl.program_id(1) == 0)
    def _():
        _fill_kv(k_srcs, v_srcs, k_scr, v_scr, MLA_HEADS, MLA_VD)

    vw = v_scr.shape[1] // MLA_HEADS

    def scores(item):
        h, r0, r1 = item
        sl = slice(h * LANES, (h + 1) * LANES)
        return _qk(q_ref[0, r0:r1, sl], k_scr[:, sl])

    def consume(item, s):
        h, r0, r1 = item
        o, l = _softmax_pv(s, v_scr[:, h * vw:(h + 1) * vw], MLA_VD)
        o_ref[0, r0:r1, h * MLA_VD:(h + 1) * MLA_VD] = (o * (1.0 / l)).astype(BF16)

    _scores_one_ahead(_head_row_units(MLA_HEADS, q_ref.shape[1]), scores, consume)


def _ctx_rows_spec(a, b, n_ctx):
    return pl.BlockSpec((1, n_ctx, a.shape[2]), lambda i, j: (b, i, 0))


def _attn_call(body, name, q, k, v, extra_specs, extra, heads, v_aug_w, out_w, b, n_lat, n_ctx, tq=1024):
    lat_rows = lambda a: pl.BlockSpec((1, n_lat, a.shape[2]), lambda i, j: (i, 0, 0))
    tq = _tile(n_lat, tq)
    return pl.pallas_call(
        functools.partial(body, n_src=2),
        grid=(b, n_lat // tq),
        in_specs=([pl.BlockSpec((1, tq, q.shape[2]), lambda i, j: (i, j, 0))] + extra_specs
                  + [_ctx_rows_spec(k, b, n_ctx), lat_rows(k), _ctx_rows_spec(v, b, n_ctx), lat_rows(v)]),
        out_specs=pl.BlockSpec((1, tq, out_w), lambda i, j: (i, j, 0)),
        out_shape=jax.ShapeDtypeStruct((b, n_lat, out_w), BF16),
        scratch_shapes=[pltpu.VMEM((n_ctx + n_lat, k.shape[2]), BF16),
                        pltpu.VMEM((n_ctx + n_lat, heads * v_aug_w), BF16)],
        compiler_params=_params(2),
        name=name,
    )(q, *extra, k, k, v, v)


def _ctx_attn_body(qd_ref, lam_ref, sg_ref, kd_ref, vd_ref, qm_ref, km_ref, vm_ref, od_ref, om_ref,
                   kd_scr, vd_scr, km_scr, vm_scr, *, lam_init, n_ctx):
    for i in range(qd_ref.shape[1] // n_ctx):
        rows = (slice(None), slice(i * n_ctx, (i + 1) * n_ctx), slice(None))
        _diff_body(qd_ref.at[rows], lam_ref, sg_ref, kd_ref.at[rows], vd_ref.at[rows], od_ref.at[rows],
                   kd_scr.at[i], vd_scr.at[i], n_src=1, lam_init=lam_init)
        _mla_body(qm_ref.at[rows], km_ref.at[rows], vm_ref.at[rows], om_ref.at[rows],
                  km_scr.at[i], vm_scr.at[i], n_src=1)


def _ctx_attn_call(lam_init, diff_extra_specs, diff_extra, qd, kd, vd, qm, km, vm, b, n_ctx):
    seg = lambda a: pl.BlockSpec((1, b * n_ctx, a.shape[2]), lambda i, j: (b, 0, 0))
    out = lambda w: pl.BlockSpec((1, b * n_ctx, w), lambda i, j: (0, 0, 0))
    return pl.pallas_call(
        functools.partial(_ctx_attn_body, lam_init=lam_init, n_ctx=n_ctx),
        grid=(1, 1),
        in_specs=[seg(qd)] + diff_extra_specs + [seg(kd), seg(vd), seg(qm), seg(km), seg(vm)],
        out_specs=[out(DIFF_W), out(MLA_V_W)],
        out_shape=[jax.ShapeDtypeStruct((1, b * n_ctx, DIFF_W), BF16),
                   jax.ShapeDtypeStruct((1, b * n_ctx, MLA_V_W), BF16)],
        scratch_shapes=[pltpu.VMEM((b, n_ctx, kd.shape[2]), BF16), pltpu.VMEM((b, n_ctx, DIFF_HEADS * MXU_W), BF16),
                        pltpu.VMEM((b, n_ctx, km.shape[2]), BF16), pltpu.VMEM((b, n_ctx, MLA_HEADS * LANES), BF16)],
        compiler_params=_params(2),
        name="ctx_attn",
    )(qd, *diff_extra, kd, vd, qm, km, vm)


def _dft_body(t_ref, zc_ref, zs_ref, o_ref):
    w = o_ref.shape[-1]
    even = _dot(t_ref[0], zc_ref[0, :, :w]) + _dot(t_ref[1], zs_ref[0, :, :w])
    odd = _dot(t_ref[2], zc_ref[0, :, w:]) + _dot(t_ref[3], zs_ref[0, :, w:])
    o_ref[0, 0] = (even + odd).astype(BF16)
    o_ref[0, 1] = (even - odd).astype(BF16)


def _dft_call(tables, zc, zs, b, latent, tm=1024):
    half, w = tables.shape[1], zc.shape[2] // 2
    t = 2 * half
    tm = _tile(half, tm)
    z_spec = pl.BlockSpec((1, half, 2 * w), (lambda i, j: (j, 0, 0)) if latent else (lambda i, j: (b, j, 0)))
    out = pl.pallas_call(
        _dft_body,
        grid=(half // tm, b),
        in_specs=[pl.BlockSpec((4, tm, half), lambda i, j: (0, i, 0)), z_spec, z_spec],
        out_specs=pl.BlockSpec((1, 2, tm, w), lambda i, j: (j, 0, i, 0)),
        out_shape=jax.ShapeDtypeStruct((b, 2, half, w), BF16),
        compiler_params=_params(2),
        name="pos_dft",
    )(tables, zc, zs)
    return out.reshape(b, t, w)


def _merge_body(x_ref, mod_ref, gt_ref, *rest, n_lat_seg):
    if n_lat_seg is None:
        (yd_ref, ym_ref, yf_ref), rest = rest[:3], rest[3:]
        yd, ym, yf = yd_ref[0], ym_ref[0], yf_ref[0]
    else:
        ys, rest = rest[:6], rest[6:]
        yd, ym, yf = (_pick(n_lat_seg, ys[2 * i], ys[2 * i + 1]) for i in range(N_BRANCHES))
    wd_ref, wm_ref, wf_ref, wo_ref, o_ref = rest
    x = x_ref[0]
    d = x.shape[-1]
    merged = (gt_ref[0, :, 0:d].astype(F32) * _dot(yd, wd_ref[0])
              + gt_ref[0, :, d:2 * d].astype(F32) * _dot(ym, wm_ref[0])
              + gt_ref[0, :, 2 * d:3 * d].astype(F32) * _dot(yf, wf_ref[0]))
    o_ref[0] = x + mod_ref[0, 0, 5:6, :] * _dot(merged.astype(BF16), wo_ref[0])


def _merge_call(x, mods, l, gt, y_lat, y_ctx, wd, wm, wf, wo, tm=512):
    b, t = y_lat[0].shape[:2]
    d = x.shape[2]
    tm = _tile(t, tm)
    tok = lambda wd_: pl.BlockSpec((1, tm, wd_), lambda i, j: (i, j, 0))
    if y_ctx is None:
        n_seg, y_specs, y_args = b, [tok(a.shape[2]) for a in y_lat], list(y_lat)
    else:
        n_seg, y_specs, y_args = b + 1, [], []
        for yl, yc in zip(y_lat, y_ctx):
            y_specs += list(_lat_or_ctx_specs(tm, yl.shape[2], b, t // tm))
            y_args += [yl, yc]
    return pl.pallas_call(
        functools.partial(_merge_body, n_lat_seg=None if y_ctx is None else b),
        grid=(n_seg, t // tm),
        in_specs=[tok(d), _mods_block(mods, l), tok(gt.shape[2])] + y_specs
                 + [_layer(wd, l), _layer(wm, l), _layer(wf, l), _layer(wo, l)],
        out_specs=tok(d),
        out_shape=jax.ShapeDtypeStruct((n_seg, t, d), F32),
        compiler_params=_params(2),
        name="merge_out",
    )(x, mods, gt, *y_args, wd, wm, wf, wo)


def _rope_tables(n_lat, n_ident):
    rows = np.arange(n_lat) // GRID_W
    cols = np.arange(n_lat) % GRID_W

    def cos_sin(dim):
        nf = dim // 4
        freqs = np.power(ROPE_BASE, -np.arange(nf, dtype=np.float64) / nf)
        ar = rows.astype(np.float64)[:, None] * freqs[None, :]
        ac = cols.astype(np.float64)[:, None] * freqs[None, :]
        ang = np.concatenate([ar, ar, ac, ac], axis=-1)
        sign = np.concatenate([-np.ones(nf), np.ones(nf), -np.ones(nf), np.ones(nf)])
        return np.cos(ang), np.sin(ang) * sign[None, :]

    cd, sd = cos_sin(DIFF_HD)
    cos_d = np.tile(cd, (1, LANES // DIFF_HD))
    sin_d = np.tile(sd, (1, LANES // DIFF_HD))
    cm, sm = cos_sin(MLA_ROPE)
    cos_m = np.ones((n_lat, LANES))
    sin_m = np.zeros((n_lat, LANES))
    cos_m[:, :MLA_ROPE] = cm
    sin_m[:, :MLA_ROPE] = sm
    lat = np.stack([cos_d, sin_d, cos_m, sin_m])
    ident = np.stack([np.ones((n_ident, LANES)), np.zeros((n_ident, LANES))] * 2)
    return jnp.asarray(np.concatenate([lat, ident], axis=1), F32)


def _angles(rows, cols, n):
    return 2.0 * np.pi * ((rows[:, None] * cols[None, :]) % n) / n


def _pos_dft_tables(n, group):
    j, m = np.arange(n // 2), np.arange(n // 2)
    scale = 1.0 / math.sqrt(n * group)
    ae, ao = _angles(j, 2 * m, n), _angles(j, 2 * m + 1, n)
    return jnp.asarray(np.stack([np.cos(ae), -np.sin(ae), np.cos(ao), -np.sin(ao)]) * scale, F32).astype(BF16)


def _chan_dft_table(group):
    a = _angles(np.arange(group), np.arange(group), group)
    return jnp.asarray(np.concatenate([np.cos(a), np.sin(a)], axis=1), F32).astype(BF16)


def _proj_weight_body(*refs):
    *w_refs, o_ref = refs
    blk = w_refs[0].shape[1]
    for k, w_ref in enumerate(w_refs):
        o_ref[0, :, k * blk:(k + 1) * blk] = w_ref[0].T.astype(BF16)


def _proj_weight_call(w_t, n_head, tail_start, blk=MXU_W, per_step=4):
    depth, cols, d = w_t.shape
    head_blocks, tail_blocks = n_head // blk, (cols - tail_start) // blk
    assert n_head % blk == 0 and (cols - tail_start) % blk == 0 and tail_start % 32 == 0
    assert (head_blocks + tail_blocks) % per_step == 0

    def window(k):
        def index(l, j):
            m = j * per_step + k
            return l, pl.multiple_of(jnp.where(m < head_blocks, m * blk, tail_start + (m - head_blocks) * blk), 32), 0
        return pl.BlockSpec((pl.Element(1), pl.Element(blk), pl.Element(d)), index)

    return pl.pallas_call(
        _proj_weight_body,
        grid=(depth, (head_blocks + tail_blocks) // per_step),
        in_specs=[window(k) for k in range(per_step)],
        out_specs=pl.BlockSpec((1, d, per_step * blk), lambda l, j: (l, 0, j)),
        out_shape=jax.ShapeDtypeStruct((depth, d, (head_blocks + tail_blocks) * blk), BF16),
        compiler_params=_params(2),
        name="proj_weight",
    )(*([w_t] * per_step))


def _prep_mla_weights(mla_w_uq, mla_w_ukv):
    depth = mla_w_uq.shape[0]
    tail = LANES - MLA_NOPE - MLA_ROPE
    uq = mla_w_uq.reshape(depth, MLA_Q_RANK, MLA_HEADS, MLA_NOPE + MLA_ROPE)
    uq = jnp.concatenate([uq[..., MLA_NOPE:], uq[..., :MLA_NOPE], jnp.zeros(uq.shape[:3] + (tail,), uq.dtype)], axis=-1)
    uq = uq.reshape(depth, MLA_Q_RANK, MLA_QK_W).astype(BF16)
    ukv = mla_w_ukv.reshape(depth, MLA_KV_RANK, MLA_HEADS, MLA_NOPE + MLA_VD)
    kn = jnp.pad(ukv[..., :MLA_NOPE], ((0, 0), (0, 0), (0, 0), (MLA_ROPE, tail)))
    kn = kn.reshape(depth, MLA_KV_RANK, MLA_QK_W)
    vv = ukv[..., MLA_NOPE:].reshape(depth, MLA_KV_RANK, MLA_V_W)
    return uq, jnp.concatenate([kn, vv], axis=2).astype(BF16)


def kernel(x, c, ctx, c_ctx, ada_w, ada_b, norm_g, ffn_w_in, ffn_w_out, w_in, diff_lambda,
           diff_subln_g, mla_q_norm_g, mla_w_uq, mla_kv_norm_g, mla_w_ukv, w_branch_diff,
           w_branch_mla, w_branch_fourier, w_out, final_norm_g):
    b, n_lat, d = x.shape
    n_ctx = ctx.shape[1]
    depth = ada_w.shape[0]
    assert b * n_ctx == n_lat, "context tokens of all batches must fill exactly one latent-length segment"

    cond_rows = -(-(b + 1) // 8) * 8
    cond = jnp.concatenate([c, c_ctx[None, :], jnp.zeros((cond_rows - b - 1, d), F32)], axis=0)
    mods = _ada_call(cond, ada_w, ada_b).reshape(depth, cond_rows, N_MOD, d)

    tab = _rope_tables(n_lat, PROJ_TILE)
    dft_x = _pos_dft_tables(n_lat, FOURIER_GROUP_DIM)
    dft_c = _pos_dft_tables(n_ctx, FOURIER_GROUP_DIM)
    cs128 = _chan_dft_table(FOURIER_GROUP_DIM)

    w_proj = _proj_weight_call(jnp.swapaxes(w_in, 1, 2), PROJ_WA_COLS, PROJ_KR_COL + MLA_ROPE)
    uq, ukv = _prep_mla_weights(mla_w_uq, mla_w_ukv)
    wd, wm, wf, wo = w_branch_diff, w_branch_mla, w_branch_fourier, w_out
    qn, kvn, sub_g = (a[:, None, :] for a in (mla_q_norm_g, mla_kv_norm_g, diff_subln_g))
    final_g = final_norm_g[None, :]
    flat = lambda a: a.reshape(1, b * n_ctx, a.shape[-1])

    h, h_ctx = x, ctx.reshape(1, b * n_ctx, d)
    for l in range(depth):
        last = l == depth - 1
        lam_init = 0.8 - 0.6 * math.exp(-0.3 * l)
        diff = functools.partial(_diff_body, lam_init=lam_init)
        diff_extra = ([_layer(diff_lambda, l), _layer(sub_g, l)], [diff_lambda, sub_g])
        sizes = (b, n_lat, n_ctx)

        h = _ffn_call(h, mods, l, norm_g, 0, ffn_w_in, ffn_w_out, 0, 0, ctx=h_ctx)
        h_ctx = None
        qd, kd, vd, qm, km, vm, zc, zs, gt = _proj_call(h, mods, l, norm_g, w_proj, uq, ukv, qn, kvn, cs128, tab, b)

        y_lat = (_attn_call(diff, "diff_attn", qd, kd, vd, *diff_extra, DIFF_HEADS, MXU_W, DIFF_W, *sizes, tq=512),
                 _attn_call(_mla_body, "mla_attn", qm, km, vm, [], [], MLA_HEADS, LANES, MLA_V_W, *sizes),
                 _dft_call(dft_x, zc, zs, b, True))
        y_ctx = None
        if not last:
            yd_c, ym_c = _ctx_attn_call(lam_init, *diff_extra, qd, kd, vd, qm, km, vm, b, n_ctx)
            y_ctx = tuple(flat(a) for a in (yd_c, ym_c, _dft_call(dft_c, zc, zs, b, False)))
        h = _merge_call(h, mods, l, gt, y_lat, y_ctx, wd, wm, wf, wo)
        h = _ffn_call(h, mods, l, norm_g, 2, ffn_w_in, ffn_w_out, 1, 6, final_g=final_g if last else None)
    return h
```
